```python
import math
import jax
import jax.numpy as jnp
from jax import lax
import numpy as np

D_MODEL = 1024
BATCH = 2
SEQ = 8192
DEPTH = 2

EPS = 1e-6
ROPE_THETA = 10000.0
Q_BLOCK = 128

A_HEADS = 4
A_DK = 64
A_DV = 2 * A_DK
A_QK_W = A_HEADS * 2 * A_DK
A_V_W = A_HEADS * A_DV

B_PAIRS = ((128, 1), (512, 4), (2048, 16))
B_GROUPS = len(B_PAIRS)
B_HEADS = 4
B_DH = 64
B_W = B_GROUPS * B_HEADS * B_DH
B_OUT_W = B_HEADS * B_DH

IN_SPLITS = [A_QK_W, A_QK_W, A_V_W, B_W, B_W, B_W, D_MODEL, D_MODEL]
IN_W = sum(IN_SPLITS)

N_GROUPS = 4
EXPERTS_PER_GROUP = 8
N_EXPERTS = N_GROUPS * EXPERTS_PER_GROUP
TOP_K = 2
D_EXPERT = 256

kernel_name = 'gated_hybrid_diffattn_dilated_hmoe_adaln'


def rms_norm(x, gain=None):
    xf = x.astype(jnp.float32)
    y = xf * lax.rsqrt(jnp.mean(xf * xf, axis=-1, keepdims=True) + EPS)
    if gain is not None:
        y = y * gain.astype(jnp.float32)
    return y.astype(x.dtype)


def rope_tables(positions, dim):
    inv_freq = ROPE_THETA ** (-jnp.arange(0, dim, 2, dtype=jnp.float32) / dim)
    ang = positions.astype(jnp.float32)[..., None] * inv_freq
    return jnp.cos(ang)[:, :, None, :], jnp.sin(ang)[:, :, None, :]


def apply_rope(x, cos, sin):
    xf = x.astype(jnp.float32)
    x1, x2 = jnp.split(xf, 2, axis=-1)
    out = jnp.concatenate([x1 * cos - x2 * sin, x2 * cos + x1 * sin], axis=-1)
    return out.astype(x.dtype)


def differential_attention(q, k, v, lam, lam_init, subln):
    bsz, seq = q.shape[0], q.shape[1]
    n_blocks = seq // Q_BLOCK
    q_blocks = q.reshape(bsz, n_blocks, Q_BLOCK, A_HEADS, 2, A_DK).transpose(1, 0, 2, 3, 4, 5)
    k = k.reshape(bsz, seq, A_HEADS, 2, A_DK)
    scale = A_DK ** -0.5

    def one_block(qb):
        s = jnp.einsum('bqhmd,bkhmd->bhmqk', qb, k).astype(jnp.float32) * scale
        p = jax.nn.softmax(s, axis=-1)
        a = p[:, :, 0] - lam * p[:, :, 1]
        return jnp.einsum('bhqk,bkhd->bqhd', a.astype(v.dtype), v)

    o = lax.map(one_block, q_blocks)
    o = o.transpose(1, 0, 2, 3, 4).reshape(bsz, seq, A_HEADS, A_DV)
    o = rms_norm(o, subln) * (1.0 - lam_init)
    return o.reshape(bsz, seq, A_V_W)


def dilated_window_attention(q, k, v, window, dilation):
    bsz, seq, heads, dh = q.shape
    radius = window // (2 * dilation)
    length = seq // dilation
    n = bsz * dilation

    def to_strided(t):
        return t.reshape(bsz, length, dilation, heads, dh).transpose(0, 2, 1, 3, 4).reshape(n, length, heads, dh)

    qs, ks, vs = to_strided(q), to_strided(k), to_strided(v)
    nb = -(-length // radius)
    lp = nb * radius
    qs = jnp.pad(qs, ((0, 0), (0, lp - length), (0, 0), (0, 0))).reshape(n, nb, radius, heads, dh)

    def windows(t):
        tp = jnp.pad(t, ((0, 0), (radius, lp - length + radius), (0, 0), (0, 0)))
        tp = tp.reshape(n, nb + 2, radius, heads, dh)
        return jnp.concatenate([tp[:, :-2], tp[:, 1:-1], tp[:, 2:]], axis=2)

    kw, vw = windows(ks), windows(vs)
    blk = jnp.arange(nb)[:, None] * radius
    qpos = blk + jnp.arange(radius)[None, :]
    kpos = blk - radius + jnp.arange(3 * radius)[None, :]
    rel = kpos[:, None, :] - qpos[:, :, None]
    valid = (jnp.abs(rel) <= radius) & (kpos[:, None, :] >= 0) & (kpos[:, None, :] < length)

    s = jnp.einsum('nbqhd,nbkhd->nbhqk', qs, kw).astype(jnp.float32) * (dh ** -0.5)
    s = jnp.where(valid[None, :, None], s, -jnp.inf)
    m = jnp.max(s, axis=-1, keepdims=True)
    e = jnp.exp(s - m)
    l = jnp.sum(e, axis=-1, keepdims=True)
    o = jnp.einsum('nbhqk,nbkhd->nbqhd', (e / l).astype(v.dtype), vw)
    lse = (m + jnp.log(l))[..., 0].transpose(0, 1, 3, 2)

    o = o.reshape(n, lp, heads, dh)[:, :length]
    lse = lse.reshape(n, lp, heads)[:, :length]
    o = o.reshape(bsz, dilation, length, heads, dh).transpose(0, 2, 1, 3, 4).reshape(bsz, seq, heads, dh)
    lse = lse.reshape(bsz, dilation, length, heads).transpose(0, 2, 1, 3).reshape(bsz, seq, heads)
    return o, lse


def token_mixer(h, cos, sin, lam, lam_init, w_in, qn_a, kn_a, subln_a, qn_b, kn_b, w_pa, w_pb, w_o):
    bsz, seq, _ = h.shape
    proj = h @ w_in
    cuts = [int(t) for t in np.cumsum(IN_SPLITS)[:-1]]
    qa, ka, va, qb, kb, vb, ga, gb = jnp.split(proj, cuts, axis=-1)

    qa = apply_rope(rms_norm(qa.reshape(bsz, seq, 2 * A_HEADS, A_DK), qn_a), cos, sin)
    ka = apply_rope(rms_norm(ka.reshape(bsz, seq, 2 * A_HEADS, A_DK), kn_a), cos, sin)
    va = va.reshape(bsz, seq, A_HEADS, A_DV)
    out_a = differential_attention(qa, ka, va, lam, lam_init, subln_a)

    qb = apply_rope(rms_norm(qb.reshape(bsz, seq, B_GROUPS * B_HEADS, B_DH), qn_b), cos, sin)
    kb = apply_rope(rms_norm(kb.reshape(bsz, seq, B_GROUPS * B_HEADS, B_DH), kn_b), cos, sin)
    qb = qb.reshape(bsz, seq, B_GROUPS, B_HEADS, B_DH)
    kb = kb.reshape(bsz, seq, B_GROUPS, B_HEADS, B_DH)
    vb = vb.reshape(bsz, seq, B_GROUPS, B_HEADS, B_DH)
    outs, lses = [], []
    for g, (window, dilation) in enumerate(B_PAIRS):
        o_g, lse_g = dilated_window_attention(qb[:, :, g], kb[:, :, g], vb[:, :, g], window, dilation)
        outs.append(o_g)
        lses.append(lse_g)
    wts = jax.nn.softmax(jnp.stack(lses, axis=-1), axis=-1)
    out_b = jnp.sum(jnp.stack(outs, axis=-2) * wts[..., None].astype(h.dtype), axis=-2)
    out_b = out_b.reshape(bsz, seq, B_OUT_W)

    merged = jax.nn.sigmoid(ga) * (out_a @ w_pa) + jax.nn.sigmoid(gb) * (out_b @ w_pb)
    return merged @ w_o


def hierarchical_moe(h, w_r1, b_r1, w_r2, b_r2, w_e_gate, w_e_up, w_e_down):
    bsz, seq, d = h.shape
    t = h.reshape(bsz * seq, d)
    p_group = jax.nn.softmax((t @ w_r1 + b_r1).astype(jnp.float32), axis=-1)
    g_val, g_idx = lax.top_k(p_group, 1)
    logits = (t @ w_r2 + b_r2).astype(jnp.float32).reshape(-1, N_GROUPS, EXPERTS_PER_GROUP)
    in_group = logits[jnp.arange(logits.shape[0]), g_idx[:, 0]]
    top_l, top_i = lax.top_k(in_group, TOP_K)
    weights = g_val * jax.nn.softmax(top_l, axis=-1)
    expert_id = g_idx * EXPERTS_PER_GROUP + top_i
    combine = jnp.einsum('tk,tke->te', weights,
                         jax.nn.one_hot(expert_id, N_EXPERTS, dtype=jnp.float32)).astype(h.dtype)
    y = jnp.zeros_like(t)
    for e in range(N_EXPERTS):
        hid = jax.nn.silu(t @ w_e_gate[e]) * (t @ w_e_up[e])
        y = y + combine[:, e:e + 1] * (hid @ w_e_down[e])
    return y.reshape(bsz, seq, d)


def setup_inputs(seed: int = 0) -> dict:
    key = jax.random.key(seed)
    ks = jax.random.split(key, 26)

    def dense(k, shape, fan_in, gain=1.0):
        return jax.random.normal(k, shape, jnp.float32) * (gain * fan_in ** -0.5)

    def near_one(k, shape):
        return 1.0 + 0.02 * jax.random.normal(k, shape, jnp.float32)

    def small(k, shape, s):
        return s * jax.random.normal(k, shape, jnp.float32)

    offsets = jax.random.randint(ks[2], (BATCH, 1), 0, 4096, dtype=jnp.int32)
    positions = (offsets + jnp.arange(SEQ, dtype=jnp.int32)[None, :]).astype(jnp.int32)
    return {
        'x': jax.random.normal(ks[0], (BATCH, SEQ, D_MODEL), jnp.float32),
        'c': jax.random.normal(ks[1], (BATCH, D_MODEL), jnp.float32),
        'positions': positions,
        'w_ada': dense(ks[3], (DEPTH, D_MODEL, 6 * D_MODEL), D_MODEL, 0.5),
        'b_ada': small(ks[4], (DEPTH, 6 * D_MODEL), 0.01),
        'w_in': dense(ks[5], (DEPTH, D_MODEL, IN_W), D_MODEL),
        'qn_a': near_one(ks[6], (DEPTH, A_DK)),
        'kn_a': near_one(ks[7], (DEPTH, A_DK)),
        'lam_q1': small(ks[8], (DEPTH, A_DK), 0.1),
        'lam_k1': small(ks[9], (DEPTH, A_DK), 0.1),
        'lam_q2': small(ks[10], (DEPTH, A_DK), 0.1),
        'lam_k2': small(ks[11], (DEPTH, A_DK), 0.1),
        'subln_a': near_one(ks[12], (DEPTH, A_DV)),
        'qn_b': near_one(ks[13], (DEPTH, B_DH)),
        'kn_b': near_one(ks[14], (DEPTH, B_DH)),
        'w_pa': dense(ks[15], (DEPTH, A_V_W, D_MODEL), A_V_W),
        'w_pb': dense(ks[16], (DEPTH, B_OUT_W, D_MODEL), B_OUT_W),
        'w_o': dense(ks[17], (DEPTH, D_MODEL, D_MODEL), D_MODEL),
        'w_r1': dense(ks[18], (DEPTH, D_MODEL, N_GROUPS), D_MODEL),
        'b_r1': small(ks[19], (DEPTH, N_GROUPS), 0.01),
        'w_r2': dense(ks[20], (DEPTH, D_MODEL, N_EXPERTS), D_MODEL),
        'b_r2': small(ks[21], (DEPTH, N_EXPERTS), 0.01),
        'w_e_gate': dense(ks[22], (DEPTH, N_EXPERTS, D_MODEL, D_EXPERT), D_MODEL),
        'w_e_up': dense(ks[23], (DEPTH, N_EXPERTS, D_MODEL, D_EXPERT), D_MODEL),
        'w_e_down': dense(ks[24], (DEPTH, N_EXPERTS, D_EXPERT, D_MODEL), D_EXPERT),
    }


def reference(x, c, positions, w_ada, b_ada, w_in, qn_a, kn_a, lam_q1, lam_k1, lam_q2, lam_k2,
              subln_a, qn_b, kn_b, w_pa, w_pb, w_o, w_r1, b_r1, w_r2, b_r2,
              w_e_gate, w_e_up, w_e_down):
    cos, sin = rope_tables(positions, A_DK)
    c_act = jax.nn.silu(c)
    for layer in range(DEPTH):
        mod = (c_act @ w_ada[layer] + b_ada[layer])[:, None, :]
        shift1, scale1, gate1, shift2, scale2, gate2 = jnp.split(mod, 6, axis=-1)
        lam_init = 0.8 - 0.6 * math.exp(-0.3 * layer)
        lam = (jnp.exp(jnp.sum(lam_q1[layer].astype(jnp.float32) * lam_k1[layer].astype(jnp.float32)))
               - jnp.exp(jnp.sum(lam_q2[layer].astype(jnp.float32) * lam_k2[layer].astype(jnp.float32)))
               + lam_init)

        h = rms_norm(x) * (1 + scale1) + shift1
        x = x + gate1 * token_mixer(h, cos, sin, lam, lam_init, w_in[layer], qn_a[layer], kn_a[layer],
                                    subln_a[layer], qn_b[layer], kn_b[layer],
                                    w_pa[layer], w_pb[layer], w_o[layer])

        h = rms_norm(x) * (1 + scale2) + shift2
        x = x + gate2 * hierarchical_moe(h, w_r1[layer], b_r1[layer], w_r2[layer], b_r2[layer],
                                         w_e_gate[layer], w_e_up[layer], w_e_down[layer])
    return x
```

```python
import functools
import math

import jax
import jax.numpy as jnp
from jax import lax
from jax.experimental import pallas as pl
from jax.experimental.pallas import tpu as pltpu

EPS = 1e-6
ROPE_THETA = 10000.0
LOG2E = math.log2(math.e)
LN2 = math.log(2.0)

A_HEADS = 4
HEAD_DIM = 64
LANES = 128
B_PAIRS = ((128, 1), (512, 4), (2048, 16))
B_GROUPS = len(B_PAIRS)
B_GROUP_W = 256
N_GROUPS = 4
EXPERTS_PER_GROUP = 8
N_EXPERTS = N_GROUPS * EXPERTS_PER_GROUP
ROUTER_W = 128

F32 = jnp.float32
BF16 = jnp.bfloat16
HIGHEST = lax.Precision.HIGHEST


def _dot(a, b):
    return jnp.dot(a, b, preferred_element_type=F32)


def _dot_nt(a, b):
    return lax.dot_general(a, b, (((1,), (1,)), ((), ())), preferred_element_type=F32)


def _sigmoid(x):
    return 1.0 / (1.0 + jnp.exp(-x))


def _ada_kernel(c_ref, w_ref, b_ref, o_ref):
    c = c_ref[...]
    c_act = c * _sigmoid(c)
    o_ref[0] = jnp.dot(c_act, w_ref[0], precision=HIGHEST, preferred_element_type=F32) + b_ref[0]


def _ada(c, w_ada, b_ada):
    depth, d, six_d = w_ada.shape
    bsz = c.shape[0]
    n_col = six_d // d
    return pl.pallas_call(
        _ada_kernel,
        grid=(depth, n_col),
        in_specs=[
            pl.BlockSpec((bsz, d), lambda l, j: (0, 0)),
            pl.BlockSpec((1, d, d), lambda l, j: (l, 0, j)),
            pl.BlockSpec((1, 1, d), lambda l, j: (l, 0, j)),
        ],
        out_specs=pl.BlockSpec((1, bsz, d), lambda l, j: (l, 0, j)),
        out_shape=jax.ShapeDtypeStruct((depth, bsz, six_d), F32),
        name="ada_mod",
    )(c, w_ada, b_ada.reshape(depth, 1, six_d))


def _rope_kernel(pos_ref, f_ref, cos_ref, sin_ref):
    ang = pos_ref[0].astype(F32) * f_ref[...]
    cos_ref[0] = jnp.cos(ang)
    sin_ref[0] = jnp.sin(ang)


def _rope_tables(positions):
    bsz, seq = positions.shape
    half = HEAD_DIM // 2
    inv_freq = ROPE_THETA ** (-jnp.arange(0, HEAD_DIM, 2, dtype=F32) / HEAD_DIM)
    cos_t, sin_t = pl.pallas_call(
        _rope_kernel,
        grid=(bsz,),
        in_specs=[
            pl.BlockSpec((1, 1, seq), lambda b: (b, 0, 0)),
            pl.BlockSpec((half, 1), lambda b: (0, 0)),
        ],
        out_specs=[pl.BlockSpec((1, half, seq), lambda b: (b, 0, 0))] * 2,
        out_shape=[jax.ShapeDtypeStruct((bsz, half, seq), F32)] * 2,
        name="rope_tables",
    )(positions.reshape(bsz, 1, seq), inv_freq.reshape(half, 1))
    cos = cos_t.transpose(0, 2, 1)
    sin = sin_t.transpose(0, 2, 1)
    cos_l = jnp.concatenate([cos, cos, cos, cos], axis=-1)
    sin_l = jnp.concatenate([-sin, sin, -sin, sin], axis=-1)
    return cos_l, sin_l


def _inproj_kernel(x_ref, mod_ref, cos_ref, sin_ref, gain_ref, wqk_ref, wv_ref, wg_ref, seg_ref,
                   qa_ref, ka_ref, qb_ref, kb_ref, va_ref, vb_ref, ga_ref, gb_ref):
    x = x_ref[0]
    ms = jnp.mean(x * x, axis=-1, keepdims=True)
    h = x * lax.rsqrt(ms + EPS) * (1.0 + mod_ref[0, 1:2, :]) + mod_ref[0, 0:1, :]
    hb = h.astype(BF16)

    cos = cos_ref[0]
    sin = sin_ref[0]
    seg = seg_ref[...]
    lane = lax.broadcasted_iota(jnp.int32, cos.shape, 1)
    first_half = (lane % HEAD_DIM) < (HEAD_DIM // 2)

    qk_outs = ((qa_ref, 4), (ka_ref, 4), (qb_ref, 6), (kb_ref, 6))
    dests = []
    for ref, n in qk_outs:
        dests += [(ref, t) for t in range(n)]
    n_qk = len(dests)
    for c in range(n_qk // 2):
        y2 = _dot(hb, wqk_ref[:, c * 256:(c + 1) * 256])
        for half in range(2):
            j = 2 * c + half
            y = y2[:, half * LANES:(half + 1) * LANES]
            sq = y * y
            hi = sq.astype(BF16)
            lo = (sq - hi.astype(F32)).astype(BF16)
            msq = _dot(hi, seg) + _dot(lo, seg)
            yn = y * lax.rsqrt(msq + EPS) * gain_ref[:, j * LANES:(j + 1) * LANES]
            partner = jnp.where(first_half, pltpu.roll(yn, 96, axis=1), pltpu.roll(yn, 32, axis=1))
            ref, t = dests[j]
            ref[0, :, t * LANES:(t + 1) * LANES] = (yn * cos + partner * sin).astype(BF16)

    v_dests = [(va_ref, 0), (va_ref, 1), (vb_ref, 0), (vb_ref, 1), (vb_ref, 2)]
    for c, (ref, t) in enumerate(v_dests):
        ref[0, :, t * 256:(t + 1) * 256] = _dot(hb, wv_ref[:, c * 256:(c + 1) * 256]).astype(BF16)

    g_dests = [(ga_ref, t) for t in range(4)] + [(gb_ref, t) for t in range(4)]
    for c, (ref, t) in enumerate(g_dests):
        g = _dot(hb, wg_ref[:, c * 256:(c + 1) * 256])
        ref[0, :, t * 256:(t + 1) * 256] = _sigmoid(g).astype(BF16)


def _inproj(x, mod_l, cos_l, sin_l, gain, wqk, wv, wg, seg, tm):
    bsz, seq, d = x.shape
    widths = (512, 512, 768, 768, 512, 768, 1024, 1024)
    tok = lambda w: pl.BlockSpec((1, tm, w), lambda b, i: (b, i, 0))
    const = lambda shape: pl.BlockSpec(shape, lambda b, i: (0,) * len(shape),
                                       pipeline_mode=pl.Buffered(1))
    return pl.pallas_call(
        _inproj_kernel,
        grid=(bsz, seq // tm),
        in_specs=[
            tok(d),
            pl.BlockSpec((1, 6, d), lambda b, i: (b, 0, 0)),
            tok(LANES), tok(LANES),
            const(gain.shape), const(wqk.shape), const(wv.shape), const(wg.shape), const(seg.shape),
        ],
        out_specs=[tok(w) for w in widths],
        out_shape=[jax.ShapeDtypeStruct((bsz, seq, w), BF16) for w in widths],
        compiler_params=pltpu.CompilerParams(dimension_semantics=("parallel", "parallel")),
        name="in_proj",
    )(x, mod_l, cos_l, sin_l, gain, wqk, wv, wg, seg)


def _attn_a_kernel(lam_ref, sub_ref, q_ref, k_ref, v_ref, o_ref, m_ref, l_ref, acc_ref, *, tk, lam_init):
    q = q_ref[0]
    seq = k_ref.shape[1]
    lane = lax.broadcasted_iota(jnp.int32, q.shape, 1)
    zero = jnp.zeros_like(q)
    q_maps = (jnp.where(lane < HEAD_DIM, q, zero), jnp.where(lane >= HEAD_DIM, q, zero))

    m_ref[...] = jnp.full(m_ref.shape, -jnp.inf, F32)
    l_ref[...] = jnp.zeros(l_ref.shape, F32)
    acc_ref[...] = jnp.zeros(acc_ref.shape, F32)

    def body(j, carry):
        start = pl.multiple_of(j * tk, tk)
        k = k_ref[0, pl.ds(start, tk), :]
        v = v_ref[0, pl.ds(start, tk), :]
        for mi in range(2):
            s = _dot_nt(q_maps[mi], k)
            m_old = m_ref[mi]
            m_new = jnp.maximum(m_old, jnp.max(s, axis=-1, keepdims=True))
            alpha = jnp.exp2(m_old - m_new)
            p = jnp.exp2(s - m_new)
            l_ref[mi] = alpha * l_ref[mi] + jnp.sum(p, axis=-1, keepdims=True)
            acc_ref[mi] = alpha * acc_ref[mi] + _dot(p.astype(BF16), v)
            m_ref[mi] = m_new
        return carry

    lax.fori_loop(0, seq // tk, body, 0)

    lam_p = lam_ref[...]
    s1 = jnp.sum(lam_p[0:1] * lam_p[1:2], axis=-1, keepdims=True)
    s2 = jnp.sum(lam_p[2:3] * lam_p[3:4], axis=-1, keepdims=True)
    lam = jnp.exp(s1) - jnp.exp(s2) + lam_init
    o = acc_ref[0] / l_ref[0] - lam * (acc_ref[1] / l_ref[1])
    msq = jnp.mean(o * o, axis=-1, keepdims=True)
    o = o * lax.rsqrt(msq + EPS) * sub_ref[...] * (1.0 - lam_init)
    o_ref[0] = o.astype(BF16)


def _attn_a(lam_p, subln, qa, ka, va, lam_init, tq, tk):
    bsz, seq, _ = qa.shape
    return pl.pallas_call(
        functools.partial(_attn_a_kernel, tk=tk, lam_init=lam_init),
        grid=(bsz, A_HEADS, seq // tq),
        in_specs=[
            pl.BlockSpec(lam_p.shape, lambda b, h, i: (0, 0)),
            pl.BlockSpec(subln.shape, lambda b, h, i: (0, 0)),
            pl.BlockSpec((1, tq, LANES), lambda b, h, i: (b, i, h)),
            pl.BlockSpec((1, seq, LANES), lambda b, h, i: (b, 0, h)),
            pl.BlockSpec((1, seq, LANES), lambda b, h, i: (b, 0, h)),
        ],
        out_specs=pl.BlockSpec((1, tq, LANES), lambda b, h, i: (b, i, h)),
        out_shape=jax.ShapeDtypeStruct((bsz, seq, A_HEADS * LANES), BF16),
        scratch_shapes=[
            pltpu.VMEM((2, tq, 1), F32),
            pltpu.VMEM((2, tq, 1), F32),
            pltpu.VMEM((2, tq, LANES), F32),
        ],
        compiler_params=pltpu.CompilerParams(dimension_semantics=("parallel", "parallel", "parallel")),
        name="diff_attn",
    )(lam_p, subln, qa, ka, va)


def _attn_b_kernel(q_ref, k_ref, v_ref, o_ref, lse_ref, *, radius):
    tq = q_ref.shape[1]
    length = k_ref.shape[1]
    win = tq + 2 * radius
    t0 = pl.program_id(2) * tq
    start = pl.multiple_of(jnp.clip(t0 - radius, 0, length - win), radius)
    kw = k_ref[0, pl.ds(start, win), :]
    vw = v_ref[0, pl.ds(start, win), :]
    q = q_ref[0]

    qpos = t0 + lax.broadcasted_iota(jnp.int32, (tq, win), 0)
    kpos = start + lax.broadcasted_iota(jnp.int32, (tq, win), 1)
    valid = jnp.abs(kpos - qpos) <= radius
    lane = lax.broadcasted_iota(jnp.int32, (tq, LANES), 1)
    low = lane < HEAD_DIM

    for c in range(B_GROUP_W // LANES):
        qc = q[:, c * LANES:(c + 1) * LANES]
        kc = kw[:, c * LANES:(c + 1) * LANES]
        vc = vw[:, c * LANES:(c + 1) * LANES]
        zero = jnp.zeros_like(qc)
        outs, lses = [], []
        for half in range(2):
            qm = jnp.where(low if half == 0 else jnp.logical_not(low), qc, zero)
            s = jnp.where(valid, _dot_nt(qm, kc), -jnp.inf)
            m = jnp.max(s, axis=-1, keepdims=True)
            p = jnp.exp2(s - m)
            l = jnp.sum(p, axis=-1, keepdims=True)
            outs.append(_dot(p.astype(BF16), vc) / l)
            lses.append((m + jnp.log2(l)) * LN2)
        o_ref[0, :, c * LANES:(c + 1) * LANES] = jnp.where(low, outs[0], outs[1]).astype(BF16)
        lse_ref[0, :, c * LANES:(c + 1) * LANES] = jnp.where(low, lses[0], lses[1])


def _attn_b(qb, kb, vb, group, tq):
    window, dilation = B_PAIRS[group]
    radius = window // (2 * dilation)
    bsz, seq, width = qb.shape
    length = seq // dilation
    n_col = width // B_GROUP_W
    view = lambda a: a.reshape(bsz, length, dilation * width)
    col = lambda r: r * n_col + group
    o, lse = pl.pallas_call(
        functools.partial(_attn_b_kernel, radius=radius),
        grid=(bsz, dilation, length // tq),
        in_specs=[
            pl.BlockSpec((1, tq, B_GROUP_W), lambda b, r, i: (b, i, col(r))),
            pl.BlockSpec((1, length, B_GROUP_W), lambda b, r, i: (b, 0, col(r))),
            pl.BlockSpec((1, length, B_GROUP_W), lambda b, r, i: (b, 0, col(r))),
        ],
        out_specs=[pl.BlockSpec((1, tq, B_GROUP_W), lambda b, r, i: (b, i, r))] * 2,
        out_shape=[jax.ShapeDtypeStruct((bsz, length, dilation * B_GROUP_W), BF16),
                   jax.ShapeDtypeStruct((bsz, length, dilation * B_GROUP_W), F32)],
        compiler_params=pltpu.CompilerParams(dimension_semantics=("parallel", "parallel", "parallel")),
        name=f"band_attn_g{group}",
    )(view(qb), view(kb), view(vb))
    return o.reshape(bsz, seq, B_GROUP_W), lse.reshape(bsz, seq, B_GROUP_W)


def _merge_kernel(x_ref, mod_ref, oa_ref, ob0_ref, ls0_ref, ob1_ref, ls1_ref, ob2_ref, ls2_ref,
                  ga_ref, gb_ref, wpa_ref, wpb_ref, wo_ref, wr_ref, br_ref,
                  x1_ref, h2_ref, comb_ref):
    ls0, ls1, ls2 = ls0_ref[0], ls1_ref[0], ls2_ref[0]
    mx = jnp.maximum(jnp.maximum(ls0, ls1), ls2)
    e0, e1, e2 = jnp.exp(ls0 - mx), jnp.exp(ls1 - mx), jnp.exp(ls2 - mx)
    ob = (e0 * ob0_ref[0].astype(F32) + e1 * ob1_ref[0].astype(F32) + e2 * ob2_ref[0].astype(F32)) / (e0 + e1 + e2)

    pa = _dot(oa_ref[0], wpa_ref[...])
    pb = _dot(ob.astype(BF16), wpb_ref[...])
    merged = ga_ref[0].astype(F32) * pa + gb_ref[0].astype(F32) * pb
    y = _dot(merged.astype(BF16), wo_ref[...])
    x1 = x_ref[0] + mod_ref[0, 2:3, :] * y
    x1_ref[0] = x1

    ms = jnp.mean(x1 * x1, axis=-1, keepdims=True)
    h2 = x1 * lax.rsqrt(ms + EPS) * (1.0 + mod_ref[0, 4:5, :]) + mod_ref[0, 3:4, :]
    h2_ref[0] = h2.astype(BF16)

    logits = jnp.dot(h2, wr_ref[...], precision=HIGHEST, preferred_element_type=F32) + br_ref[...]
    lane = lax.broadcasted_iota(jnp.int32, logits.shape, 1)
    neg = -jnp.inf
    big = ROUTER_W
    is_grp = (lane >= N_EXPERTS) & (lane < N_EXPERTS + N_GROUPS)
    lg = jnp.where(is_grp, logits, neg)
    mg = jnp.max(lg, axis=-1, keepdims=True)
    g_lane = jnp.min(jnp.where(lg == mg, lane, big), axis=-1, keepdims=True)
    g_val = 1.0 / jnp.sum(jnp.exp(lg - mg), axis=-1, keepdims=True)
    lo = (g_lane - N_EXPERTS) * EXPERTS_PER_GROUP
    in_grp = (lane >= lo) & (lane < lo + EXPERTS_PER_GROUP)
    le = jnp.where(in_grp, logits, neg)
    m1 = jnp.max(le, axis=-1, keepdims=True)
    i1 = jnp.min(jnp.where(le == m1, lane, big), axis=-1, keepdims=True)
    le2 = jnp.where(lane == i1, neg, le)
    m2 = jnp.max(le2, axis=-1, keepdims=True)
    i2 = jnp.min(jnp.where(le2 == m2, lane, big), axis=-1, keepdims=True)
    e = jnp.exp(m2 - m1)
    w1 = g_val / (1.0 + e)
    w2 = g_val * e / (1.0 + e)
    comb_ref[0] = jnp.where(lane == i1, w1, 0.0) + jnp.where(lane == i2, w2, 0.0)


def _merge(x, mod_l, oa, obs, ga, gb, wpa, wpb, wo, wr, br, tm):
    bsz, seq, d = x.shape
    tok = lambda w: pl.BlockSpec((1, tm, w), lambda b, i: (b, i, 0))
    const = lambda shape: pl.BlockSpec(shape, lambda b, i: (0,) * len(shape),
                                       pipeline_mode=pl.Buffered(1))
    ob_args, ob_specs = [], []
    for o, lse in obs:
        ob_args += [o, lse]
        ob_specs += [tok(B_GROUP_W), tok(B_GROUP_W)]
    return pl.pallas_call(
        _merge_kernel,
        grid=(bsz, seq // tm),
        in_specs=[tok(d), pl.BlockSpec((1, 6, d), lambda b, i: (b, 0, 0)), tok(oa.shape[-1])]
                 + ob_specs + [tok(d), tok(d)]
                 + [const(w.shape) for w in (wpa, wpb, wo, wr, br)],
        out_specs=[tok(d), tok(d), tok(ROUTER_W)],
        out_shape=[jax.ShapeDtypeStruct((bsz, seq, d), F32),
                   jax.ShapeDtypeStruct((bsz, seq, d), BF16),
                   jax.ShapeDtypeStruct((bsz, seq, ROUTER_W), F32)],
        compiler_params=pltpu.CompilerParams(dimension_semantics=("parallel", "parallel")),
        name="merge_proj",
    )(x, mod_l, oa, *ob_args, ga, gb, wpa, wpb, wo, wr, br)


def _moe_kernel(x1_ref, mod_ref, h_ref, comb_ref, wg_ref, wu_ref, wd_ref, o_ref, acc_ref):
    e = pl.program_id(2)

    @pl.when(e == 0)
    def _():
        acc_ref[...] = jnp.zeros(acc_ref.shape, F32)

    h = h_ref[0]
    a = _dot(h, wg_ref[0])
    u = _dot(h, wu_ref[0])
    hid = (a * _sigmoid(a)) * u
    comb = comb_ref[0]
    lane = lax.broadcasted_iota(jnp.int32, comb.shape, 1)
    w = jnp.sum(jnp.where(lane == e, comb, 0.0), axis=-1, keepdims=True)
    acc_ref[...] += w * _dot(hid.astype(BF16), wd_ref[0])

    @pl.when(e == pl.num_programs(2) - 1)
    def _():
        o_ref[0] = x1_ref[0] + mod_ref[0, 5:6, :] * acc_ref[...]


def _moe_dense(x1, mod_l, h2, comb, weg, weu, wed, tm):
    bsz, seq, d = x1.shape
    n_e, _, d_e = weg.shape
    tok = lambda w: pl.BlockSpec((1, tm, w), lambda b, i, e: (b, i, 0))
    return pl.pallas_call(
        _moe_kernel,
        grid=(bsz, seq // tm, n_e),
        in_specs=[tok(d), pl.BlockSpec((1, 6, d), lambda b, i, e: (b, 0, 0)), tok(d), tok(ROUTER_W),
                  pl.BlockSpec((1, d, d_e), lambda b, i, e: (e, 0, 0)),
                  pl.BlockSpec((1, d, d_e), lambda b, i, e: (e, 0, 0)),
                  pl.BlockSpec((1, d_e, d), lambda b, i, e: (e, 0, 0))],
        out_specs=tok(d),
        out_shape=jax.ShapeDtypeStruct((bsz, seq, d), F32),
        scratch_shapes=[pltpu.VMEM((tm, d), F32)],
        compiler_params=pltpu.CompilerParams(dimension_semantics=("parallel", "parallel", "arbitrary")),
        name="moe_dense",
    )(x1, mod_l, h2, comb, weg, weu, wed)


def _tiles(seq):
    return dict(
        tm_proj=min(512, seq),
        tq_a=min(256, seq),
        tk_a=min(512, seq),
        tq_b=128,
        tm_merge=min(512, seq),
        tm_moe=min(1024, seq),
    )


def kernel(x, c, positions, w_ada, b_ada, w_in, qn_a, kn_a, lam_q1, lam_k1, lam_q2, lam_k2,
           subln_a, qn_b, kn_b, w_pa, w_pb, w_o, w_r1, b_r1, w_r2, b_r2,
           w_e_gate, w_e_up, w_e_down):
    depth = w_ada.shape[0]
    bsz, seq, d = x.shape
    t = _tiles(seq)

    mod = _ada(c, w_ada, b_ada).reshape(depth, bsz, 6, d)
    cos_l, sin_l = _rope_tables(positions)
    seg = jnp.kron(jnp.eye(LANES // HEAD_DIM, dtype=F32),
                   jnp.full((HEAD_DIM, HEAD_DIM), 1.0 / HEAD_DIM, F32)).astype(BF16)
    q_scale = HEAD_DIM ** -0.5 * LOG2E

    for layer in range(depth):
        lam_init = 0.8 - 0.6 * math.exp(-0.3 * layer)
        w = w_in[layer]
        wqk = jnp.concatenate([w[:, 0:1024], w[:, 1536:3072]], axis=1).astype(BF16)
        wv = jnp.concatenate([w[:, 1024:1536], w[:, 3072:3840]], axis=1).astype(BF16)
        wg = w[:, 3840:].astype(BF16)
        gain = jnp.concatenate([
            jnp.tile(qn_a[layer] * q_scale, 8), jnp.tile(kn_a[layer], 8),
            jnp.tile(qn_b[layer] * q_scale, 12), jnp.tile(kn_b[layer], 12)]).reshape(1, -1)

        qa, ka, qb, kb, va, vb, ga, gb = _inproj(x, mod[layer], cos_l, sin_l, gain, wqk, wv, wg, seg,
                                                 t["tm_proj"])

        lam_p = jnp.stack([lam_q1[layer], lam_k1[layer], lam_q2[layer], lam_k2[layer]])
        oa = _attn_a(lam_p, subln_a[layer].reshape(1, -1), qa, ka, va, lam_init, t["tq_a"], t["tk_a"])
        obs = [_attn_b(qb, kb, vb, g, t["tq_b"]) for g in range(B_GROUPS)]

        wr = jnp.zeros((d, ROUTER_W), F32)
        wr = wr.at[:, :N_EXPERTS].set(w_r2[layer]).at[:, N_EXPERTS:N_EXPERTS + N_GROUPS].set(w_r1[layer])
        br = jnp.zeros((1, ROUTER_W), F32)
        br = br.at[0, :N_EXPERTS].set(b_r2[layer]).at[0, N_EXPERTS:N_EXPERTS + N_GROUPS].set(b_r1[layer])
        x1, h2, comb = _merge(x, mod[layer], oa, obs, ga, gb,
                              w_pa[layer].astype(BF16), w_pb[layer].astype(BF16), w_o[layer].astype(BF16),
                              wr, br, t["tm_merge"])

        x = _moe_dense(x1, mod[layer], h2, comb,
                       w_e_gate[layer].astype(BF16), w_e_up[layer].astype(BF16),
                       w_e_down[layer].astype(BF16), t["tm_moe"])
    return x
```

```python
import functools
import math

import jax
import jax.numpy as jnp
from jax import lax
from jax.experimental import pallas as pl
from jax.experimental.pallas import tpu as pltpu

EPS = 1e-6
ROPE_THETA = 10000.0
LOG2E = math.log2(math.e)
LN2 = math.log(2.0)

A_HEADS = 4
HEAD_DIM = 64
LANES = 128
B_PAIRS = ((128, 1), (512, 4), (2048, 16))
B_GROUPS = len(B_PAIRS)
B_GROUP_W = 256
N_GROUPS = 4
EXPERTS_PER_GROUP = 8
N_EXPERTS = N_GROUPS * EXPERTS_PER_GROUP
ROUTER_W = 128

F32 = jnp.float32
BF16 = jnp.bfloat16
HIGHEST = lax.Precision.HIGHEST


def _dot(a, b):
    return jnp.dot(a, b, preferred_element_type=F32)


def _dot_nt(a, b):
    return lax.dot_general(a, b, (((1,), (1,)), ((), ())), preferred_element_type=F32)


def _sigmoid(x):
    return 1.0 / (1.0 + jnp.exp(-x))


def _ada_kernel(c_ref, w_ref, b_ref, o_ref):
    c = c_ref[...]
    c_act = c * _sigmoid(c)
    o_ref[0] = jnp.dot(c_act, w_ref[0], precision=HIGHEST, preferred_element_type=F32) + b_ref[0]


def _ada(c, w_ada, b_ada):
    depth, d, six_d = w_ada.shape
    bsz = c.shape[0]
    n_col = six_d // d
    return pl.pallas_call(
        _ada_kernel,
        grid=(depth, n_col),
        in_specs=[
            pl.BlockSpec((bsz, d), lambda l, j: (0, 0)),
            pl.BlockSpec((1, d, d), lambda l, j: (l, 0, j)),
            pl.BlockSpec((1, 1, d), lambda l, j: (l, 0, j)),
        ],
        out_specs=pl.BlockSpec((1, bsz, d), lambda l, j: (l, 0, j)),
        out_shape=jax.ShapeDtypeStruct((depth, bsz, six_d), F32),
        name="ada_mod",
    )(c, w_ada, b_ada.reshape(depth, 1, six_d))


def _rope_kernel(pos_ref, f_ref, cos_ref, sin_ref):
    ang = pos_ref[0].astype(F32) * f_ref[...]
    cos_ref[0] = jnp.cos(ang)
    sin_ref[0] = jnp.sin(ang)


def _rope_tables(positions):
    bsz, seq = positions.shape
    half = HEAD_DIM // 2
    inv_freq = ROPE_THETA ** (-jnp.arange(0, HEAD_DIM, 2, dtype=F32) / HEAD_DIM)
    cos_t, sin_t = pl.pallas_call(
        _rope_kernel,
        grid=(bsz,),
        in_specs=[
            pl.BlockSpec((1, 1, seq), lambda b: (b, 0, 0)),
            pl.BlockSpec((half, 1), lambda b: (0, 0)),
        ],
        out_specs=[pl.BlockSpec((1, half, seq), lambda b: (b, 0, 0))] * 2,
        out_shape=[jax.ShapeDtypeStruct((bsz, half, seq), F32)] * 2,
        name="rope_tables",
    )(positions.reshape(bsz, 1, seq), inv_freq.reshape(half, 1))
    cos = cos_t.transpose(0, 2, 1)
    sin = sin_t.transpose(0, 2, 1)
    cos_l = jnp.concatenate([cos, cos, cos, cos], axis=-1)
    sin_l = jnp.concatenate([-sin, sin, -sin, sin], axis=-1)
    return cos_l, sin_l


def _inproj_kernel(x_ref, mod_ref, cos_ref, sin_ref, gain_ref, wqk_ref, wvat_ref, wvb_ref, wg_ref, seg_ref,
                   qa_ref, ka_ref, qb_ref, kb_ref, vat_ref, vb_ref, ga_ref, gb_ref):
    x = x_ref[0]
    ms = jnp.mean(x * x, axis=-1, keepdims=True)
    h = x * lax.rsqrt(ms + EPS) * (1.0 + mod_ref[0, 1:2, :]) + mod_ref[0, 0:1, :]
    hb = h.astype(BF16)

    cos = cos_ref[0]
    sin = sin_ref[0]
    seg = seg_ref[...]
    lane = lax.broadcasted_iota(jnp.int32, cos.shape, 1)
    first_half = (lane % HEAD_DIM) < (HEAD_DIM // 2)

    qk_outs = ((qa_ref, 4), (ka_ref, 4), (qb_ref, 6), (kb_ref, 6))
    dests = []
    for ref, n in qk_outs:
        dests += [(ref, t) for t in range(n)]
    n_qk = len(dests)
    for c in range(n_qk // 2):
        y2 = _dot(hb, wqk_ref[:, c * 256:(c + 1) * 256])
        for half in range(2):
            j = 2 * c + half
            y = y2[:, half * LANES:(half + 1) * LANES]
            sq = y * y
            hi = sq.astype(BF16)
            lo = (sq - hi.astype(F32)).astype(BF16)
            msq = _dot(hi, seg) + _dot(lo, seg)
            yn = y * lax.rsqrt(msq + EPS) * gain_ref[:, j * LANES:(j + 1) * LANES]
            partner = jnp.where(first_half, pltpu.roll(yn, 96, axis=1), pltpu.roll(yn, 32, axis=1))
            ref, t = dests[j]
            ref[0, :, t * LANES:(t + 1) * LANES] = (yn * cos + partner * sin).astype(BF16)

    for c in range(wvat_ref.shape[0] // 256):
        vat_ref[0, c * 256:(c + 1) * 256, :] = _dot_nt(wvat_ref[c * 256:(c + 1) * 256, :], hb).astype(BF16)
    for c in range(wvb_ref.shape[1] // 256):
        vb_ref[0, :, c * 256:(c + 1) * 256] = _dot(hb, wvb_ref[:, c * 256:(c + 1) * 256]).astype(BF16)

    g_dests = [(ga_ref, t) for t in range(4)] + [(gb_ref, t) for t in range(4)]
    for c, (ref, t) in enumerate(g_dests):
        g = _dot(hb, wg_ref[:, c * 256:(c + 1) * 256])
        ref[0, :, t * 256:(t + 1) * 256] = _sigmoid(g).astype(BF16)


def _inproj(x, mod_l, cos_l, sin_l, gain, wqk, wvat, wvb, wg, seg, tm):
    bsz, seq, d = x.shape
    a_v_w = wvat.shape[0]
    tok = lambda w: pl.BlockSpec((1, tm, w), lambda b, i: (b, i, 0))
    tok_t = pl.BlockSpec((1, a_v_w, tm), lambda b, i: (b, 0, i))
    const = lambda shape: pl.BlockSpec(shape, lambda b, i: (0,) * len(shape),
                                       pipeline_mode=pl.Buffered(1))
    row = lambda w: jax.ShapeDtypeStruct((bsz, seq, w), BF16)
    return pl.pallas_call(
        _inproj_kernel,
        grid=(bsz, seq // tm),
        in_specs=[
            tok(d),
            pl.BlockSpec((1, 6, d), lambda b, i: (b, 0, 0)),
            tok(LANES), tok(LANES),
            const(gain.shape), const(wqk.shape), const(wvat.shape), const(wvb.shape), const(wg.shape),
            const(seg.shape),
        ],
        out_specs=[tok(512), tok(512), tok(768), tok(768), tok_t, tok(768), tok(d), tok(d)],
        out_shape=[row(512), row(512), row(768), row(768),
                   jax.ShapeDtypeStruct((bsz, a_v_w, seq), BF16), row(768), row(d), row(d)],
        compiler_params=pltpu.CompilerParams(dimension_semantics=("parallel", "parallel")),
        name="in_proj",
    )(x, mod_l, cos_l, sin_l, gain, wqk, wvat, wvb, wg, seg)


SCORE_BOUND_NO_SHIFT = 64.0
KV_CHUNK = 256


def _attn_a_kernel(bound_ref, lam_ref, sub_ref, q_ref, k_ref, vt_ref, o_ref,
                   m_ref, l_ref, acc_ref, pa_ref, pb_ref, *, tk, lam_init):
    q = q_ref[0]
    tq = q.shape[0]
    seq = k_ref.shape[1]
    lane = lax.broadcasted_iota(jnp.int32, q.shape, 1)
    zero = jnp.zeros_like(q)
    q_maps = (jnp.where(lane < HEAD_DIM, q, zero), jnp.where(lane >= HEAD_DIM, q, zero))
    n_chunk = tk // KV_CHUNK

    acc_ref[...] = jnp.zeros(acc_ref.shape, F32)
    l_ref[...] = jnp.zeros(l_ref.shape, F32)

    def scores_exp(tile, p_ref):
        start = pl.multiple_of(tile * tk, tk)
        k = k_ref[0, pl.ds(start, tk), :]
        for mi in range(2):
            p = jnp.exp2(_dot_nt(k, q_maps[mi]))
            l_ref[mi] += jnp.sum(p.reshape(tk // 8, 8, tq), axis=0)
            p_ref[mi] = p.astype(BF16)

    def weighted_values(tile, p_ref):
        start = pl.multiple_of(tile * tk, tk)
        vt = vt_ref[0, :, pl.ds(start, tk)]
        for mi in range(2):
            acc_ref[mi] += _dot(vt, p_ref[mi])

    n_tiles = seq // tk

    def no_shift_pair(jj, carry):
        t = 2 * jj
        scores_exp(t + 1, pb_ref)
        weighted_values(t, pa_ref)
        scores_exp(t + 2, pa_ref)
        weighted_values(t + 1, pb_ref)
        return carry

    def no_shift_loop():
        scores_exp(0, pa_ref)
        lax.fori_loop(0, n_tiles // 2 - 1, no_shift_pair, 0)
        scores_exp(n_tiles - 1, pb_ref)
        weighted_values(n_tiles - 2, pa_ref)
        weighted_values(n_tiles - 1, pb_ref)

    def online_max_body(j, carry):
        for c in range(n_chunk):
            start = pl.multiple_of(j * tk + c * KV_CHUNK, KV_CHUNK)
            k = k_ref[0, pl.ds(start, KV_CHUNK), :]
            vt = vt_ref[0, :, pl.ds(start, KV_CHUNK)]
            for mi in range(2):
                s = _dot_nt(k, q_maps[mi])
                m_old = m_ref[mi]
                m_new = jnp.maximum(m_old, jnp.max(s, axis=0, keepdims=True))
                alpha = jnp.exp2(m_old - m_new)
                p = jnp.exp2(s - m_new[0:1])
                l_ref[mi] = alpha * l_ref[mi] + jnp.sum(p.reshape(KV_CHUNK // 8, 8, tq), axis=0)
                acc_ref[mi] = alpha[0:1] * acc_ref[mi] + _dot(vt, p.astype(BF16))
                m_ref[mi] = m_new
        return carry

    no_shift = bound_ref[0] <= SCORE_BOUND_NO_SHIFT

    @pl.when(no_shift)
    def _():
        no_shift_loop()

    @pl.when(jnp.logical_not(no_shift))
    def _():
        m_ref[...] = jnp.full(m_ref.shape, -jnp.inf, F32)
        lax.fori_loop(0, seq // tk, online_max_body, 0)

    lam_p = lam_ref[...]
    s1 = jnp.sum(lam_p[0:1] * lam_p[1:2], axis=-1, keepdims=True)
    s2 = jnp.sum(lam_p[2:3] * lam_p[3:4], axis=-1, keepdims=True)
    lam = jnp.exp(s1) - jnp.exp(s2) + lam_init
    l0 = jnp.sum(l_ref[0], axis=0, keepdims=True)
    l1 = jnp.sum(l_ref[1], axis=0, keepdims=True)
    ot = acc_ref[0] / l0 - lam * (acc_ref[1] / l1)
    o = ot.T
    msq = jnp.mean(o * o, axis=-1, keepdims=True)
    o = o * lax.rsqrt(msq + EPS) * sub_ref[...] * (1.0 - lam_init)
    o_ref[0] = o.astype(BF16)


def _attn_a(score_bound, lam_p, subln, qa, ka, vat, lam_init, tq, tk):
    bsz, seq, _ = qa.shape
    return pl.pallas_call(
        functools.partial(_attn_a_kernel, tk=tk, lam_init=lam_init),
        grid=(bsz, A_HEADS, seq // tq),
        in_specs=[
            pl.BlockSpec(memory_space=pltpu.SMEM),
            pl.BlockSpec(lam_p.shape, lambda b, h, i: (0, 0)),
            pl.BlockSpec(subln.shape, lambda b, h, i: (0, 0)),
            pl.BlockSpec((1, tq, LANES), lambda b, h, i: (b, i, h)),
            pl.BlockSpec((1, seq, LANES), lambda b, h, i: (b, 0, h)),
            pl.BlockSpec((1, LANES, seq), lambda b, h, i: (b, h, 0)),
        ],
        out_specs=pl.BlockSpec((1, tq, LANES), lambda b, h, i: (b, i, h)),
        out_shape=jax.ShapeDtypeStruct((bsz, seq, A_HEADS * LANES), BF16),
        scratch_shapes=[
            pltpu.VMEM((2, 8, tq), F32),
            pltpu.VMEM((2, 8, tq), F32),
            pltpu.VMEM((2, LANES, tq), F32),
            pltpu.VMEM((2, tk, tq), BF16),
            pltpu.VMEM((2, tk, tq), BF16),
        ],
        compiler_params=pltpu.CompilerParams(dimension_semantics=("parallel", "parallel", "parallel")),
        name="diff_attn",
    )(score_bound, lam_p, subln, qa, ka, vat)


def _attn_b_kernel(q_ref, k_ref, v_ref, o_ref, lse_ref, *, radius):
    tq = q_ref.shape[1]
    length = k_ref.shape[1]
    win = tq + 2 * radius
    t0 = pl.program_id(2) * tq
    start = pl.multiple_of(jnp.clip(t0 - radius, 0, length - win), radius)
    kw = k_ref[0, pl.ds(start, win), :]
    vw = v_ref[0, pl.ds(start, win), :]
    q = q_ref[0]

    qpos = t0 + lax.broadcasted_iota(jnp.int32, (tq, win), 0)
    kpos = start + lax.broadcasted_iota(jnp.int32, (tq, win), 1)
    valid = jnp.abs(kpos - qpos) <= radius
    lane = lax.broadcasted_iota(jnp.int32, (tq, LANES), 1)
    low = lane < HEAD_DIM

    for c in range(B_GROUP_W // LANES):
        qc = q[:, c * LANES:(c + 1) * LANES]
        kc = kw[:, c * LANES:(c + 1) * LANES]
        vc = vw[:, c * LANES:(c + 1) * LANES]
        zero = jnp.zeros_like(qc)
        outs, lses = [], []
        for half in range(2):
            qm = jnp.where(low if half == 0 else jnp.logical_not(low), qc, zero)
            s = jnp.where(valid, _dot_nt(qm, kc), -jnp.inf)
            m = jnp.max(s, axis=-1, keepdims=True)
            p = jnp.exp2(s - m)
            l = jnp.sum(p, axis=-1, keepdims=True)
            outs.append(_dot(p.astype(BF16), vc) / l)
            lses.append((m + jnp.log2(l)) * LN2)
        o_ref[0, :, c * LANES:(c + 1) * LANES] = jnp.where(low, outs[0], outs[1]).astype(BF16)
        lse_ref[0, :, c * LANES:(c + 1) * LANES] = jnp.where(low, lses[0], lses[1])


def _attn_b(qb, kb, vb, group, tq):
    window, dilation = B_PAIRS[group]
    radius = window // (2 * dilation)
    bsz, seq, width = qb.shape
    length = seq // dilation
    n_col = width // B_GROUP_W
    view = lambda a: a.reshape(bsz, length, dilation * width)
    col = lambda r: r * n_col + group
    o, lse = pl.pallas_call(
        functools.partial(_attn_b_kernel, radius=radius),
        grid=(bsz, dilation, length // tq),
        in_specs=[
            pl.BlockSpec((1, tq, B_GROUP_W), lambda b, r, i: (b, i, col(r))),
            pl.BlockSpec((1, length, B_GROUP_W), lambda b, r, i: (b, 0, col(r))),
            pl.BlockSpec((1, length, B_GROUP_W), lambda b, r, i: (b, 0, col(r))),
        ],
        out_specs=[pl.BlockSpec((1, tq, B_GROUP_W), lambda b, r, i: (b, i, r))] * 2,
        out_shape=[jax.ShapeDtypeStruct((bsz, length, dilation * B_GROUP_W), BF16),
                   jax.ShapeDtypeStruct((bsz, length, dilation * B_GROUP_W), F32)],
        compiler_params=pltpu.CompilerParams(dimension_semantics=("parallel", "parallel", "parallel")),
        name=f"band_attn_g{group}",
    )(view(qb), view(kb), view(vb))
    return o.reshape(bsz, seq, B_GROUP_W), lse.reshape(bsz, seq, B_GROUP_W)


def _merge_kernel(x_ref, mod_ref, oa_ref, ob0_ref, ls0_ref, ob1_ref, ls1_ref, ob2_ref, ls2_ref,
                  ga_ref, gb_ref, wpa_ref, wpb_ref, wo_ref, wr_ref, br_ref,
                  x1_ref, h2_ref, comb_ref):
    ls0, ls1, ls2 = ls0_ref[0], ls1_ref[0], ls2_ref[0]
    mx = jnp.maximum(jnp.maximum(ls0, ls1), ls2)
    e0, e1, e2 = jnp.exp(ls0 - mx), jnp.exp(ls1 - mx), jnp.exp(ls2 - mx)
    ob = (e0 * ob0_ref[0].astype(F32) + e1 * ob1_ref[0].astype(F32) + e2 * ob2_ref[0].astype(F32)) / (e0 + e1 + e2)

    pa = _dot(oa_ref[0], wpa_ref[...])
    pb = _dot(ob.astype(BF16), wpb_ref[...])
    merged = ga_ref[0].astype(F32) * pa + gb_ref[0].astype(F32) * pb
    y = _dot(merged.astype(BF16), wo_ref[...])
    x1 = x_ref[0] + mod_ref[0, 2:3, :] * y
    x1_ref[0] = x1

    ms = jnp.mean(x1 * x1, axis=-1, keepdims=True)
    h2 = x1 * lax.rsqrt(ms + EPS) * (1.0 + mod_ref[0, 4:5, :]) + mod_ref[0, 3:4, :]
    h2_ref[0] = h2.astype(BF16)

    logits = jnp.dot(h2, wr_ref[...], precision=HIGHEST, preferred_element_type=F32) + br_ref[...]
    lane = lax.broadcasted_iota(jnp.int32, logits.shape, 1)
    neg = -jnp.inf
    big = ROUTER_W
    is_grp = (lane >= N_EXPERTS) & (lane < N_EXPERTS + N_GROUPS)
    lg = jnp.where(is_grp, logits, neg)
    mg = jnp.max(lg, axis=-1, keepdims=True)
    g_lane = jnp.min(jnp.where(lg == mg, lane, big), axis=-1, keepdims=True)
    g_val = 1.0 / jnp.sum(jnp.exp(lg - mg), axis=-1, keepdims=True)
    lo = (g_lane - N_EXPERTS) * EXPERTS_PER_GROUP
    in_grp = (lane >= lo) & (lane < lo + EXPERTS_PER_GROUP)
    le = jnp.where(in_grp, logits, neg)
    m1 = jnp.max(le, axis=-1, keepdims=True)
    i1 = jnp.min(jnp.where(le == m1, lane, big), axis=-1, keepdims=True)
    le2 = jnp.where(lane == i1, neg, le)
    m2 = jnp.max(le2, axis=-1, keepdims=True)
    i2 = jnp.min(jnp.where(le2 == m2, lane, big), axis=-1, keepdims=True)
    e = jnp.exp(m2 - m1)
    w1 = g_val / (1.0 + e)
    w2 = g_val * e / (1.0 + e)
    comb_ref[0] = jnp.where(lane == i1, w1, 0.0) + jnp.where(lane == i2, w2, 0.0)


def _merge(x, mod_l, oa, obs, ga, gb, wpa, wpb, wo, wr, br, tm):
    bsz, seq, d = x.shape
    tok = lambda w: pl.BlockSpec((1, tm, w), lambda b, i: (b, i, 0))
    const = lambda shape: pl.BlockSpec(shape, lambda b, i: (0,) * len(shape),
                                       pipeline_mode=pl.Buffered(1))
    ob_args, ob_specs = [], []
    for o, lse in obs:
        ob_args += [o, lse]
        ob_specs += [tok(B_GROUP_W), tok(B_GROUP_W)]
    return pl.pallas_call(
        _merge_kernel,
        grid=(bsz, seq // tm),
        in_specs=[tok(d), pl.BlockSpec((1, 6, d), lambda b, i: (b, 0, 0)), tok(oa.shape[-1])]
                 + ob_specs + [tok(d), tok(d)]
                 + [const(w.shape) for w in (wpa, wpb, wo, wr, br)],
        out_specs=[tok(d), tok(d), tok(ROUTER_W)],
        out_shape=[jax.ShapeDtypeStruct((bsz, seq, d), F32),
                   jax.ShapeDtypeStruct((bsz, seq, d), BF16),
                   jax.ShapeDtypeStruct((bsz, seq, ROUTER_W), F32)],
        compiler_params=pltpu.CompilerParams(dimension_semantics=("parallel", "parallel")),
        name="merge_proj",
    )(x, mod_l, oa, *ob_args, ga, gb, wpa, wpb, wo, wr, br)


def _moe_kernel(x1_ref, mod_ref, h_ref, comb_ref, wg_ref, wu_ref, wd_ref, o_ref, acc_ref):
    e = pl.program_id(2)

    @pl.when(e == 0)
    def _():
        acc_ref[...] = jnp.zeros(acc_ref.shape, F32)

    h = h_ref[0]
    a = _dot(h, wg_ref[0])
    u = _dot(h, wu_ref[0])
    hid = (a * _sigmoid(a)) * u
    comb = comb_ref[0]
    lane = lax.broadcasted_iota(jnp.int32, comb.shape, 1)
    w = jnp.sum(jnp.where(lane == e, comb, 0.0), axis=-1, keepdims=True)
    acc_ref[...] += w * _dot(hid.astype(BF16), wd_ref[0])

    @pl.when(e == pl.num_programs(2) - 1)
    def _():
        o_ref[0] = x1_ref[0] + mod_ref[0, 5:6, :] * acc_ref[...]


def _moe_dense(x1, mod_l, h2, comb, weg, weu, wed, tm):
    bsz, seq, d = x1.shape
    n_e, _, d_e = weg.shape
    tok = lambda w: pl.BlockSpec((1, tm, w), lambda b, i, e: (b, i, 0))
    return pl.pallas_call(
        _moe_kernel,
        grid=(bsz, seq // tm, n_e),
        in_specs=[tok(d), pl.BlockSpec((1, 6, d), lambda b, i, e: (b, 0, 0)), tok(d), tok(ROUTER_W),
                  pl.BlockSpec((1, d, d_e), lambda b, i, e: (e, 0, 0)),
                  pl.BlockSpec((1, d, d_e), lambda b, i, e: (e, 0, 0)),
                  pl.BlockSpec((1, d_e, d), lambda b, i, e: (e, 0, 0))],
        out_specs=tok(d),
        out_shape=jax.ShapeDtypeStruct((bsz, seq, d), F32),
        scratch_shapes=[pltpu.VMEM((tm, d), F32)],
        compiler_params=pltpu.CompilerParams(dimension_semantics=("parallel", "parallel", "arbitrary")),
        name="moe_dense",
    )(x1, mod_l, h2, comb, weg, weu, wed)


def _tiles(seq):
    return dict(
        tm_proj=min(512, seq),
        tq_a=min(256, seq),
        tk_a=min(512, seq),
        tq_b=128,
        tm_merge=min(512, seq),
        tm_moe=min(1024, seq),
    )


def kernel(x, c, positions, w_ada, b_ada, w_in, qn_a, kn_a, lam_q1, lam_k1, lam_q2, lam_k2,
           subln_a, qn_b, kn_b, w_pa, w_pb, w_o, w_r1, b_r1, w_r2, b_r2,
           w_e_gate, w_e_up, w_e_down):
    depth = w_ada.shape[0]
    bsz, seq, d = x.shape
    t = _tiles(seq)

    mod = _ada(c, w_ada, b_ada).reshape(depth, bsz, 6, d)
    cos_l, sin_l = _rope_tables(positions)
    seg = jnp.kron(jnp.eye(LANES // HEAD_DIM, dtype=F32),
                   jnp.full((HEAD_DIM, HEAD_DIM), 1.0 / HEAD_DIM, F32)).astype(BF16)
    q_scale = HEAD_DIM ** -0.5 * LOG2E

    for layer in range(depth):
        lam_init = 0.8 - 0.6 * math.exp(-0.3 * layer)
        w = w_in[layer]
        wqk = jnp.concatenate([w[:, 0:1024], w[:, 1536:3072]], axis=1).astype(BF16)
        wvat = w[:, 1024:1536].T.astype(BF16)
        wvb = w[:, 3072:3840].astype(BF16)
        wg = w[:, 3840:].astype(BF16)
        gain = jnp.concatenate([
            jnp.tile(qn_a[layer] * q_scale, 8), jnp.tile(kn_a[layer], 8),
            jnp.tile(qn_b[layer] * q_scale, 12), jnp.tile(kn_b[layer], 12)]).reshape(1, -1)

        qa, ka, qb, kb, vat, vb, ga, gb = _inproj(x, mod[layer], cos_l, sin_l, gain, wqk, wvat, wvb, wg, seg,
                                                  t["tm_proj"])

        score_bound = (1.01 * HEAD_DIM * q_scale * jnp.max(jnp.abs(qn_a[layer]))
                       * jnp.max(jnp.abs(kn_a[layer]))).reshape(1)
        lam_p = jnp.stack([lam_q1[layer], lam_k1[layer], lam_q2[layer], lam_k2[layer]])
        oa = _attn_a(score_bound, lam_p, subln_a[layer].reshape(1, -1), qa, ka, vat, lam_init,
                     t["tq_a"], t["tk_a"])
        obs = [_attn_b(qb, kb, vb, g, t["tq_b"]) for g in range(B_GROUPS)]

        wr = jnp.zeros((d, ROUTER_W), F32)
        wr = wr.at[:, :N_EXPERTS].set(w_r2[layer]).at[:, N_EXPERTS:N_EXPERTS + N_GROUPS].set(w_r1[layer])
        br = jnp.zeros((1, ROUTER_W), F32)
        br = br.at[0, :N_EXPERTS].set(b_r2[layer]).at[0, N_EXPERTS:N_EXPERTS + N_GROUPS].set(b_r1[layer])
        x1, h2, comb = _merge(x, mod[layer], oa, obs, ga, gb,
                              w_pa[layer].astype(BF16), w_pb[layer].astype(BF16), w_o[layer].astype(BF16),
                              wr, br, t["tm_merge"])

        x = _moe_dense(x1, mod[layer], h2, comb,
                       w_e_gate[layer].astype(BF16), w_e_up[layer].astype(BF16),
                       w_e_down[layer].astype(BF16), t["tm_moe"])
    return x
```

```python
import functools
import math

import jax
import jax.numpy as jnp
from jax import lax
from jax.experimental import pallas as pl
from jax.experimental.pallas import tpu as pltpu

EPS = 1e-6
ROPE_THETA = 10000.0
LOG2E = math.log2(math.e)
LN2 = math.log(2.0)

A_HEADS = 4
HEAD_DIM = 64
LANES = 128
B_PAIRS = ((128, 1), (512, 4), (2048, 16))
B_GROUPS = len(B_PAIRS)
B_GROUP_W = 256
N_GROUPS = 4
EXPERTS_PER_GROUP = 8
N_EXPERTS = N_GROUPS * EXPERTS_PER_GROUP
ROUTER_W = 128

F32 = jnp.float32
BF16 = jnp.bfloat16
HIGHEST = lax.Precision.HIGHEST


def _dot(a, b):
    return jnp.dot(a, b, preferred_element_type=F32)


def _dot_nt(a, b):
    return lax.dot_general(a, b, (((1,), (1,)), ((), ())), preferred_element_type=F32)


def _sigmoid(x):
    return 1.0 / (1.0 + jnp.exp(-x))


def _ada_kernel(c_ref, w_ref, b_ref, o_ref):
    c = c_ref[...]
    c_act = c * _sigmoid(c)
    o_ref[0] = jnp.dot(c_act, w_ref[0], precision=HIGHEST, preferred_element_type=F32) + b_ref[0]


def _ada(c, w_ada, b_ada):
    depth, d, six_d = w_ada.shape
    bsz = c.shape[0]
    n_col = six_d // d
    return pl.pallas_call(
        _ada_kernel,
        grid=(depth, n_col),
        in_specs=[
            pl.BlockSpec((bsz, d), lambda l, j: (0, 0)),
            pl.BlockSpec((1, d, d), lambda l, j: (l, 0, j)),
            pl.BlockSpec((1, 1, d), lambda l, j: (l, 0, j)),
        ],
        out_specs=pl.BlockSpec((1, bsz, d), lambda l, j: (l, 0, j)),
        out_shape=jax.ShapeDtypeStruct((depth, bsz, six_d), F32),
        name="ada_mod",
    )(c, w_ada, b_ada.reshape(depth, 1, six_d))


def _rope_kernel(pos_ref, f_ref, cos_ref, sin_ref):
    ang = pos_ref[0].astype(F32) * f_ref[...]
    cos_ref[0] = jnp.cos(ang)
    sin_ref[0] = jnp.sin(ang)


def _rope_tables(positions):
    bsz, seq = positions.shape
    half = HEAD_DIM // 2
    inv_freq = ROPE_THETA ** (-jnp.arange(0, HEAD_DIM, 2, dtype=F32) / HEAD_DIM)
    cos_t, sin_t = pl.pallas_call(
        _rope_kernel,
        grid=(bsz,),
        in_specs=[
            pl.BlockSpec((1, 1, seq), lambda b: (b, 0, 0)),
            pl.BlockSpec((half, 1), lambda b: (0, 0)),
        ],
        out_specs=[pl.BlockSpec((1, half, seq), lambda b: (b, 0, 0))] * 2,
        out_shape=[jax.ShapeDtypeStruct((bsz, half, seq), F32)] * 2,
        name="rope_tables",
    )(positions.reshape(bsz, 1, seq), inv_freq.reshape(half, 1))
    cos = cos_t.transpose(0, 2, 1)
    sin = sin_t.transpose(0, 2, 1)
    cos_l = jnp.concatenate([cos, cos, cos, cos], axis=-1)
    sin_l = jnp.concatenate([-sin, sin, -sin, sin], axis=-1)
    return cos_l, sin_l


def _inproj_kernel(x_ref, mod_ref, cos_ref, sin_ref, gain_ref, wqk_ref, wvat_ref, wvb_ref, wg_ref, seg_ref,
                   qa_ref, ka_ref, qb_ref, kb_ref, vat_ref, vb_ref, ga_ref, gb_ref):
    x = x_ref[0]
    ms = jnp.mean(x * x, axis=-1, keepdims=True)
    h = x * lax.rsqrt(ms + EPS) * (1.0 + mod_ref[0, 1:2, :]) + mod_ref[0, 0:1, :]
    hb = h.astype(BF16)

    cos = cos_ref[0]
    sin = sin_ref[0]
    seg = seg_ref[...]
    lane = lax.broadcasted_iota(jnp.int32, cos.shape, 1)
    first_half = (lane % HEAD_DIM) < (HEAD_DIM // 2)

    qk_outs = ((qa_ref, 4), (ka_ref, 4), (qb_ref, 6), (kb_ref, 6))
    dests = []
    for ref, n in qk_outs:
        dests += [(ref, t) for t in range(n)]
    n_qk = len(dests)
    for c in range(n_qk // 2):
        y2 = _dot(hb, wqk_ref[:, c * 256:(c + 1) * 256])
        for half in range(2):
            j = 2 * c + half
            y = y2[:, half * LANES:(half + 1) * LANES]
            sq = y * y
            hi = sq.astype(BF16)
            lo = (sq - hi.astype(F32)).astype(BF16)
            msq = _dot(hi, seg) + _dot(lo, seg)
            yn = y * lax.rsqrt(msq + EPS) * gain_ref[:, j * LANES:(j + 1) * LANES]
            partner = jnp.where(first_half, pltpu.roll(yn, 96, axis=1), pltpu.roll(yn, 32, axis=1))
            ref, t = dests[j]
            ref[0, :, t * LANES:(t + 1) * LANES] = (yn * cos + partner * sin).astype(BF16)

    for c in range(wvat_ref.shape[0] // 256):
        vat_ref[0, c * 256:(c + 1) * 256, :] = _dot_nt(wvat_ref[c * 256:(c + 1) * 256, :], hb).astype(BF16)
    for c in range(wvb_ref.shape[1] // 256):
        vb_ref[0, :, c * 256:(c + 1) * 256] = _dot(hb, wvb_ref[:, c * 256:(c + 1) * 256]).astype(BF16)

    g_dests = [(ga_ref, t) for t in range(4)] + [(gb_ref, t) for t in range(4)]
    for c, (ref, t) in enumerate(g_dests):
        g = _dot(hb, wg_ref[:, c * 256:(c + 1) * 256])
        ref[0, :, t * 256:(t + 1) * 256] = _sigmoid(g).astype(BF16)


def _inproj(x, mod_l, cos_l, sin_l, gain, wqk, wvat, wvb, wg, seg, tm):
    bsz, seq, d = x.shape
    a_v_w = wvat.shape[0]
    tok = lambda w: pl.BlockSpec((1, tm, w), lambda b, i: (b, i, 0))
    tok_t = pl.BlockSpec((1, a_v_w, tm), lambda b, i: (b, 0, i))
    const = lambda shape: pl.BlockSpec(shape, lambda b, i: (0,) * len(shape),
                                       pipeline_mode=pl.Buffered(1))
    row = lambda w: jax.ShapeDtypeStruct((bsz, seq, w), BF16)
    return pl.pallas_call(
        _inproj_kernel,
        grid=(bsz, seq // tm),
        in_specs=[
            tok(d),
            pl.BlockSpec((1, 6, d), lambda b, i: (b, 0, 0)),
            tok(LANES), tok(LANES),
            const(gain.shape), const(wqk.shape), const(wvat.shape), const(wvb.shape), const(wg.shape),
            const(seg.shape),
        ],
        out_specs=[tok(512), tok(512), tok(768), tok(768), tok_t, tok(768), tok(d), tok(d)],
        out_shape=[row(512), row(512), row(768), row(768),
                   jax.ShapeDtypeStruct((bsz, a_v_w, seq), BF16), row(768), row(d), row(d)],
        compiler_params=pltpu.CompilerParams(dimension_semantics=("parallel", "parallel")),
        name="in_proj",
    )(x, mod_l, cos_l, sin_l, gain, wqk, wvat, wvb, wg, seg)


SCORE_BOUND_NO_SHIFT = 64.0
KV_CHUNK = 256


def _attn_a_kernel(bound_ref, lam_ref, sub_ref, q_ref, k_ref, vt_ref, o_ref,
                   m_ref, l_ref, acc_ref, pa_ref, pb_ref, *, tk, lam_init):
    q = q_ref[0]
    tq = q.shape[0]
    seq = k_ref.shape[1]
    lane = lax.broadcasted_iota(jnp.int32, q.shape, 1)
    zero = jnp.zeros_like(q)
    q_maps = (jnp.where(lane < HEAD_DIM, q, zero), jnp.where(lane >= HEAD_DIM, q, zero))
    n_chunk = tk // KV_CHUNK

    acc_ref[...] = jnp.zeros(acc_ref.shape, F32)
    l_ref[...] = jnp.zeros(l_ref.shape, F32)

    def scores_exp(tile, p_ref):
        start = pl.multiple_of(tile * tk, tk)
        k = k_ref[0, pl.ds(start, tk), :]
        for mi in range(2):
            p = jnp.exp2(_dot_nt(k, q_maps[mi]))
            l_ref[mi] += jnp.sum(p.reshape(tk // 8, 8, tq), axis=0)
            p_ref[mi] = p.astype(BF16)

    def weighted_values(tile, p_ref):
        start = pl.multiple_of(tile * tk, tk)
        vt = vt_ref[0, :, pl.ds(start, tk)]
        for mi in range(2):
            acc_ref[mi] += _dot(vt, p_ref[mi])

    n_tiles = seq // tk

    def no_shift_pair(jj, carry):
        t = 2 * jj
        scores_exp(t + 1, pb_ref)
        weighted_values(t, pa_ref)
        scores_exp(t + 2, pa_ref)
        weighted_values(t + 1, pb_ref)
        return carry

    def no_shift_loop():
        scores_exp(0, pa_ref)
        lax.fori_loop(0, n_tiles // 2 - 1, no_shift_pair, 0)
        scores_exp(n_tiles - 1, pb_ref)
        weighted_values(n_tiles - 2, pa_ref)
        weighted_values(n_tiles - 1, pb_ref)

    def online_max_body(j, carry):
        for c in range(n_chunk):
            start = pl.multiple_of(j * tk + c * KV_CHUNK, KV_CHUNK)
            k = k_ref[0, pl.ds(start, KV_CHUNK), :]
            vt = vt_ref[0, :, pl.ds(start, KV_CHUNK)]
            for mi in range(2):
                s = _dot_nt(k, q_maps[mi])
                m_old = m_ref[mi]
                m_new = jnp.maximum(m_old, jnp.max(s, axis=0, keepdims=True))
                alpha = jnp.exp2(m_old - m_new)
                p = jnp.exp2(s - m_new[0:1])
                l_ref[mi] = alpha * l_ref[mi] + jnp.sum(p.reshape(KV_CHUNK // 8, 8, tq), axis=0)
                acc_ref[mi] = alpha[0:1] * acc_ref[mi] + _dot(vt, p.astype(BF16))
                m_ref[mi] = m_new
        return carry

    no_shift = bound_ref[0] <= SCORE_BOUND_NO_SHIFT

    @pl.when(no_shift)
    def _():
        no_shift_loop()

    @pl.when(jnp.logical_not(no_shift))
    def _():
        m_ref[...] = jnp.full(m_ref.shape, -jnp.inf, F32)
        lax.fori_loop(0, seq // tk, online_max_body, 0)

    lam_p = lam_ref[...]
    s1 = jnp.sum(lam_p[0:1] * lam_p[1:2], axis=-1, keepdims=True)
    s2 = jnp.sum(lam_p[2:3] * lam_p[3:4], axis=-1, keepdims=True)
    lam = jnp.exp(s1) - jnp.exp(s2) + lam_init
    l0 = jnp.sum(l_ref[0], axis=0, keepdims=True)
    l1 = jnp.sum(l_ref[1], axis=0, keepdims=True)
    ot = acc_ref[0] / l0 - lam * (acc_ref[1] / l1)
    o = ot.T
    msq = jnp.mean(o * o, axis=-1, keepdims=True)
    o = o * lax.rsqrt(msq + EPS) * sub_ref[...] * (1.0 - lam_init)
    o_ref[0] = o.astype(BF16)


def _attn_a(score_bound, lam_p, subln, qa, ka, vat, lam_init, tq, tk):
    bsz, seq, _ = qa.shape
    return pl.pallas_call(
        functools.partial(_attn_a_kernel, tk=tk, lam_init=lam_init),
        grid=(bsz, A_HEADS, seq // tq),
        in_specs=[
            pl.BlockSpec(memory_space=pltpu.SMEM),
            pl.BlockSpec(lam_p.shape, lambda b, h, i: (0, 0)),
            pl.BlockSpec(subln.shape, lambda b, h, i: (0, 0)),
            pl.BlockSpec((1, tq, LANES), lambda b, h, i: (b, i, h)),
            pl.BlockSpec((1, seq, LANES), lambda b, h, i: (b, 0, h)),
            pl.BlockSpec((1, LANES, seq), lambda b, h, i: (b, h, 0)),
        ],
        out_specs=pl.BlockSpec((1, tq, LANES), lambda b, h, i: (b, i, h)),
        out_shape=jax.ShapeDtypeStruct((bsz, seq, A_HEADS * LANES), BF16),
        scratch_shapes=[
            pltpu.VMEM((2, 8, tq), F32),
            pltpu.VMEM((2, 8, tq), F32),
            pltpu.VMEM((2, LANES, tq), F32),
            pltpu.VMEM((2, tk, tq), BF16),
            pltpu.VMEM((2, tk, tq), BF16),
        ],
        compiler_params=pltpu.CompilerParams(dimension_semantics=("parallel", "parallel", "parallel")),
        name="diff_attn",
    )(score_bound, lam_p, subln, qa, ka, vat)


def _attn_b_kernel(q_ref, k_ref, v_ref, o_ref, lse_ref, *, radius):
    tq = q_ref.shape[1]
    length = k_ref.shape[1]
    win = tq + 2 * radius
    t0 = pl.program_id(2) * tq
    start = pl.multiple_of(jnp.clip(t0 - radius, 0, length - win), radius)
    kw = k_ref[0, pl.ds(start, win), :]
    vw = v_ref[0, pl.ds(start, win), :]
    q = q_ref[0]

    qpos = t0 + lax.broadcasted_iota(jnp.int32, (tq, win), 0)
    kpos = start + lax.broadcasted_iota(jnp.int32, (tq, win), 1)
    valid = jnp.abs(kpos - qpos) <= radius
    lane = lax.broadcasted_iota(jnp.int32, (tq, LANES), 1)
    low = lane < HEAD_DIM

    for c in range(B_GROUP_W // LANES):
        qc = q[:, c * LANES:(c + 1) * LANES]
        kc = kw[:, c * LANES:(c + 1) * LANES]
        vc = vw[:, c * LANES:(c + 1) * LANES]
        zero = jnp.zeros_like(qc)
        outs, lses = [], []
        for half in range(2):
            qm = jnp.where(low if half == 0 else jnp.logical_not(low), qc, zero)
            s = jnp.where(valid, _dot_nt(qm, kc), -jnp.inf)
            m = jnp.max(s, axis=-1, keepdims=True)
            p = jnp.exp2(s - m)
            l = jnp.sum(p, axis=-1, keepdims=True)
            outs.append(_dot(p.astype(BF16), vc) / l)
            lses.append((m + jnp.log2(l)) * LN2)
        o_ref[0, :, c * LANES:(c + 1) * LANES] = jnp.where(low, outs[0], outs[1]).astype(BF16)
        lse_ref[0, :, c * LANES:(c + 1) * LANES] = jnp.where(low, lses[0], lses[1])


def _attn_b(qb, kb, vb, group, tq):
    window, dilation = B_PAIRS[group]
    radius = window // (2 * dilation)
    bsz, seq, width = qb.shape
    length = seq // dilation
    n_col = width // B_GROUP_W
    view = lambda a: a.reshape(bsz, length, dilation * width)
    col = lambda r: r * n_col + group
    o, lse = pl.pallas_call(
        functools.partial(_attn_b_kernel, radius=radius),
        grid=(bsz, dilation, length // tq),
        in_specs=[
            pl.BlockSpec((1, tq, B_GROUP_W), lambda b, r, i: (b, i, col(r))),
            pl.BlockSpec((1, length, B_GROUP_W), lambda b, r, i: (b, 0, col(r))),
            pl.BlockSpec((1, length, B_GROUP_W), lambda b, r, i: (b, 0, col(r))),
        ],
        out_specs=[pl.BlockSpec((1, tq, B_GROUP_W), lambda b, r, i: (b, i, r))] * 2,
        out_shape=[jax.ShapeDtypeStruct((bsz, length, dilation * B_GROUP_W), BF16),
                   jax.ShapeDtypeStruct((bsz, length, dilation * B_GROUP_W), F32)],
        compiler_params=pltpu.CompilerParams(dimension_semantics=("parallel", "parallel", "parallel")),
        name=f"band_attn_g{group}",
    )(view(qb), view(kb), view(vb))
    return o.reshape(bsz, seq, B_GROUP_W), lse.reshape(bsz, seq, B_GROUP_W)


def _merge_kernel(x_ref, mod_ref, oa_ref, ob0_ref, ls0_ref, ob1_ref, ls1_ref, ob2_ref, ls2_ref,
                  ga_ref, gb_ref, wpa_ref, wpb_ref, wo_ref, wr_ref, br_ref,
                  x1_ref, h2_ref, route_ref):
    ls0, ls1, ls2 = ls0_ref[0], ls1_ref[0], ls2_ref[0]
    mx = jnp.maximum(jnp.maximum(ls0, ls1), ls2)
    e0, e1, e2 = jnp.exp(ls0 - mx), jnp.exp(ls1 - mx), jnp.exp(ls2 - mx)
    ob = (e0 * ob0_ref[0].astype(F32) + e1 * ob1_ref[0].astype(F32) + e2 * ob2_ref[0].astype(F32)) / (e0 + e1 + e2)

    pa = _dot(oa_ref[0], wpa_ref[...])
    pb = _dot(ob.astype(BF16), wpb_ref[...])
    merged = ga_ref[0].astype(F32) * pa + gb_ref[0].astype(F32) * pb
    y = _dot(merged.astype(BF16), wo_ref[...])
    x1 = x_ref[0] + mod_ref[0, 2:3, :] * y
    x1_ref[0] = x1

    ms = jnp.mean(x1 * x1, axis=-1, keepdims=True)
    h2 = x1 * lax.rsqrt(ms + EPS) * (1.0 + mod_ref[0, 4:5, :]) + mod_ref[0, 3:4, :]
    h2_ref[0] = h2.astype(BF16)

    logits = jnp.dot(h2, wr_ref[...], precision=HIGHEST, preferred_element_type=F32) + br_ref[...]
    lane = lax.broadcasted_iota(jnp.int32, logits.shape, 1)
    neg = -jnp.inf
    big = ROUTER_W
    is_grp = (lane >= N_EXPERTS) & (lane < N_EXPERTS + N_GROUPS)
    lg = jnp.where(is_grp, logits, neg)
    mg = jnp.max(lg, axis=-1, keepdims=True)
    g_lane = jnp.min(jnp.where(lg == mg, lane, big), axis=-1, keepdims=True)
    g_val = 1.0 / jnp.sum(jnp.exp(lg - mg), axis=-1, keepdims=True)
    lo = (g_lane - N_EXPERTS) * EXPERTS_PER_GROUP
    in_grp = (lane >= lo) & (lane < lo + EXPERTS_PER_GROUP)
    le = jnp.where(in_grp, logits, neg)
    m1 = jnp.max(le, axis=-1, keepdims=True)
    i1 = jnp.min(jnp.where(le == m1, lane, big), axis=-1, keepdims=True)
    le2 = jnp.where(lane == i1, neg, le)
    m2 = jnp.max(le2, axis=-1, keepdims=True)
    i2 = jnp.min(jnp.where(le2 == m2, lane, big), axis=-1, keepdims=True)
    e = jnp.exp(m2 - m1)
    w1 = g_val / (1.0 + e)
    w2 = g_val * e / (1.0 + e)
    route_ref[0] = jnp.where(lane == 0, i1.astype(F32),
                             jnp.where(lane == 1, i2.astype(F32),
                                       jnp.where(lane == 2, w1, jnp.where(lane == 3, w2, 0.0))))


def _merge(x, mod_l, oa, obs, ga, gb, wpa, wpb, wo, wr, br, tm):
    bsz, seq, d = x.shape
    tok = lambda w: pl.BlockSpec((1, tm, w), lambda b, i: (b, i, 0))
    const = lambda shape: pl.BlockSpec(shape, lambda b, i: (0,) * len(shape),
                                       pipeline_mode=pl.Buffered(1))
    ob_args, ob_specs = [], []
    for o, lse in obs:
        ob_args += [o, lse]
        ob_specs += [tok(B_GROUP_W), tok(B_GROUP_W)]
    return pl.pallas_call(
        _merge_kernel,
        grid=(bsz, seq // tm),
        in_specs=[tok(d), pl.BlockSpec((1, 6, d), lambda b, i: (b, 0, 0)), tok(oa.shape[-1])]
                 + ob_specs + [tok(d), tok(d)]
                 + [const(w.shape) for w in (wpa, wpb, wo, wr, br)],
        out_specs=[tok(d), tok(d), tok(ROUTER_W)],
        out_shape=[jax.ShapeDtypeStruct((bsz, seq, d), F32),
                   jax.ShapeDtypeStruct((bsz, seq, d), BF16),
                   jax.ShapeDtypeStruct((bsz, seq, ROUTER_W), F32)],
        compiler_params=pltpu.CompilerParams(dimension_semantics=("parallel", "parallel")),
        name="merge_proj",
    )(x, mod_l, oa, *ob_args, ga, gb, wpa, wpb, wo, wr, br)


MOE_BLOCK = 1024
SEG_ALIGN = 16
ROW_CHUNK = 128
PERM_CHUNK = 256
SORTED_ROWS = -(-(2 * MOE_BLOCK + N_EXPERTS * (SEG_ALIGN - 1)) // PERM_CHUNK) * PERM_CHUNK
SORTED_ROWS_ALLOC = SORTED_ROWS + ROW_CHUNK


def _plan_kernel(route_ref, posc_ref, posr_ref, base_ref, npad_ref):
    r = route_ref[0]
    tb = r.shape[0]
    lane = lax.broadcasted_iota(jnp.int32, r.shape, 1).astype(F32)
    oh1 = jnp.where(lane == r[:, 0:1], 1.0, 0.0)
    oh2 = jnp.where(lane == r[:, 1:2], 1.0, 0.0)
    cnt1 = jnp.sum(oh1, axis=0, keepdims=True)
    cnt2 = jnp.sum(oh2, axis=0, keepdims=True)
    npad = jnp.floor((cnt1 + cnt2 + (SEG_ALIGN - 1)) * (1.0 / SEG_ALIGN)) * SEG_ALIGN
    ri = lax.broadcasted_iota(jnp.int32, (LANES, LANES), 0)
    ci = lax.broadcasted_iota(jnp.int32, (LANES, LANES), 1)
    upper = jnp.where(ri < ci, 1.0, 0.0)
    npad8 = jnp.broadcast_to(npad, (8, LANES))
    base = jnp.dot(npad8, upper, precision=HIGHEST, preferred_element_type=F32)[0:1]

    ti = lax.broadcasted_iota(jnp.int32, (tb, tb), 0)
    tj = lax.broadcasted_iota(jnp.int32, (tb, tb), 1)
    before = jnp.where(tj < ti, 1.0, 0.0).astype(BF16)
    pre1 = _dot(before, oh1.astype(BF16))
    pre2 = _dot(before, oh2.astype(BF16))
    pos1 = jnp.sum(oh1 * (base + pre1), axis=-1, keepdims=True)
    pos2 = jnp.sum(oh2 * (base + cnt1 + pre2), axis=-1, keepdims=True)
    packed = jnp.where(lane == 0.0, pos1, jnp.where(lane == 1.0, pos2, jnp.where(lane >= 2.0, r, 0.0)))
    posc_ref[0] = packed
    posr_ref[0] = packed.T[0:8, :]
    base_ref[0] = base
    npad_ref[0] = npad


def _moe_plan(route):
    n_blk, tb, _ = route.shape
    vec = pl.BlockSpec((1, 1, LANES), lambda i: (i, 0, 0))
    return pl.pallas_call(
        _plan_kernel,
        grid=(n_blk,),
        in_specs=[pl.BlockSpec((1, tb, ROUTER_W), lambda i: (i, 0, 0))],
        out_specs=[pl.BlockSpec((1, tb, LANES), lambda i: (i, 0, 0)),
                   pl.BlockSpec((1, 8, tb), lambda i: (i, 0, 0)), vec, vec],
        out_shape=[jax.ShapeDtypeStruct((n_blk, tb, LANES), F32),
                   jax.ShapeDtypeStruct((n_blk, 8, tb), F32),
                   jax.ShapeDtypeStruct((n_blk, 1, LANES), F32),
                   jax.ShapeDtypeStruct((n_blk, 1, LANES), F32)],
        compiler_params=pltpu.CompilerParams(dimension_semantics=("parallel",)),
        name="moe_plan",
    )(route)


def _moe_kernel(base_ref, npad_ref, x1_ref, mod_ref, h_ref, posc_ref, posr_ref, wg_ref, wu_ref, wd_ref,
                o_ref, xs_ref, ys_ref):
    blk = pl.program_id(0)
    e = pl.program_id(1)
    tb = h_ref.shape[1]

    @pl.when(e == 0)
    def _():
        pr = posr_ref[0]
        h = h_ref[0]
        for c in range(SORTED_ROWS // PERM_CHUNK):
            rid = (c * PERM_CHUNK + lax.broadcasted_iota(jnp.int32, (PERM_CHUNK, tb), 0)).astype(F32)
            sel = jnp.where(rid == pr[0:1], 1.0, jnp.where(rid == pr[1:2], 1.0, 0.0)).astype(BF16)
            xs_ref[c * PERM_CHUNK:(c + 1) * PERM_CHUNK, :] = _dot(sel, h).astype(BF16)
        xs_ref[SORTED_ROWS:, :] = jnp.zeros((ROW_CHUNK, xs_ref.shape[1]), BF16)
        ys_ref[...] = jnp.zeros(ys_ref.shape, BF16)

    seg_start = base_ref[blk, e]
    seg_rows = npad_ref[blk, e]

    def expert_chunk(c, carry):
        r0 = pl.multiple_of(seg_start + c * ROW_CHUNK, SEG_ALIGN)
        xc = xs_ref[pl.ds(r0, ROW_CHUNK), :]
        a = _dot(xc, wg_ref[0])
        u = _dot(xc, wu_ref[0])
        hid = (a * _sigmoid(a)) * u
        ys_ref[pl.ds(r0, ROW_CHUNK), :] = _dot(hid.astype(BF16), wd_ref[0]).astype(BF16)
        return carry

    lax.fori_loop(0, (seg_rows + ROW_CHUNK - 1) // ROW_CHUNK, expert_chunk, 0)

    @pl.when(e == pl.num_programs(1) - 1)
    def _():
        ys = ys_ref[...]
        for c in range(tb // PERM_CHUNK):
            pc = posc_ref[0, c * PERM_CHUNK:(c + 1) * PERM_CHUNK, :]
            rid = lax.broadcasted_iota(jnp.int32, (PERM_CHUNK, SORTED_ROWS_ALLOC), 1).astype(F32)
            wsel = (jnp.where(rid == pc[:, 0:1], pc[:, 2:3], 0.0)
                    + jnp.where(rid == pc[:, 1:2], pc[:, 3:4], 0.0)).astype(BF16)
            y = _dot(wsel, ys)
            rows = slice(c * PERM_CHUNK, (c + 1) * PERM_CHUNK)
            o_ref[0, rows, :] = x1_ref[0, rows, :] + mod_ref[0, 5:6, :] * y


def _moe(x1, mod_l, h2, route, weg, weu, wed):
    bsz, seq, d = x1.shape
    n_e, _, d_e = weg.shape
    tb = min(MOE_BLOCK, seq)
    assert tb == MOE_BLOCK, "sorted-row capacity is sized for MOE_BLOCK tokens"
    per_batch = seq // tb
    n_blk = bsz * per_batch
    blocked = lambda a: a.reshape(n_blk, tb, a.shape[-1])
    posc, posr, base, npad = _moe_plan(blocked(route))
    base_i = base.reshape(n_blk, LANES).astype(jnp.int32)
    npad_i = npad.reshape(n_blk, LANES).astype(jnp.int32)

    tok = lambda w: pl.BlockSpec((1, tb, w), lambda i, e, *_: (i, 0, 0))
    grid_spec = pltpu.PrefetchScalarGridSpec(
        num_scalar_prefetch=2,
        grid=(n_blk, n_e),
        in_specs=[tok(d),
                  pl.BlockSpec((1, 6, d), lambda i, e, *_: (i // per_batch, 0, 0)),
                  tok(d), tok(LANES),
                  pl.BlockSpec((1, 8, tb), lambda i, e, *_: (i, 0, 0)),
                  pl.BlockSpec((1, d, d_e), lambda i, e, *_: (e, 0, 0)),
                  pl.BlockSpec((1, d, d_e), lambda i, e, *_: (e, 0, 0)),
                  pl.BlockSpec((1, d_e, d), lambda i, e, *_: (e, 0, 0))],
        out_specs=tok(d),
        scratch_shapes=[pltpu.VMEM((SORTED_ROWS_ALLOC, d), BF16),
                        pltpu.VMEM((SORTED_ROWS_ALLOC, d), BF16)],
    )
    out = pl.pallas_call(
        _moe_kernel,
        grid_spec=grid_spec,
        out_shape=jax.ShapeDtypeStruct((n_blk, tb, d), F32),
        compiler_params=pltpu.CompilerParams(dimension_semantics=("parallel", "arbitrary")),
        name="moe_experts",
    )(base_i, npad_i, blocked(x1), mod_l, blocked(h2), posc, posr, weg, weu, wed)
    return out.reshape(bsz, seq, d)


def _tiles(seq):
    return dict(
        tm_proj=min(512, seq),
        tq_a=min(256, seq),
        tk_a=min(512, seq),
        tq_b=128,
        tm_merge=min(512, seq),
    )


def kernel(x, c, positions, w_ada, b_ada, w_in, qn_a, kn_a, lam_q1, lam_k1, lam_q2, lam_k2,
           subln_a, qn_b, kn_b, w_pa, w_pb, w_o, w_r1, b_r1, w_r2, b_r2,
           w_e_gate, w_e_up, w_e_down):
    depth = w_ada.shape[0]
    bsz, seq, d = x.shape
    t = _tiles(seq)

    mod = _ada(c, w_ada, b_ada).reshape(depth, bsz, 6, d)
    cos_l, sin_l = _rope_tables(positions)
    seg = jnp.kron(jnp.eye(LANES // HEAD_DIM, dtype=F32),
                   jnp.full((HEAD_DIM, HEAD_DIM), 1.0 / HEAD_DIM, F32)).astype(BF16)
    q_scale = HEAD_DIM ** -0.5 * LOG2E

    for layer in range(depth):
        lam_init = 0.8 - 0.6 * math.exp(-0.3 * layer)
        w = w_in[layer]
        wqk = jnp.concatenate([w[:, 0:1024], w[:, 1536:3072]], axis=1).astype(BF16)
        wvat = w[:, 1024:1536].T.astype(BF16)
        wvb = w[:, 3072:3840].astype(BF16)
        wg = w[:, 3840:].astype(BF16)
        gain = jnp.concatenate([
            jnp.tile(qn_a[layer] * q_scale, 8), jnp.tile(kn_a[layer], 8),
            jnp.tile(qn_b[layer] * q_scale, 12), jnp.tile(kn_b[layer], 12)]).reshape(1, -1)

        qa, ka, qb, kb, vat, vb, ga, gb = _inproj(x, mod[layer], cos_l, sin_l, gain, wqk, wvat, wvb, wg, seg,
                                                  t["tm_proj"])

        score_bound = (1.01 * HEAD_DIM * q_scale * jnp.max(jnp.abs(qn_a[layer]))
                       * jnp.max(jnp.abs(kn_a[layer]))).reshape(1)
        lam_p = jnp.stack([lam_q1[layer], lam_k1[layer], lam_q2[layer], lam_k2[layer]])
        oa = _attn_a(score_bound, lam_p, subln_a[layer].reshape(1, -1), qa, ka, vat, lam_init,
                     t["tq_a"], t["tk_a"])
        obs = [_attn_b(qb, kb, vb, g, t["tq_b"]) for g in range(B_GROUPS)]

        wr = jnp.zeros((d, ROUTER_W), F32)
        wr = wr.at[:, :N_EXPERTS].set(w_r2[layer]).at[:, N_EXPERTS:N_EXPERTS + N_GROUPS].set(w_r1[layer])
        br = jnp.zeros((1, ROUTER_W), F32)
        br = br.at[0, :N_EXPERTS].set(b_r2[layer]).at[0, N_EXPERTS:N_EXPERTS + N_GROUPS].set(b_r1[layer])
        x1, h2, route = _merge(x, mod[layer], oa, obs, ga, gb,
                               w_pa[layer].astype(BF16), w_pb[layer].astype(BF16), w_o[layer].astype(BF16),
                               wr, br, t["tm_merge"])

        x = _moe(x1, mod[layer], h2, route,
                 w_e_gate[layer].astype(BF16), w_e_up[layer].astype(BF16), w_e_down[layer].astype(BF16))
    return x
```

```python
import functools
import math

import jax
import jax.numpy as jnp
from jax import lax
from jax.experimental import pallas as pl
from jax.experimental.pallas import tpu as pltpu

EPS = 1e-6
ROPE_THETA = 10000.0
LOG2E = math.log2(math.e)
LN2 = math.log(2.0)

A_HEADS = 4
HEAD_DIM = 64
LANES = 128
B_PAIRS = ((128, 1), (512, 4), (2048, 16))
B_GROUPS = len(B_PAIRS)
B_GROUP_W = 256
N_GROUPS = 4
EXPERTS_PER_GROUP = 8
N_EXPERTS = N_GROUPS * EXPERTS_PER_GROUP
ROUTER_W = 128

F32 = jnp.float32
BF16 = jnp.bfloat16
HIGHEST = lax.Precision.HIGHEST


def _dot(a, b):
    return jnp.dot(a, b, preferred_element_type=F32)


def _dot_nt(a, b):
    return lax.dot_general(a, b, (((1,), (1,)), ((), ())), preferred_element_type=F32)


def _sigmoid(x):
    return 1.0 / (1.0 + jnp.exp(-x))


def _ada_kernel(c_ref, w_ref, b_ref, o_ref):
    c = c_ref[...]
    c_act = c * _sigmoid(c)
    o_ref[0] = jnp.dot(c_act, w_ref[0], precision=HIGHEST, preferred_element_type=F32) + b_ref[0]


def _ada(c, w_ada, b_ada):
    depth, d, six_d = w_ada.shape
    bsz = c.shape[0]
    n_col = six_d // d
    return pl.pallas_call(
        _ada_kernel,
        grid=(depth, n_col),
        in_specs=[
            pl.BlockSpec((bsz, d), lambda l, j: (0, 0)),
            pl.BlockSpec((1, d, d), lambda l, j: (l, 0, j)),
            pl.BlockSpec((1, 1, d), lambda l, j: (l, 0, j)),
        ],
        out_specs=pl.BlockSpec((1, bsz, d), lambda l, j: (l, 0, j)),
        out_shape=jax.ShapeDtypeStruct((depth, bsz, six_d), F32),
        name="ada_mod",
    )(c, w_ada, b_ada.reshape(depth, 1, six_d))


def _rope_kernel(pos_ref, f_ref, cos_ref, sin_ref):
    ang = pos_ref[0].astype(F32) * f_ref[...]
    cos_ref[0] = jnp.cos(ang)
    sin_ref[0] = jnp.sin(ang)


def _rope_tables(positions):
    bsz, seq = positions.shape
    half = HEAD_DIM // 2
    inv_freq = ROPE_THETA ** (-jnp.arange(0, HEAD_DIM, 2, dtype=F32) / HEAD_DIM)
    cos_t, sin_t = pl.pallas_call(
        _rope_kernel,
        grid=(bsz,),
        in_specs=[
            pl.BlockSpec((1, 1, seq), lambda b: (b, 0, 0)),
            pl.BlockSpec((half, 1), lambda b: (0, 0)),
        ],
        out_specs=[pl.BlockSpec((1, half, seq), lambda b: (b, 0, 0))] * 2,
        out_shape=[jax.ShapeDtypeStruct((bsz, half, seq), F32)] * 2,
        name="rope_tables",
    )(positions.reshape(bsz, 1, seq), inv_freq.reshape(half, 1))
    cos = cos_t.transpose(0, 2, 1)
    sin = sin_t.transpose(0, 2, 1)
    cos_l = jnp.concatenate([cos, cos, cos, cos], axis=-1)
    sin_l = jnp.concatenate([-sin, sin, -sin, sin], axis=-1)
    return cos_l, sin_l


def _inproj_kernel(x_ref, mod_ref, cos_ref, sin_ref, gain_ref, wqk_ref, wvat_ref, wvb_ref, wg_ref, seg_ref,
                   qa_ref, ka_ref, qb0_ref, qb1_ref, qb2_ref, kb0_ref, kb1_ref, kb2_ref,
                   vat_ref, vb0_ref, vb1_ref, vb2_ref, ga_ref, gb_ref, stage_ref):
    x = x_ref[0]
    tm = x.shape[0]
    ms = jnp.mean(x * x, axis=-1, keepdims=True)
    h = x * lax.rsqrt(ms + EPS) * (1.0 + mod_ref[0, 1:2, :]) + mod_ref[0, 0:1, :]
    hb = h.astype(BF16)

    cos = cos_ref[0]
    sin = sin_ref[0]
    seg = seg_ref[...]
    lane = lax.broadcasted_iota(jnp.int32, cos.shape, 1)
    first_half = (lane % HEAD_DIM) < (HEAD_DIM // 2)

    def norm_rope(y, j):
        sq = y * y
        hi = sq.astype(BF16)
        lo = (sq - hi.astype(F32)).astype(BF16)
        msq = _dot(hi, seg) + _dot(lo, seg)
        yn = y * lax.rsqrt(msq + EPS) * gain_ref[:, j * LANES:(j + 1) * LANES]
        partner = jnp.where(first_half, pltpu.roll(yn, 96, axis=1), pltpu.roll(yn, 32, axis=1))
        return yn * cos + partner * sin

    def store_dilated(ref, group, val):
        dil = B_PAIRS[group][1]
        if dil == 1:
            ref[0] = val.astype(BF16)
            return
        for hf in range(B_GROUP_W // LANES):
            stage_ref[hf] = val[:, hf * LANES:(hf + 1) * LANES]
        for r in range(dil):
            for hf in range(B_GROUP_W // LANES):
                rows = stage_ref[hf, pl.ds(r, tm // dil, stride=dil), :]
                col = r * B_GROUP_W + hf * LANES
                ref[0, :, col:col + LANES] = rows.astype(BF16)

    flat = [(qa_ref, 0), (qa_ref, 1), (ka_ref, 0), (ka_ref, 1)]
    grouped = [(qb0_ref, 0), (qb1_ref, 1), (qb2_ref, 2), (kb0_ref, 0), (kb1_ref, 1), (kb2_ref, 2)]
    for c in range(len(flat) + len(grouped)):
        y2 = _dot(hb, wqk_ref[:, c * 256:(c + 1) * 256])
        halves = [norm_rope(y2[:, hf * LANES:(hf + 1) * LANES], 2 * c + hf) for hf in range(2)]
        if c < len(flat):
            ref, t = flat[c]
            for hf in range(2):
                ref[0, :, t * 256 + hf * LANES:t * 256 + (hf + 1) * LANES] = halves[hf].astype(BF16)
        else:
            ref, group = grouped[c - len(flat)]
            store_dilated(ref, group, jnp.concatenate(halves, axis=1))

    for c in range(wvat_ref.shape[0] // 256):
        vat_ref[0, c * 256:(c + 1) * 256, :] = _dot_nt(wvat_ref[c * 256:(c + 1) * 256, :], hb).astype(BF16)
    for group, ref in enumerate((vb0_ref, vb1_ref, vb2_ref)):
        store_dilated(ref, group, _dot(hb, wvb_ref[:, group * 256:(group + 1) * 256]))

    g_dests = [(ga_ref, t) for t in range(4)] + [(gb_ref, t) for t in range(4)]
    for c, (ref, t) in enumerate(g_dests):
        g = _dot(hb, wg_ref[:, c * 256:(c + 1) * 256])
        ref[0, :, t * 256:(t + 1) * 256] = _sigmoid(g).astype(BF16)


def _inproj(x, mod_l, cos_l, sin_l, gain, wqk, wvat, wvb, wg, seg, tm):
    bsz, seq, d = x.shape
    a_v_w = wvat.shape[0]
    tok = lambda w: pl.BlockSpec((1, tm, w), lambda b, i: (b, i, 0))
    tok_t = pl.BlockSpec((1, a_v_w, tm), lambda b, i: (b, 0, i))
    const = lambda shape: pl.BlockSpec(shape, lambda b, i: (0,) * len(shape),
                                       pipeline_mode=pl.Buffered(1))
    row = lambda w: jax.ShapeDtypeStruct((bsz, seq, w), BF16)
    dil_specs = [pl.BlockSpec((1, tm // dl, dl * B_GROUP_W), lambda b, i: (b, i, 0)) for _, dl in B_PAIRS]
    dil_shapes = [jax.ShapeDtypeStruct((bsz, seq // dl, dl * B_GROUP_W), BF16) for _, dl in B_PAIRS]
    outs = pl.pallas_call(
        _inproj_kernel,
        grid=(bsz, seq // tm),
        in_specs=[
            tok(d),
            pl.BlockSpec((1, 6, d), lambda b, i: (b, 0, 0)),
            tok(LANES), tok(LANES),
            const(gain.shape), const(wqk.shape), const(wvat.shape), const(wvb.shape), const(wg.shape),
            const(seg.shape),
        ],
        out_specs=[tok(512), tok(512)] + dil_specs + dil_specs + [tok_t] + dil_specs + [tok(d), tok(d)],
        out_shape=[row(512), row(512)] + dil_shapes + dil_shapes
                  + [jax.ShapeDtypeStruct((bsz, a_v_w, seq), BF16)] + dil_shapes + [row(d), row(d)],
        scratch_shapes=[pltpu.VMEM((B_GROUP_W // LANES, tm, LANES), F32)],
        compiler_params=pltpu.CompilerParams(dimension_semantics=("parallel", "parallel")),
        name="in_proj",
    )(x, mod_l, cos_l, sin_l, gain, wqk, wvat, wvb, wg, seg)
    qa, ka = outs[0], outs[1]
    qb, kb, vat, vb, ga, gb = outs[2:5], outs[5:8], outs[8], outs[9:12], outs[12], outs[13]
    return qa, ka, qb, kb, vat, vb, ga, gb


SCORE_BOUND_NO_SHIFT = 64.0
KV_CHUNK = 256


def _attn_a_kernel(bound_ref, lam_ref, sub_ref, q_ref, k_ref, vt_ref, o_ref,
                   m_ref, l_ref, acc_ref, pa_ref, pb_ref, *, tk, lam_init):
    q = q_ref[0]
    tq = q.shape[0]
    seq = k_ref.shape[1]
    lane = lax.broadcasted_iota(jnp.int32, q.shape, 1)
    zero = jnp.zeros_like(q)
    q_maps = (jnp.where(lane < HEAD_DIM, q, zero), jnp.where(lane >= HEAD_DIM, q, zero))
    n_chunk = tk // KV_CHUNK

    acc_ref[...] = jnp.zeros(acc_ref.shape, F32)
    l_ref[...] = jnp.zeros(l_ref.shape, F32)

    def scores_exp(tile, p_ref):
        start = pl.multiple_of(tile * tk, tk)
        k = k_ref[0, pl.ds(start, tk), :]
        for mi in range(2):
            p = jnp.exp2(_dot_nt(k, q_maps[mi]))
            l_ref[mi] += jnp.sum(p.reshape(tk // 8, 8, tq), axis=0)
            p_ref[mi] = p.astype(BF16)

    def weighted_values(tile, p_ref):
        start = pl.multiple_of(tile * tk, tk)
        vt = vt_ref[0, :, pl.ds(start, tk)]
        for mi in range(2):
            acc_ref[mi] += _dot(vt, p_ref[mi])

    n_tiles = seq // tk

    def no_shift_pair(jj, carry):
        t = 2 * jj
        scores_exp(t + 1, pb_ref)
        weighted_values(t, pa_ref)
        scores_exp(t + 2, pa_ref)
        weighted_values(t + 1, pb_ref)
        return carry

    def no_shift_loop():
        scores_exp(0, pa_ref)
        lax.fori_loop(0, n_tiles // 2 - 1, no_shift_pair, 0)
        scores_exp(n_tiles - 1, pb_ref)
        weighted_values(n_tiles - 2, pa_ref)
        weighted_values(n_tiles - 1, pb_ref)

    def online_max_body(j, carry):
        for c in range(n_chunk):
            start = pl.multiple_of(j * tk + c * KV_CHUNK, KV_CHUNK)
            k = k_ref[0, pl.ds(start, KV_CHUNK), :]
            vt = vt_ref[0, :, pl.ds(start, KV_CHUNK)]
            for mi in range(2):
                s = _dot_nt(k, q_maps[mi])
                m_old = m_ref[mi]
                m_new = jnp.maximum(m_old, jnp.max(s, axis=0, keepdims=True))
                alpha = jnp.exp2(m_old - m_new)
                p = jnp.exp2(s - m_new[0:1])
                l_ref[mi] = alpha * l_ref[mi] + jnp.sum(p.reshape(KV_CHUNK // 8, 8, tq), axis=0)
                acc_ref[mi] = alpha[0:1] * acc_ref[mi] + _dot(vt, p.astype(BF16))
                m_ref[mi] = m_new
        return carry

    no_shift = bound_ref[0] <= SCORE_BOUND_NO_SHIFT

    @pl.when(no_shift)
    def _():
        no_shift_loop()

    @pl.when(jnp.logical_not(no_shift))
    def _():
        m_ref[...] = jnp.full(m_ref.shape, -jnp.inf, F32)
        lax.fori_loop(0, seq // tk, online_max_body, 0)

    lam_p = lam_ref[...]
    s1 = jnp.sum(lam_p[0:1] * lam_p[1:2], axis=-1, keepdims=True)
    s2 = jnp.sum(lam_p[2:3] * lam_p[3:4], axis=-1, keepdims=True)
    lam = jnp.exp(s1) - jnp.exp(s2) + lam_init
    l0 = jnp.sum(l_ref[0], axis=0, keepdims=True)
    l1 = jnp.sum(l_ref[1], axis=0, keepdims=True)
    ot = acc_ref[0] / l0 - lam * (acc_ref[1] / l1)
    o = ot.T
    msq = jnp.mean(o * o, axis=-1, keepdims=True)
    o = o * lax.rsqrt(msq + EPS) * sub_ref[...] * (1.0 - lam_init)
    o_ref[0] = o.astype(BF16)


def _attn_a(score_bound, lam_p, subln, qa, ka, vat, lam_init, tq, tk):
    bsz, seq, _ = qa.shape
    return pl.pallas_call(
        functools.partial(_attn_a_kernel, tk=tk, lam_init=lam_init),
        grid=(bsz, A_HEADS, seq // tq),
        in_specs=[
            pl.BlockSpec(memory_space=pltpu.SMEM),
            pl.BlockSpec(lam_p.shape, lambda b, h, i: (0, 0)),
            pl.BlockSpec(subln.shape, lambda b, h, i: (0, 0)),
            pl.BlockSpec((1, tq, LANES), lambda b, h, i: (b, i, h)),
            pl.BlockSpec((1, seq, LANES), lambda b, h, i: (b, 0, h)),
            pl.BlockSpec((1, LANES, seq), lambda b, h, i: (b, h, 0)),
        ],
        out_specs=pl.BlockSpec((1, tq, LANES), lambda b, h, i: (b, i, h)),
        out_shape=jax.ShapeDtypeStruct((bsz, seq, A_HEADS * LANES), BF16),
        scratch_shapes=[
            pltpu.VMEM((2, 8, tq), F32),
            pltpu.VMEM((2, 8, tq), F32),
            pltpu.VMEM((2, LANES, tq), F32),
            pltpu.VMEM((2, tk, tq), BF16),
            pltpu.VMEM((2, tk, tq), BF16),
        ],
        compiler_params=pltpu.CompilerParams(dimension_semantics=("parallel", "parallel", "parallel")),
        name="diff_attn",
    )(score_bound, lam_p, subln, qa, ka, vat)


def _attn_b_kernel(q_ref, k_ref, v_ref, o_ref, lse_ref, *, radius):
    tq = q_ref.shape[1]
    length = k_ref.shape[1]
    win = tq + 2 * radius
    t0 = pl.program_id(2) * tq
    start = pl.multiple_of(jnp.clip(t0 - radius, 0, length - win), radius)
    kw = k_ref[0, pl.ds(start, win), :]
    vw = v_ref[0, pl.ds(start, win), :]
    q = q_ref[0]

    qpos = t0 + lax.broadcasted_iota(jnp.int32, (tq, win), 0)
    kpos = start + lax.broadcasted_iota(jnp.int32, (tq, win), 1)
    valid = jnp.abs(kpos - qpos) <= radius
    lane = lax.broadcasted_iota(jnp.int32, (tq, LANES), 1)
    low = lane < HEAD_DIM

    for c in range(B_GROUP_W // LANES):
        qc = q[:, c * LANES:(c + 1) * LANES]
        kc = kw[:, c * LANES:(c + 1) * LANES]
        vc = vw[:, c * LANES:(c + 1) * LANES]
        zero = jnp.zeros_like(qc)
        outs, lses = [], []
        for half in range(2):
            qm = jnp.where(low if half == 0 else jnp.logical_not(low), qc, zero)
            s = jnp.where(valid, _dot_nt(qm, kc), -jnp.inf)
            m = jnp.max(s, axis=-1, keepdims=True)
            p = jnp.exp2(s - m)
            l = jnp.sum(p, axis=-1, keepdims=True)
            outs.append(_dot(p.astype(BF16), vc) / l)
            lses.append((m + jnp.log2(l)) * LN2)
        o_ref[0, :, c * LANES:(c + 1) * LANES] = jnp.where(low, outs[0], outs[1]).astype(BF16)
        lse_ref[0, :, c * LANES:(c + 1) * LANES] = jnp.where(low, lses[0], lses[1])


def _attn_b(qg, kg, vg, group, tq):
    window, dilation = B_PAIRS[group]
    radius = window // (2 * dilation)
    bsz, length, _ = qg.shape
    return pl.pallas_call(
        functools.partial(_attn_b_kernel, radius=radius),
        grid=(bsz, dilation, length // tq),
        in_specs=[
            pl.BlockSpec((1, tq, B_GROUP_W), lambda b, r, i: (b, i, r)),
            pl.BlockSpec((1, length, B_GROUP_W), lambda b, r, i: (b, 0, r)),
            pl.BlockSpec((1, length, B_GROUP_W), lambda b, r, i: (b, 0, r)),
        ],
        out_specs=[pl.BlockSpec((1, tq, B_GROUP_W), lambda b, r, i: (b, i, r))] * 2,
        out_shape=[jax.ShapeDtypeStruct((bsz, length, dilation * B_GROUP_W), BF16),
                   jax.ShapeDtypeStruct((bsz, length, dilation * B_GROUP_W), F32)],
        compiler_params=pltpu.CompilerParams(dimension_semantics=("parallel", "parallel", "parallel")),
        name=f"band_attn_g{group}",
    )(qg, kg, vg)


def _merge_kernel(x_ref, mod_ref, oa_ref, ob0_ref, ls0_ref, ob1_ref, ls1_ref, ob2_ref, ls2_ref,
                  ga_ref, gb_ref, wpa_ref, wpb_ref, wo_ref, wr_ref, br_ref,
                  x1_ref, h2_ref, route_ref, so1_ref, sl1_ref, so2_ref, sl2_ref):
    tm = x_ref.shape[1]

    def token_major(o_ref, l_ref, group, so_ref, sl_ref):
        dil = B_PAIRS[group][1]
        n_hf = B_GROUP_W // LANES
        for r in range(dil):
            for hf in range(n_hf):
                cols = slice(r * B_GROUP_W + hf * LANES, r * B_GROUP_W + (hf + 1) * LANES)
                so_ref[hf, pl.ds(r, tm // dil, stride=dil), :] = o_ref[0, :, cols].astype(F32)
                sl_ref[hf, pl.ds(r, tm // dil, stride=dil), :] = l_ref[0, :, cols]
        return (jnp.concatenate([so_ref[hf] for hf in range(n_hf)], axis=1),
                jnp.concatenate([sl_ref[hf] for hf in range(n_hf)], axis=1))

    o0, ls0 = ob0_ref[0].astype(F32), ls0_ref[0]
    o1, ls1 = token_major(ob1_ref, ls1_ref, 1, so1_ref, sl1_ref)
    o2, ls2 = token_major(ob2_ref, ls2_ref, 2, so2_ref, sl2_ref)
    mx = jnp.maximum(jnp.maximum(ls0, ls1), ls2)
    e0, e1, e2 = jnp.exp(ls0 - mx), jnp.exp(ls1 - mx), jnp.exp(ls2 - mx)
    ob = (e0 * o0 + e1 * o1 + e2 * o2) / (e0 + e1 + e2)

    pa = _dot(oa_ref[0], wpa_ref[...])
    pb = _dot(ob.astype(BF16), wpb_ref[...])
    merged = ga_ref[0].astype(F32) * pa + gb_ref[0].astype(F32) * pb
    y = _dot(merged.astype(BF16), wo_ref[...])
    x1 = x_ref[0] + mod_ref[0, 2:3, :] * y
    x1_ref[0] = x1

    ms = jnp.mean(x1 * x1, axis=-1, keepdims=True)
    h2 = x1 * lax.rsqrt(ms + EPS) * (1.0 + mod_ref[0, 4:5, :]) + mod_ref[0, 3:4, :]
    h2_ref[0] = h2.astype(BF16)

    logits = jnp.dot(h2, wr_ref[...], precision=HIGHEST, preferred_element_type=F32) + br_ref[...]
    lane = lax.broadcasted_iota(jnp.int32, logits.shape, 1)
    neg = -jnp.inf
    big = ROUTER_W
    is_grp = (lane >= N_EXPERTS) & (lane < N_EXPERTS + N_GROUPS)
    lg = jnp.where(is_grp, logits, neg)
    mg = jnp.max(lg, axis=-1, keepdims=True)
    g_lane = jnp.min(jnp.where(lg == mg, lane, big), axis=-1, keepdims=True)
    g_val = 1.0 / jnp.sum(jnp.exp(lg - mg), axis=-1, keepdims=True)
    lo = (g_lane - N_EXPERTS) * EXPERTS_PER_GROUP
    in_grp = (lane >= lo) & (lane < lo + EXPERTS_PER_GROUP)
    le = jnp.where(in_grp, logits, neg)
    m1 = jnp.max(le, axis=-1, keepdims=True)
    i1 = jnp.min(jnp.where(le == m1, lane, big), axis=-1, keepdims=True)
    le2 = jnp.where(lane == i1, neg, le)
    m2 = jnp.max(le2, axis=-1, keepdims=True)
    i2 = jnp.min(jnp.where(le2 == m2, lane, big), axis=-1, keepdims=True)
    e = jnp.exp(m2 - m1)
    w1 = g_val / (1.0 + e)
    w2 = g_val * e / (1.0 + e)
    route_ref[0] = jnp.where(lane == 0, i1.astype(F32),
                             jnp.where(lane == 1, i2.astype(F32),
                                       jnp.where(lane == 2, w1, jnp.where(lane == 3, w2, 0.0))))


def _merge(x, mod_l, oa, obs, ga, gb, wpa, wpb, wo, wr, br, tm):
    bsz, seq, d = x.shape
    tok = lambda w: pl.BlockSpec((1, tm, w), lambda b, i: (b, i, 0))
    const = lambda shape: pl.BlockSpec(shape, lambda b, i: (0,) * len(shape),
                                       pipeline_mode=pl.Buffered(1))
    ob_args, ob_specs = [], []
    for (o, lse), (_, dl) in zip(obs, B_PAIRS):
        ob_args += [o, lse]
        ob_specs += [pl.BlockSpec((1, tm // dl, dl * B_GROUP_W), lambda b, i: (b, i, 0))] * 2
    return pl.pallas_call(
        _merge_kernel,
        grid=(bsz, seq // tm),
        in_specs=[tok(d), pl.BlockSpec((1, 6, d), lambda b, i: (b, 0, 0)), tok(oa.shape[-1])]
                 + ob_specs + [tok(d), tok(d)]
                 + [const(w.shape) for w in (wpa, wpb, wo, wr, br)],
        out_specs=[tok(d), tok(d), tok(ROUTER_W)],
        out_shape=[jax.ShapeDtypeStruct((bsz, seq, d), F32),
                   jax.ShapeDtypeStruct((bsz, seq, d), BF16),
                   jax.ShapeDtypeStruct((bsz, seq, ROUTER_W), F32)],
        scratch_shapes=[pltpu.VMEM((B_GROUP_W // LANES, tm, LANES), F32)] * 4,
        compiler_params=pltpu.CompilerParams(dimension_semantics=("parallel", "parallel")),
        name="merge_proj",
    )(x, mod_l, oa, *ob_args, ga, gb, wpa, wpb, wo, wr, br)


MOE_BLOCK = 1024
SEG_ALIGN = 16
ROW_CHUNK = 128
PERM_CHUNK = 256
SORTED_ROWS = -(-(2 * MOE_BLOCK + N_EXPERTS * (SEG_ALIGN - 1)) // PERM_CHUNK) * PERM_CHUNK
SORTED_ROWS_ALLOC = SORTED_ROWS + ROW_CHUNK


def _plan_kernel(route_ref, posc_ref, posr_ref, base_ref, npad_ref):
    r = route_ref[0]
    tb = r.shape[0]
    lane = lax.broadcasted_iota(jnp.int32, r.shape, 1).astype(F32)
    oh1 = jnp.where(lane == r[:, 0:1], 1.0, 0.0)
    oh2 = jnp.where(lane == r[:, 1:2], 1.0, 0.0)
    cnt1 = jnp.sum(oh1, axis=0, keepdims=True)
    cnt2 = jnp.sum(oh2, axis=0, keepdims=True)
    npad = jnp.floor((cnt1 + cnt2 + (SEG_ALIGN - 1)) * (1.0 / SEG_ALIGN)) * SEG_ALIGN
    ri = lax.broadcasted_iota(jnp.int32, (LANES, LANES), 0)
    ci = lax.broadcasted_iota(jnp.int32, (LANES, LANES), 1)
    upper = jnp.where(ri < ci, 1.0, 0.0)
    npad8 = jnp.broadcast_to(npad, (8, LANES))
    base = jnp.dot(npad8, upper, precision=HIGHEST, preferred_element_type=F32)[0:1]

    ti = lax.broadcasted_iota(jnp.int32, (tb, tb), 0)
    tj = lax.broadcasted_iota(jnp.int32, (tb, tb), 1)
    before = jnp.where(tj < ti, 1.0, 0.0).astype(BF16)
    pre1 = _dot(before, oh1.astype(BF16))
    pre2 = _dot(before, oh2.astype(BF16))
    pos1 = jnp.sum(oh1 * (base + pre1), axis=-1, keepdims=True)
    pos2 = jnp.sum(oh2 * (base + cnt1 + pre2), axis=-1, keepdims=True)
    packed = jnp.where(lane == 0.0, pos1, jnp.where(lane == 1.0, pos2, jnp.where(lane >= 2.0, r, 0.0)))
    posc_ref[0] = packed
    posr_ref[0] = packed.T[0:8, :]
    base_ref[0] = base
    npad_ref[0] = npad


def _moe_plan(route):
    n_blk, tb, _ = route.shape
    vec = pl.BlockSpec((1, 1, LANES), lambda i: (i, 0, 0))
    return pl.pallas_call(
        _plan_kernel,
        grid=(n_blk,),
        in_specs=[pl.BlockSpec((1, tb, ROUTER_W), lambda i: (i, 0, 0))],
        out_specs=[pl.BlockSpec((1, tb, LANES), lambda i: (i, 0, 0)),
                   pl.BlockSpec((1, 8, tb), lambda i: (i, 0, 0)), vec, vec],
        out_shape=[jax.ShapeDtypeStruct((n_blk, tb, LANES), F32),
                   jax.ShapeDtypeStruct((n_blk, 8, tb), F32),
                   jax.ShapeDtypeStruct((n_blk, 1, LANES), F32),
                   jax.ShapeDtypeStruct((n_blk, 1, LANES), F32)],
        compiler_params=pltpu.CompilerParams(dimension_semantics=("parallel",)),
        name="moe_plan",
    )(route)


def _moe_kernel(base_ref, npad_ref, x1_ref, mod_ref, h_ref, posc_ref, posr_ref, wg_ref, wu_ref, wd_ref,
                o_ref, xs_ref, ys_ref):
    blk = pl.program_id(0)
    e = pl.program_id(1)
    tb = h_ref.shape[1]

    @pl.when(e == 0)
    def _():
        pr = posr_ref[0]
        h = h_ref[0]
        for c in range(SORTED_ROWS // PERM_CHUNK):
            rid = (c * PERM_CHUNK + lax.broadcasted_iota(jnp.int32, (PERM_CHUNK, tb), 0)).astype(F32)
            sel = jnp.where(rid == pr[0:1], 1.0, jnp.where(rid == pr[1:2], 1.0, 0.0)).astype(BF16)
            xs_ref[c * PERM_CHUNK:(c + 1) * PERM_CHUNK, :] = _dot(sel, h).astype(BF16)
        xs_ref[SORTED_ROWS:, :] = jnp.zeros((ROW_CHUNK, xs_ref.shape[1]), BF16)
        ys_ref[...] = jnp.zeros(ys_ref.shape, BF16)

    seg_start = base_ref[blk, e]
    seg_rows = npad_ref[blk, e]

    def expert_chunk(c, carry):
        r0 = pl.multiple_of(seg_start + c * ROW_CHUNK, SEG_ALIGN)
        xc = xs_ref[pl.ds(r0, ROW_CHUNK), :]
        a = _dot(xc, wg_ref[0])
        u = _dot(xc, wu_ref[0])
        hid = (a * _sigmoid(a)) * u
        ys_ref[pl.ds(r0, ROW_CHUNK), :] = _dot(hid.astype(BF16), wd_ref[0]).astype(BF16)
        return carry

    lax.fori_loop(0, (seg_rows + ROW_CHUNK - 1) // ROW_CHUNK, expert_chunk, 0)

    @pl.when(e == pl.num_programs(1) - 1)
    def _():
        ys = ys_ref[...]
        for c in range(tb // PERM_CHUNK):
            pc = posc_ref[0, c * PERM_CHUNK:(c + 1) * PERM_CHUNK, :]
            rid = lax.broadcasted_iota(jnp.int32, (PERM_CHUNK, SORTED_ROWS_ALLOC), 1).astype(F32)
            wsel = (jnp.where(rid == pc[:, 0:1], pc[:, 2:3], 0.0)
                    + jnp.where(rid == pc[:, 1:2], pc[:, 3:4], 0.0)).astype(BF16)
            y = _dot(wsel, ys)
            rows = slice(c * PERM_CHUNK, (c + 1) * PERM_CHUNK)
            o_ref[0, rows, :] = x1_ref[0, rows, :] + mod_ref[0, 5:6, :] * y


def _moe(x1, mod_l, h2, route, weg, weu, wed):
    bsz, seq, d = x1.shape
    n_e, _, d_e = weg.shape
    tb = min(MOE_BLOCK, seq)
    assert tb == MOE_BLOCK, "sorted-row capacity is sized for MOE_BLOCK tokens"
    per_batch = seq // tb
    n_blk = bsz * per_batch
    blocked = lambda a: a.reshape(n_blk, tb, a.shape[-1])
    posc, posr, base, npad = _moe_plan(blocked(route))
    base_i = base.reshape(n_blk, LANES).astype(jnp.int32)
    npad_i = npad.reshape(n_blk, LANES).astype(jnp.int32)

    tok = lambda w: pl.BlockSpec((1, tb, w), lambda i, e, *_: (i, 0, 0))
    grid_spec = pltpu.PrefetchScalarGridSpec(
        num_scalar_prefetch=2,
        grid=(n_blk, n_e),
        in_specs=[tok(d),
                  pl.BlockSpec((1, 6, d), lambda i, e, *_: (i // per_batch, 0, 0)),
                  tok(d), tok(LANES),
                  pl.BlockSpec((1, 8, tb), lambda i, e, *_: (i, 0, 0)),
                  pl.BlockSpec((1, d, d_e), lambda i, e, *_: (e, 0, 0)),
                  pl.BlockSpec((1, d, d_e), lambda i, e, *_: (e, 0, 0)),
                  pl.BlockSpec((1, d_e, d), lambda i, e, *_: (e, 0, 0))],
        out_specs=tok(d),
        scratch_shapes=[pltpu.VMEM((SORTED_ROWS_ALLOC, d), BF16),
                        pltpu.VMEM((SORTED_ROWS_ALLOC, d), BF16)],
    )
    out = pl.pallas_call(
        _moe_kernel,
        grid_spec=grid_spec,
        out_shape=jax.ShapeDtypeStruct((n_blk, tb, d), F32),
        compiler_params=pltpu.CompilerParams(dimension_semantics=("parallel", "arbitrary")),
        name="moe_experts",
    )(base_i, npad_i, blocked(x1), mod_l, blocked(h2), posc, posr, weg, weu, wed)
    return out.reshape(bsz, seq, d)


def _tiles(seq):
    return dict(
        tm_proj=min(512, seq),
        tq_a=min(256, seq),
        tk_a=min(512, seq),
        tq_b=128,
        tm_merge=min(512, seq),
    )


def kernel(x, c, positions, w_ada, b_ada, w_in, qn_a, kn_a, lam_q1, lam_k1, lam_q2, lam_k2,
           subln_a, qn_b, kn_b, w_pa, w_pb, w_o, w_r1, b_r1, w_r2, b_r2,
           w_e_gate, w_e_up, w_e_down):
    depth = w_ada.shape[0]
    bsz, seq, d = x.shape
    t = _tiles(seq)

    mod = _ada(c, w_ada, b_ada).reshape(depth, bsz, 6, d)
    cos_l, sin_l = _rope_tables(positions)
    seg = jnp.kron(jnp.eye(LANES // HEAD_DIM, dtype=F32),
                   jnp.full((HEAD_DIM, HEAD_DIM), 1.0 / HEAD_DIM, F32)).astype(BF16)
    q_scale = HEAD_DIM ** -0.5 * LOG2E

    for layer in range(depth):
        lam_init = 0.8 - 0.6 * math.exp(-0.3 * layer)
        w = w_in[layer]
        wqk = jnp.concatenate([w[:, 0:1024], w[:, 1536:3072]], axis=1).astype(BF16)
        wvat = w[:, 1024:1536].T.astype(BF16)
        wvb = w[:, 3072:3840].astype(BF16)
        wg = w[:, 3840:].astype(BF16)
        gain = jnp.concatenate([
            jnp.tile(qn_a[layer] * q_scale, 8), jnp.tile(kn_a[layer], 8),
            jnp.tile(qn_b[layer] * q_scale, 12), jnp.tile(kn_b[layer], 12)]).reshape(1, -1)

        qa, ka, qb, kb, vat, vb, ga, gb = _inproj(x, mod[layer], cos_l, sin_l, gain, wqk, wvat, wvb, wg, seg,
                                                  t["tm_proj"])

        score_bound = (1.01 * HEAD_DIM * q_scale * jnp.max(jnp.abs(qn_a[layer]))
                       * jnp.max(jnp.abs(kn_a[layer]))).reshape(1)
        lam_p = jnp.stack([lam_q1[layer], lam_k1[layer], lam_q2[layer], lam_k2[layer]])
        oa = _attn_a(score_bound, lam_p, subln_a[layer].reshape(1, -1), qa, ka, vat, lam_init,
                     t["tq_a"], t["tk_a"])
        obs = [_attn_b(qb[g], kb[g], vb[g], g, t["tq_b"]) for g in range(B_GROUPS)]

        wr = jnp.zeros((d, ROUTER_W), F32)
        wr = wr.at[:, :N_EXPERTS].set(w_r2[layer]).at[:, N_EXPERTS:N_EXPERTS + N_GROUPS].set(w_r1[layer])
        br = jnp.zeros((1, ROUTER_W), F32)
        br = br.at[0, :N_EXPERTS].set(b_r2[layer]).at[0, N_EXPERTS:N_EXPERTS + N_GROUPS].set(b_r1[layer])
        x1, h2, route = _merge(x, mod[layer], oa, obs, ga, gb,
                               w_pa[layer].astype(BF16), w_pb[layer].astype(BF16), w_o[layer].astype(BF16),
                               wr, br, t["tm_merge"])

        x = _moe(x1, mod[layer], h2, route,
                 w_e_gate[layer].astype(BF16), w_e_up[layer].astype(BF16), w_e_down[layer].astype(BF16))
    return x
```

```python
import functools
import math

import jax
import jax.numpy as jnp
from jax import lax
from jax.experimental import pallas as pl
from jax.experimental.pallas import tpu as pltpu

EPS = 1e-6
ROPE_THETA = 10000.0
LOG2E = math.log2(math.e)
LN2 = math.log(2.0)

A_HEADS = 4
HEAD_DIM = 64
LANES = 128
B_PAIRS = ((128, 1), (512, 4), (2048, 16))
B_GROUPS = len(B_PAIRS)
B_GROUP_W = 256
N_GROUPS = 4
EXPERTS_PER_GROUP = 8
N_EXPERTS = N_GROUPS * EXPERTS_PER_GROUP
ROUTER_W = 128

F32 = jnp.float32
BF16 = jnp.bfloat16
HIGHEST = lax.Precision.HIGHEST


def _dot(a, b):
    return jnp.dot(a, b, preferred_element_type=F32)


def _dot_nt(a, b):
    return lax.dot_general(a, b, (((1,), (1,)), ((), ())), preferred_element_type=F32)


def _sigmoid(x):
    return 1.0 / (1.0 + jnp.exp(-x))


def _ada_kernel(c_ref, w_ref, b_ref, o_ref):
    c = c_ref[...]
    c_act = c * _sigmoid(c)
    o_ref[0] = jnp.dot(c_act, w_ref[0], precision=HIGHEST, preferred_element_type=F32) + b_ref[0]


def _ada(c, w_ada, b_ada):
    depth, d, six_d = w_ada.shape
    bsz = c.shape[0]
    n_col = six_d // d
    return pl.pallas_call(
        _ada_kernel,
        grid=(depth, n_col),
        in_specs=[
            pl.BlockSpec((bsz, d), lambda l, j: (0, 0)),
            pl.BlockSpec((1, d, d), lambda l, j: (l, 0, j)),
            pl.BlockSpec((1, 1, d), lambda l, j: (l, 0, j)),
        ],
        out_specs=pl.BlockSpec((1, bsz, d), lambda l, j: (l, 0, j)),
        out_shape=jax.ShapeDtypeStruct((depth, bsz, six_d), F32),
        name="ada_mod",
    )(c, w_ada, b_ada.reshape(depth, 1, six_d))


def _rope_kernel(pos_ref, f_ref, cos_ref, sin_ref):
    ang = pos_ref[0].astype(F32) * f_ref[...]
    cos_ref[0] = jnp.cos(ang)
    sin_ref[0] = jnp.sin(ang)


def _rope_tables(positions):
    bsz, seq = positions.shape
    half = HEAD_DIM // 2
    inv_freq = ROPE_THETA ** (-jnp.arange(0, HEAD_DIM, 2, dtype=F32) / HEAD_DIM)
    cos_t, sin_t = pl.pallas_call(
        _rope_kernel,
        grid=(bsz,),
        in_specs=[
            pl.BlockSpec((1, 1, seq), lambda b: (b, 0, 0)),
            pl.BlockSpec((half, 1), lambda b: (0, 0)),
        ],
        out_specs=[pl.BlockSpec((1, half, seq), lambda b: (b, 0, 0))] * 2,
        out_shape=[jax.ShapeDtypeStruct((bsz, half, seq), F32)] * 2,
        name="rope_tables",
    )(positions.reshape(bsz, 1, seq), inv_freq.reshape(half, 1))
    cos = cos_t.transpose(0, 2, 1)
    sin = sin_t.transpose(0, 2, 1)
    cos_l = jnp.concatenate([cos, cos, cos, cos], axis=-1)
    sin_l = jnp.concatenate([-sin, sin, -sin, sin], axis=-1)
    return cos_l, sin_l


def _inproj_kernel(x_ref, mod_ref, cos_ref, sin_ref, gain_ref, wqk_ref, wvat_ref, wvb_ref, wg_ref, seg_ref,
                   qa_ref, ka_ref, qb0_ref, qb1_ref, qb2_ref, kb0_ref, kb1_ref, kb2_ref,
                   vat_ref, vb0_ref, vb1_ref, vb2_ref, ga_ref, gb_ref, stage_ref):
    x = x_ref[0]
    tm = x.shape[0]
    ms = jnp.mean(x * x, axis=-1, keepdims=True)
    h = x * lax.rsqrt(ms + EPS) * (1.0 + mod_ref[0, 1:2, :]) + mod_ref[0, 0:1, :]
    hb = h.astype(BF16)

    cos = cos_ref[0]
    sin = sin_ref[0]
    seg = seg_ref[...]
    lane = lax.broadcasted_iota(jnp.int32, cos.shape, 1)
    first_half = (lane % HEAD_DIM) < (HEAD_DIM // 2)

    def norm_rope(y, j):
        sq = y * y
        hi = sq.astype(BF16)
        lo = (sq - hi.astype(F32)).astype(BF16)
        msq = _dot(hi, seg) + _dot(lo, seg)
        yn = y * lax.rsqrt(msq + EPS) * gain_ref[:, j * LANES:(j + 1) * LANES]
        partner = jnp.where(first_half, pltpu.roll(yn, 96, axis=1), pltpu.roll(yn, 32, axis=1))
        return yn * cos + partner * sin

    def store_dilated(ref, group, val):
        dil = B_PAIRS[group][1]
        if dil == 1:
            ref[0] = val.astype(BF16)
            return
        for hf in range(B_GROUP_W // LANES):
            stage_ref[hf] = val[:, hf * LANES:(hf + 1) * LANES]
        for r in range(dil):
            for hf in range(B_GROUP_W // LANES):
                rows = stage_ref[hf, pl.ds(r, tm // dil, stride=dil), :]
                col = r * B_GROUP_W + hf * LANES
                ref[0, :, col:col + LANES] = rows.astype(BF16)

    flat = [(qa_ref, 0), (qa_ref, 1), (ka_ref, 0), (ka_ref, 1)]
    grouped = [(qb0_ref, 0), (qb1_ref, 1), (qb2_ref, 2), (kb0_ref, 0), (kb1_ref, 1), (kb2_ref, 2)]
    for c in range(len(flat) + len(grouped)):
        y2 = _dot(hb, wqk_ref[:, c * 256:(c + 1) * 256])
        halves = [norm_rope(y2[:, hf * LANES:(hf + 1) * LANES], 2 * c + hf) for hf in range(2)]
        if c < len(flat):
            ref, t = flat[c]
            for hf in range(2):
                ref[0, :, t * 256 + hf * LANES:t * 256 + (hf + 1) * LANES] = halves[hf].astype(BF16)
        else:
            ref, group = grouped[c - len(flat)]
            store_dilated(ref, group, jnp.concatenate(halves, axis=1))

    for c in range(wvat_ref.shape[0] // 256):
        vat_ref[0, c * 256:(c + 1) * 256, :] = _dot_nt(wvat_ref[c * 256:(c + 1) * 256, :], hb).astype(BF16)
    for group, ref in enumerate((vb0_ref, vb1_ref, vb2_ref)):
        store_dilated(ref, group, _dot(hb, wvb_ref[:, group * 256:(group + 1) * 256]))

    g_dests = [(ga_ref, t) for t in range(4)] + [(gb_ref, t) for t in range(4)]
    for c, (ref, t) in enumerate(g_dests):
        g = _dot(hb, wg_ref[:, c * 256:(c + 1) * 256])
        ref[0, :, t * 256:(t + 1) * 256] = _sigmoid(g).astype(BF16)


def _inproj(x, mod_l, cos_l, sin_l, gain, wqk, wvat, wvb, wg, seg, tm):
    bsz, seq, d = x.shape
    a_v_w = wvat.shape[0]
    tok = lambda w: pl.BlockSpec((1, tm, w), lambda b, i: (b, i, 0))
    tok_t = pl.BlockSpec((1, a_v_w, tm), lambda b, i: (b, 0, i))
    const = lambda shape: pl.BlockSpec(shape, lambda b, i: (0,) * len(shape),
                                       pipeline_mode=pl.Buffered(1))
    row = lambda w: jax.ShapeDtypeStruct((bsz, seq, w), BF16)
    dil_specs = [pl.BlockSpec((1, tm // dl, dl * B_GROUP_W), lambda b, i: (b, i, 0)) for _, dl in B_PAIRS]
    dil_shapes = [jax.ShapeDtypeStruct((bsz, seq // dl, dl * B_GROUP_W), BF16) for _, dl in B_PAIRS]
    outs = pl.pallas_call(
        _inproj_kernel,
        grid=(bsz, seq // tm),
        in_specs=[
            tok(d),
            pl.BlockSpec((1, 6, d), lambda b, i: (b, 0, 0)),
            tok(LANES), tok(LANES),
            const(gain.shape), const(wqk.shape), const(wvat.shape), const(wvb.shape), const(wg.shape),
            const(seg.shape),
        ],
        out_specs=[tok(512), tok(512)] + dil_specs + dil_specs + [tok_t] + dil_specs + [tok(d), tok(d)],
        out_shape=[row(512), row(512)] + dil_shapes + dil_shapes
                  + [jax.ShapeDtypeStruct((bsz, a_v_w, seq), BF16)] + dil_shapes + [row(d), row(d)],
        scratch_shapes=[pltpu.VMEM((B_GROUP_W // LANES, tm, LANES), F32)],
        compiler_params=pltpu.CompilerParams(dimension_semantics=("parallel", "parallel")),
        name="in_proj",
    )(x, mod_l, cos_l, sin_l, gain, wqk, wvat, wvb, wg, seg)
    qa, ka = outs[0], outs[1]
    qb, kb, vat, vb, ga, gb = outs[2:5], outs[5:8], outs[8], outs[9:12], outs[12], outs[13]
    return qa, ka, qb, kb, vat, vb, ga, gb


SCORE_BOUND_NO_SHIFT = 64.0
KV_CHUNK = 256


def _attn_a_kernel(bound_ref, lam_ref, sub_ref, q_ref, k_ref, vt_ref, o_ref,
                   m_ref, l_ref, acc_ref, pa_ref, pb_ref, *, tk, lam_init):
    q = q_ref[0]
    tq = q.shape[0]
    seq = k_ref.shape[1]
    lane = lax.broadcasted_iota(jnp.int32, q.shape, 1)
    zero = jnp.zeros_like(q)
    q_maps = (jnp.where(lane < HEAD_DIM, q, zero), jnp.where(lane >= HEAD_DIM, q, zero))
    n_chunk = tk // KV_CHUNK

    acc_ref[...] = jnp.zeros(acc_ref.shape, F32)
    l_ref[...] = jnp.zeros(l_ref.shape, F32)

    def scores_exp(tile, p_ref):
        start = pl.multiple_of(tile * tk, tk)
        k = k_ref[0, pl.ds(start, tk), :]
        for mi in range(2):
            p = jnp.exp2(_dot_nt(k, q_maps[mi]))
            l_ref[mi] += jnp.sum(p.reshape(tk // 8, 8, tq), axis=0)
            p_ref[mi] = p.astype(BF16)

    def weighted_values(tile, p_ref):
        start = pl.multiple_of(tile * tk, tk)
        vt = vt_ref[0, :, pl.ds(start, tk)]
        for mi in range(2):
            acc_ref[mi] += _dot(vt, p_ref[mi])

    n_tiles = seq // tk

    def no_shift_pair(jj, carry):
        t = 2 * jj
        scores_exp(t + 1, pb_ref)
        weighted_values(t, pa_ref)
        scores_exp(t + 2, pa_ref)
        weighted_values(t + 1, pb_ref)
        return carry

    def no_shift_loop():
        scores_exp(0, pa_ref)
        lax.fori_loop(0, n_tiles // 2 - 1, no_shift_pair, 0)
        scores_exp(n_tiles - 1, pb_ref)
        weighted_values(n_tiles - 2, pa_ref)
        weighted_values(n_tiles - 1, pb_ref)

    def online_max_body(j, carry):
        for c in range(n_chunk):
            start = pl.multiple_of(j * tk + c * KV_CHUNK, KV_CHUNK)
            k = k_ref[0, pl.ds(start, KV_CHUNK), :]
            vt = vt_ref[0, :, pl.ds(start, KV_CHUNK)]
            for mi in range(2):
                s = _dot_nt(k, q_maps[mi])
                m_old = m_ref[mi]
                m_new = jnp.maximum(m_old, jnp.max(s, axis=0, keepdims=True))
                alpha = jnp.exp2(m_old - m_new)
                p = jnp.exp2(s - m_new[0:1])
                l_ref[mi] = alpha * l_ref[mi] + jnp.sum(p.reshape(KV_CHUNK // 8, 8, tq), axis=0)
                acc_ref[mi] = alpha[0:1] * acc_ref[mi] + _dot(vt, p.astype(BF16))
                m_ref[mi] = m_new
        return carry

    no_shift = bound_ref[0] <= SCORE_BOUND_NO_SHIFT

    @pl.when(no_shift)
    def _():
        no_shift_loop()

    @pl.when(jnp.logical_not(no_shift))
    def _():
        m_ref[...] = jnp.full(m_ref.shape, -jnp.inf, F32)
        lax.fori_loop(0, seq // tk, online_max_body, 0)

    lam_p = lam_ref[...]
    s1 = jnp.sum(lam_p[0:1] * lam_p[1:2], axis=-1, keepdims=True)
    s2 = jnp.sum(lam_p[2:3] * lam_p[3:4], axis=-1, keepdims=True)
    lam = jnp.exp(s1) - jnp.exp(s2) + lam_init
    l0 = jnp.sum(l_ref[0], axis=0, keepdims=True)
    l1 = jnp.sum(l_ref[1], axis=0, keepdims=True)
    ot = acc_ref[0] / l0 - lam * (acc_ref[1] / l1)
    o = ot.T
    msq = jnp.mean(o * o, axis=-1, keepdims=True)
    o = o * lax.rsqrt(msq + EPS) * sub_ref[...] * (1.0 - lam_init)
    o_ref[0] = o.astype(BF16)


def _attn_a(score_bound, lam_p, subln, qa, ka, vat, lam_init, tq, tk):
    bsz, seq, _ = qa.shape
    return pl.pallas_call(
        functools.partial(_attn_a_kernel, tk=tk, lam_init=lam_init),
        grid=(bsz, A_HEADS, seq // tq),
        in_specs=[
            pl.BlockSpec(memory_space=pltpu.SMEM),
            pl.BlockSpec(lam_p.shape, lambda b, h, i: (0, 0)),
            pl.BlockSpec(subln.shape, lambda b, h, i: (0, 0)),
            pl.BlockSpec((1, tq, LANES), lambda b, h, i: (b, i, h)),
            pl.BlockSpec((1, seq, LANES), lambda b, h, i: (b, 0, h)),
            pl.BlockSpec((1, LANES, seq), lambda b, h, i: (b, h, 0)),
        ],
        out_specs=pl.BlockSpec((1, tq, LANES), lambda b, h, i: (b, i, h)),
        out_shape=jax.ShapeDtypeStruct((bsz, seq, A_HEADS * LANES), BF16),
        scratch_shapes=[
            pltpu.VMEM((2, 8, tq), F32),
            pltpu.VMEM((2, 8, tq), F32),
            pltpu.VMEM((2, LANES, tq), F32),
            pltpu.VMEM((2, tk, tq), BF16),
            pltpu.VMEM((2, tk, tq), BF16),
        ],
        compiler_params=pltpu.CompilerParams(dimension_semantics=("parallel", "parallel", "parallel")),
        name="diff_attn",
    )(score_bound, lam_p, subln, qa, ka, vat)


def _attn_b_kernel(q_ref, k_ref, v_ref, o_ref, lse_ref, *, radius):
    tq = q_ref.shape[1]
    length = k_ref.shape[1]
    win = tq + 2 * radius
    t0 = pl.program_id(2) * tq
    start = pl.multiple_of(jnp.clip(t0 - radius, 0, length - win), radius)
    kw = k_ref[0, pl.ds(start, win), :]
    vw = v_ref[0, pl.ds(start, win), :]
    q = q_ref[0]

    qpos = t0 + lax.broadcasted_iota(jnp.int32, (tq, win), 0)
    kpos = start + lax.broadcasted_iota(jnp.int32, (tq, win), 1)
    valid = jnp.abs(kpos - qpos) <= radius
    lane = lax.broadcasted_iota(jnp.int32, (tq, LANES), 1)
    low = lane < HEAD_DIM

    for c in range(B_GROUP_W // LANES):
        qc = q[:, c * LANES:(c + 1) * LANES]
        kc = kw[:, c * LANES:(c + 1) * LANES]
        vc = vw[:, c * LANES:(c + 1) * LANES]
        zero = jnp.zeros_like(qc)
        outs, lses = [], []
        for half in range(2):
            qm = jnp.where(low if half == 0 else jnp.logical_not(low), qc, zero)
            s = jnp.where(valid, _dot_nt(qm, kc), -jnp.inf)
            m = jnp.max(s, axis=-1, keepdims=True)
            p = jnp.exp2(s - m)
            l = jnp.sum(p, axis=-1, keepdims=True)
            outs.append(_dot(p.astype(BF16), vc) / l)
            lses.append((m + jnp.log2(l)) * LN2)
        o_ref[0, :, c * LANES:(c + 1) * LANES] = jnp.where(low, outs[0], outs[1]).astype(BF16)
        lse_ref[0, :, c * LANES:(c + 1) * LANES] = jnp.where(low, lses[0], lses[1])


def _attn_b(qg, kg, vg, group, tq):
    window, dilation = B_PAIRS[group]
    radius = window // (2 * dilation)
    bsz, length, _ = qg.shape
    return pl.pallas_call(
        functools.partial(_attn_b_kernel, radius=radius),
        grid=(bsz, dilation, length // tq),
        in_specs=[
            pl.BlockSpec((1, tq, B_GROUP_W), lambda b, r, i: (b, i, r)),
            pl.BlockSpec((1, length, B_GROUP_W), lambda b, r, i: (b, 0, r)),
            pl.BlockSpec((1, length, B_GROUP_W), lambda b, r, i: (b, 0, r)),
        ],
        out_specs=[pl.BlockSpec((1, tq, B_GROUP_W), lambda b, r, i: (b, i, r))] * 2,
        out_shape=[jax.ShapeDtypeStruct((bsz, length, dilation * B_GROUP_W), BF16),
                   jax.ShapeDtypeStruct((bsz, length, dilation * B_GROUP_W), F32)],
        compiler_params=pltpu.CompilerParams(dimension_semantics=("parallel", "parallel", "parallel")),
        name=f"band_attn_g{group}",
    )(qg, kg, vg)


def _merge_kernel(x_ref, mod_ref, oa_ref, ob0_ref, ls0_ref, ob1_ref, ls1_ref, ob2_ref, ls2_ref,
                  ga_ref, gb_ref, wpa_ref, wpb_ref, wo_ref, wr_ref, br_ref,
                  x1_ref, h2_ref, route_ref, so1_ref, sl1_ref, so2_ref, sl2_ref):
    tm = x_ref.shape[1]

    def token_major(o_ref, l_ref, group, so_ref, sl_ref):
        dil = B_PAIRS[group][1]
        n_hf = B_GROUP_W // LANES
        for r in range(dil):
            for hf in range(n_hf):
                cols = slice(r * B_GROUP_W + hf * LANES, r * B_GROUP_W + (hf + 1) * LANES)
                so_ref[hf, pl.ds(r, tm // dil, stride=dil), :] = o_ref[0, :, cols].astype(F32)
                sl_ref[hf, pl.ds(r, tm // dil, stride=dil), :] = l_ref[0, :, cols]
        return (jnp.concatenate([so_ref[hf] for hf in range(n_hf)], axis=1),
                jnp.concatenate([sl_ref[hf] for hf in range(n_hf)], axis=1))

    o0, ls0 = ob0_ref[0].astype(F32), ls0_ref[0]
    o1, ls1 = token_major(ob1_ref, ls1_ref, 1, so1_ref, sl1_ref)
    o2, ls2 = token_major(ob2_ref, ls2_ref, 2, so2_ref, sl2_ref)
    mx = jnp.maximum(jnp.maximum(ls0, ls1), ls2)
    e0, e1, e2 = jnp.exp(ls0 - mx), jnp.exp(ls1 - mx), jnp.exp(ls2 - mx)
    ob = (e0 * o0 + e1 * o1 + e2 * o2) / (e0 + e1 + e2)

    pa = _dot(oa_ref[0], wpa_ref[...])
    pb = _dot(ob.astype(BF16), wpb_ref[...])
    merged = ga_ref[0].astype(F32) * pa + gb_ref[0].astype(F32) * pb
    y = _dot(merged.astype(BF16), wo_ref[...])
    x1 = x_ref[0] + mod_ref[0, 2:3, :] * y
    x1_ref[0] = x1

    ms = jnp.mean(x1 * x1, axis=-1, keepdims=True)
    h2 = x1 * lax.rsqrt(ms + EPS) * (1.0 + mod_ref[0, 4:5, :]) + mod_ref[0, 3:4, :]
    h2_ref[0] = h2.astype(BF16)

    logits = jnp.dot(h2, wr_ref[...], precision=HIGHEST, preferred_element_type=F32) + br_ref[...]
    lane = lax.broadcasted_iota(jnp.int32, logits.shape, 1)
    neg = -jnp.inf
    big = ROUTER_W
    is_grp = (lane >= N_EXPERTS) & (lane < N_EXPERTS + N_GROUPS)
    lg = jnp.where(is_grp, logits, neg)
    mg = jnp.max(lg, axis=-1, keepdims=True)
    g_lane = jnp.min(jnp.where(lg == mg, lane, big), axis=-1, keepdims=True)
    g_val = 1.0 / jnp.sum(jnp.exp(lg - mg), axis=-1, keepdims=True)
    lo = (g_lane - N_EXPERTS) * EXPERTS_PER_GROUP
    in_grp = (lane >= lo) & (lane < lo + EXPERTS_PER_GROUP)
    le = jnp.where(in_grp, logits, neg)
    m1 = jnp.max(le, axis=-1, keepdims=True)
    i1 = jnp.min(jnp.where(le == m1, lane, big), axis=-1, keepdims=True)
    le2 = jnp.where(lane == i1, neg, le)
    m2 = jnp.max(le2, axis=-1, keepdims=True)
    i2 = jnp.min(jnp.where(le2 == m2, lane, big), axis=-1, keepdims=True)
    e = jnp.exp(m2 - m1)
    w1 = g_val / (1.0 + e)
    w2 = g_val * e / (1.0 + e)
    route_ref[0] = jnp.where(lane == 0, i1.astype(F32),
                             jnp.where(lane == 1, i2.astype(F32),
                                       jnp.where(lane == 2, w1, jnp.where(lane == 3, w2, 0.0))))


def _merge(x, mod_l, oa, obs, ga, gb, wpa, wpb, wo, wr, br, tm):
    bsz, seq, d = x.shape
    tok = lambda w: pl.BlockSpec((1, tm, w), lambda b, i: (b, i, 0))
    const = lambda shape: pl.BlockSpec(shape, lambda b, i: (0,) * len(shape),
                                       pipeline_mode=pl.Buffered(1))
    ob_args, ob_specs = [], []
    for (o, lse), (_, dl) in zip(obs, B_PAIRS):
        ob_args += [o, lse]
        ob_specs += [pl.BlockSpec((1, tm // dl, dl * B_GROUP_W), lambda b, i: (b, i, 0))] * 2
    return pl.pallas_call(
        _merge_kernel,
        grid=(bsz, seq // tm),
        in_specs=[tok(d), pl.BlockSpec((1, 6, d), lambda b, i: (b, 0, 0)), tok(oa.shape[-1])]
                 + ob_specs + [tok(d), tok(d)]
                 + [const(w.shape) for w in (wpa, wpb, wo, wr, br)],
        out_specs=[tok(d), tok(d), tok(ROUTER_W)],
        out_shape=[jax.ShapeDtypeStruct((bsz, seq, d), F32),
                   jax.ShapeDtypeStruct((bsz, seq, d), BF16),
                   jax.ShapeDtypeStruct((bsz, seq, ROUTER_W), F32)],
        scratch_shapes=[pltpu.VMEM((B_GROUP_W // LANES, tm, LANES), F32)] * 4,
        compiler_params=pltpu.CompilerParams(dimension_semantics=("parallel", "parallel")),
        name="merge_proj",
    )(x, mod_l, oa, *ob_args, ga, gb, wpa, wpb, wo, wr, br)


MOE_BLOCK = 1024
SEG_ALIGN = 16
ROW_CHUNK = 128
PERM_CHUNK = 256
EXPERTS_PER_STEP = 4
SORTED_ROWS = -(-(2 * MOE_BLOCK + N_EXPERTS * (SEG_ALIGN - 1)) // PERM_CHUNK) * PERM_CHUNK
SORTED_ROWS_ALLOC = SORTED_ROWS + ROW_CHUNK


def _plan_kernel(route_ref, posc_ref, posr_ref, base_ref, npad_ref):
    r = route_ref[0]
    tb = r.shape[0]
    lane = lax.broadcasted_iota(jnp.int32, r.shape, 1).astype(F32)
    oh1 = jnp.where(lane == r[:, 0:1], 1.0, 0.0)
    oh2 = jnp.where(lane == r[:, 1:2], 1.0, 0.0)
    cnt1 = jnp.sum(oh1, axis=0, keepdims=True)
    cnt2 = jnp.sum(oh2, axis=0, keepdims=True)
    npad = jnp.floor((cnt1 + cnt2 + (SEG_ALIGN - 1)) * (1.0 / SEG_ALIGN)) * SEG_ALIGN
    ri = lax.broadcasted_iota(jnp.int32, (LANES, LANES), 0)
    ci = lax.broadcasted_iota(jnp.int32, (LANES, LANES), 1)
    upper = jnp.where(ri < ci, 1.0, 0.0)
    npad8 = jnp.broadcast_to(npad, (8, LANES))
    base = jnp.dot(npad8, upper, precision=HIGHEST, preferred_element_type=F32)[0:1]

    ti = lax.broadcasted_iota(jnp.int32, (tb, tb), 0)
    tj = lax.broadcasted_iota(jnp.int32, (tb, tb), 1)
    before = jnp.where(tj < ti, 1.0, 0.0).astype(BF16)
    pre1 = _dot(before, oh1.astype(BF16))
    pre2 = _dot(before, oh2.astype(BF16))
    pos1 = jnp.sum(oh1 * (base + pre1), axis=-1, keepdims=True)
    pos2 = jnp.sum(oh2 * (base + cnt1 + pre2), axis=-1, keepdims=True)
    packed = jnp.where(lane == 0.0, pos1, jnp.where(lane == 1.0, pos2, jnp.where(lane >= 2.0, r, 0.0)))
    posc_ref[0] = packed
    posr_ref[0] = packed.T[0:8, :]
    base_ref[0] = base
    npad_ref[0] = npad


def _moe_plan(route):
    n_blk, tb, _ = route.shape
    vec = pl.BlockSpec((1, 1, LANES), lambda i: (i, 0, 0))
    return pl.pallas_call(
        _plan_kernel,
        grid=(n_blk,),
        in_specs=[pl.BlockSpec((1, tb, ROUTER_W), lambda i: (i, 0, 0))],
        out_specs=[pl.BlockSpec((1, tb, LANES), lambda i: (i, 0, 0)),
                   pl.BlockSpec((1, 8, tb), lambda i: (i, 0, 0)), vec, vec],
        out_shape=[jax.ShapeDtypeStruct((n_blk, tb, LANES), F32),
                   jax.ShapeDtypeStruct((n_blk, 8, tb), F32),
                   jax.ShapeDtypeStruct((n_blk, 1, LANES), F32),
                   jax.ShapeDtypeStruct((n_blk, 1, LANES), F32)],
        compiler_params=pltpu.CompilerParams(dimension_semantics=("parallel",)),
        name="moe_plan",
    )(route)


def _moe_kernel(base_ref, npad_ref, x1_ref, mod_ref, h_ref, posc_ref, posr_ref, wg_ref, wu_ref, wd_ref,
                o_ref, xs_ref, ys_ref):
    blk = pl.program_id(0)
    e = pl.program_id(1)
    tb = h_ref.shape[1]

    @pl.when(e == 0)
    def _():
        pr = posr_ref[0]
        h = h_ref[0]
        for c in range(SORTED_ROWS // PERM_CHUNK):
            rid = (c * PERM_CHUNK + lax.broadcasted_iota(jnp.int32, (PERM_CHUNK, tb), 0)).astype(F32)
            sel = jnp.where(rid == pr[0:1], 1.0, jnp.where(rid == pr[1:2], 1.0, 0.0)).astype(BF16)
            xs_ref[c * PERM_CHUNK:(c + 1) * PERM_CHUNK, :] = _dot(sel, h).astype(BF16)
        xs_ref[SORTED_ROWS:, :] = jnp.zeros((ROW_CHUNK, xs_ref.shape[1]), BF16)
        ys_ref[...] = jnp.zeros(ys_ref.shape, BF16)

    def gate_up(r0, j):
        xc = xs_ref[pl.ds(r0, ROW_CHUNK), :]
        return _dot(xc, wg_ref[j]), _dot(xc, wu_ref[j])

    def down(r0, j, a, u):
        hid = (a * _sigmoid(a)) * u
        ys_ref[pl.ds(r0, ROW_CHUNK), :] = _dot(hid.astype(BF16), wd_ref[j]).astype(BF16)

    n_here = wg_ref.shape[0]
    starts = [pl.multiple_of(base_ref[blk, e * n_here + j], SEG_ALIGN) for j in range(n_here)]
    first = [gate_up(starts[j], j) for j in range(n_here)]
    for j in range(n_here):
        down(starts[j], j, *first[j])
    for j in range(n_here):
        seg_rows = npad_ref[blk, e * n_here + j]

        def more(c, carry, j=j, seg_rows=seg_rows):
            r0 = starts[j] + jnp.minimum(c * ROW_CHUNK, seg_rows - ROW_CHUNK)
            r0 = pl.multiple_of(r0, SEG_ALIGN)
            down(r0, j, *gate_up(r0, j))
            return carry

        lax.fori_loop(1, (seg_rows + ROW_CHUNK - 1) // ROW_CHUNK, more, 0)

    @pl.when(e == pl.num_programs(1) - 1)
    def _():
        ys = ys_ref[...]
        for c in range(tb // PERM_CHUNK):
            pc = posc_ref[0, c * PERM_CHUNK:(c + 1) * PERM_CHUNK, :]
            rid = lax.broadcasted_iota(jnp.int32, (PERM_CHUNK, SORTED_ROWS_ALLOC), 1).astype(F32)
            wsel = (jnp.where(rid == pc[:, 0:1], pc[:, 2:3], 0.0)
                    + jnp.where(rid == pc[:, 1:2], pc[:, 3:4], 0.0)).astype(BF16)
            y = _dot(wsel, ys)
            rows = slice(c * PERM_CHUNK, (c + 1) * PERM_CHUNK)
            o_ref[0, rows, :] = x1_ref[0, rows, :] + mod_ref[0, 5:6, :] * y


def _moe(x1, mod_l, h2, route, weg, weu, wed):
    bsz, seq, d = x1.shape
    n_e, _, d_e = weg.shape
    tb = min(MOE_BLOCK, seq)
    assert tb == MOE_BLOCK, "sorted-row capacity is sized for MOE_BLOCK tokens"
    per_batch = seq // tb
    n_blk = bsz * per_batch
    blocked = lambda a: a.reshape(n_blk, tb, a.shape[-1])
    posc, posr, base, npad = _moe_plan(blocked(route))
    base_i = base.reshape(n_blk, LANES).astype(jnp.int32)
    npad_i = npad.reshape(n_blk, LANES).astype(jnp.int32)

    tok = lambda w: pl.BlockSpec((1, tb, w), lambda i, e, *_: (i, 0, 0))
    grid_spec = pltpu.PrefetchScalarGridSpec(
        num_scalar_prefetch=2,
        grid=(n_blk, n_e // EXPERTS_PER_STEP),
        in_specs=[tok(d),
                  pl.BlockSpec((1, 6, d), lambda i, e, *_: (i // per_batch, 0, 0)),
                  tok(d), tok(LANES),
                  pl.BlockSpec((1, 8, tb), lambda i, e, *_: (i, 0, 0)),
                  pl.BlockSpec((EXPERTS_PER_STEP, d, d_e), lambda i, e, *_: (e, 0, 0)),
                  pl.BlockSpec((EXPERTS_PER_STEP, d, d_e), lambda i, e, *_: (e, 0, 0)),
                  pl.BlockSpec((EXPERTS_PER_STEP, d_e, d), lambda i, e, *_: (e, 0, 0))],
        out_specs=tok(d),
        scratch_shapes=[pltpu.VMEM((SORTED_ROWS_ALLOC, d), BF16),
                        pltpu.VMEM((SORTED_ROWS_ALLOC, d), BF16)],
    )
    out = pl.pallas_call(
        _moe_kernel,
        grid_spec=grid_spec,
        out_shape=jax.ShapeDtypeStruct((n_blk, tb, d), F32),
        compiler_params=pltpu.CompilerParams(dimension_semantics=("parallel", "arbitrary")),
        name="moe_experts",
    )(base_i, npad_i, blocked(x1), mod_l, blocked(h2), posc, posr, weg, weu, wed)
    return out.reshape(bsz, seq, d)


def _tiles(seq):
    return dict(
        tm_proj=min(512, seq),
        tq_a=min(512, seq),
        tk_a=min(1024, seq),
        tq_b=128,
        tm_merge=min(512, seq),
    )


def kernel(x, c, positions, w_ada, b_ada, w_in, qn_a, kn_a, lam_q1, lam_k1, lam_q2, lam_k2,
           subln_a, qn_b, kn_b, w_pa, w_pb, w_o, w_r1, b_r1, w_r2, b_r2,
           w_e_gate, w_e_up, w_e_down):
    depth = w_ada.shape[0]
    bsz, seq, d = x.shape
    t = _tiles(seq)

    mod = _ada(c, w_ada, b_ada).reshape(depth, bsz, 6, d)
    cos_l, sin_l = _rope_tables(positions)
    seg = jnp.kron(jnp.eye(LANES // HEAD_DIM, dtype=F32),
                   jnp.full((HEAD_DIM, HEAD_DIM), 1.0 / HEAD_DIM, F32)).astype(BF16)
    q_scale = HEAD_DIM ** -0.5 * LOG2E

    for layer in range(depth):
        lam_init = 0.8 - 0.6 * math.exp(-0.3 * layer)
        w = w_in[layer]
        wqk = jnp.concatenate([w[:, 0:1024], w[:, 1536:3072]], axis=1).astype(BF16)
        wvat = w[:, 1024:1536].T.astype(BF16)
        wvb = w[:, 3072:3840].astype(BF16)
        wg = w[:, 3840:].astype(BF16)
        gain = jnp.concatenate([
            jnp.tile(qn_a[layer] * q_scale, 8), jnp.tile(kn_a[layer], 8),
            jnp.tile(qn_b[layer] * q_scale, 12), jnp.tile(kn_b[layer], 12)]).reshape(1, -1)

        qa, ka, qb, kb, vat, vb, ga, gb = _inproj(x, mod[layer], cos_l, sin_l, gain, wqk, wvat, wvb, wg, seg,
                                                  t["tm_proj"])

        score_bound = (1.01 * HEAD_DIM * q_scale * jnp.max(jnp.abs(qn_a[layer]))
                       * jnp.max(jnp.abs(kn_a[layer]))).reshape(1)
        lam_p = jnp.stack([lam_q1[layer], lam_k1[layer], lam_q2[layer], lam_k2[layer]])
        oa = _attn_a(score_bound, lam_p, subln_a[layer].reshape(1, -1), qa, ka, vat, lam_init,
                     t["tq_a"], t["tk_a"])
        obs = [_attn_b(qb[g], kb[g], vb[g], g, t["tq_b"]) for g in range(B_GROUPS)]

        wr = jnp.zeros((d, ROUTER_W), F32)
        wr = wr.at[:, :N_EXPERTS].set(w_r2[layer]).at[:, N_EXPERTS:N_EXPERTS + N_GROUPS].set(w_r1[layer])
        br = jnp.zeros((1, ROUTER_W), F32)
        br = br.at[0, :N_EXPERTS].set(b_r2[layer]).at[0, N_EXPERTS:N_EXPERTS + N_GROUPS].set(b_r1[layer])
        x1, h2, route = _merge(x, mod[layer], oa, obs, ga, gb,
                               w_pa[layer].astype(BF16), w_pb[layer].astype(BF16), w_o[layer].astype(BF16),
                               wr, br, t["tm_merge"])

        x = _moe(x1, mod[layer], h2, route,
                 w_e_gate[layer].astype(BF16), w_e_up[layer].astype(BF16), w_e_down[layer].astype(BF16))
    return x
```

```python
import functools
import math

import jax
import jax.numpy as jnp
from jax import lax
from jax.experimental import pallas as pl
from jax.experimental.pallas import tpu as pltpu

EPS = 1e-6
ROPE_THETA = 10000.0
LOG2E = math.log2(math.e)
LN2 = math.log(2.0)

A_HEADS = 4
HEAD_DIM = 64
LANES = 128
B_PAIRS = ((128, 1), (512, 4), (2048, 16))
B_GROUPS = len(B_PAIRS)
B_GROUP_W = 256
N_GROUPS = 4
EXPERTS_PER_GROUP = 8
N_EXPERTS = N_GROUPS * EXPERTS_PER_GROUP
ROUTER_W = 128

F32 = jnp.float32
BF16 = jnp.bfloat16
HIGHEST = lax.Precision.HIGHEST


def _dot(a, b):
    return jnp.dot(a, b, preferred_element_type=F32)


def _dot_nt(a, b):
    return lax.dot_general(a, b, (((1,), (1,)), ((), ())), preferred_element_type=F32)


def _sigmoid(x):
    return 1.0 / (1.0 + jnp.exp(-x))


def _ada_kernel(c_ref, w_ref, b_ref, o_ref):
    c = c_ref[...]
    c_act = c * _sigmoid(c)
    o_ref[0] = jnp.dot(c_act, w_ref[0], precision=HIGHEST, preferred_element_type=F32) + b_ref[0]


def _ada(c, w_ada, b_ada):
    depth, d, six_d = w_ada.shape
    bsz = c.shape[0]
    n_col = six_d // d
    return pl.pallas_call(
        _ada_kernel,
        grid=(depth, n_col),
        in_specs=[
            pl.BlockSpec((bsz, d), lambda l, j: (0, 0)),
            pl.BlockSpec((1, d, d), lambda l, j: (l, 0, j)),
            pl.BlockSpec((1, 1, d), lambda l, j: (l, 0, j)),
        ],
        out_specs=pl.BlockSpec((1, bsz, d), lambda l, j: (l, 0, j)),
        out_shape=jax.ShapeDtypeStruct((depth, bsz, six_d), F32),
        name="ada_mod",
    )(c, w_ada, b_ada.reshape(depth, 1, six_d))


def _rope_kernel(pos_ref, f_ref, cos_ref, sin_ref):
    ang = pos_ref[0].astype(F32) * f_ref[...]
    cos_ref[0] = jnp.cos(ang)
    sin_ref[0] = jnp.sin(ang)


def _rope_tables(positions):
    bsz, seq = positions.shape
    half = HEAD_DIM // 2
    inv_freq = ROPE_THETA ** (-jnp.arange(0, HEAD_DIM, 2, dtype=F32) / HEAD_DIM)
    cos_t, sin_t = pl.pallas_call(
        _rope_kernel,
        grid=(bsz,),
        in_specs=[
            pl.BlockSpec((1, 1, seq), lambda b: (b, 0, 0)),
            pl.BlockSpec((half, 1), lambda b: (0, 0)),
        ],
        out_specs=[pl.BlockSpec((1, half, seq), lambda b: (b, 0, 0))] * 2,
        out_shape=[jax.ShapeDtypeStruct((bsz, half, seq), F32)] * 2,
        name="rope_tables",
    )(positions.reshape(bsz, 1, seq), inv_freq.reshape(half, 1))
    cos = cos_t.transpose(0, 2, 1)
    sin = sin_t.transpose(0, 2, 1)
    cos_l = jnp.concatenate([cos, cos, cos, cos], axis=-1)
    sin_l = jnp.concatenate([-sin, sin, -sin, sin], axis=-1)
    return cos_l, sin_l


def _inproj_kernel(x_ref, mod_ref, cos_ref, sin_ref, gain_ref, wqk_ref, wvat_ref, wvb_ref, wg_ref, seg_ref,
                   qa_ref, ka_ref, qb0_ref, qb1_ref, qb2_ref, kb0_ref, kb1_ref, kb2_ref,
                   vat_ref, vb0_ref, vb1_ref, vb2_ref, ga_ref, gb_ref, stage_ref):
    x = x_ref[0]
    tm = x.shape[0]
    ms = jnp.mean(x * x, axis=-1, keepdims=True)
    h = x * lax.rsqrt(ms + EPS) * (1.0 + mod_ref[0, 1:2, :]) + mod_ref[0, 0:1, :]
    hb = h.astype(BF16)

    cos = cos_ref[0]
    sin = sin_ref[0]
    seg = seg_ref[...]
    lane = lax.broadcasted_iota(jnp.int32, cos.shape, 1)
    first_half = (lane % HEAD_DIM) < (HEAD_DIM // 2)

    def norm_rope(y2, c):
        msq = _dot((y2 * y2).astype(BF16), seg)
        yn2 = y2 * lax.rsqrt(msq + EPS) * gain_ref[:, c * 256:(c + 1) * 256]
        out = []
        for hf in range(2):
            yn = yn2[:, hf * LANES:(hf + 1) * LANES]
            partner = jnp.where(first_half, pltpu.roll(yn, 96, axis=1), pltpu.roll(yn, 32, axis=1))
            out.append(yn * cos + partner * sin)
        return out

    def store_dilated(ref, group, val):
        dil = B_PAIRS[group][1]
        if dil == 1:
            ref[0] = val.astype(BF16)
            return
        for hf in range(B_GROUP_W // LANES):
            stage_ref[hf] = val[:, hf * LANES:(hf + 1) * LANES]
        for r in range(dil):
            for hf in range(B_GROUP_W // LANES):
                rows = stage_ref[hf, pl.ds(r, tm // dil, stride=dil), :]
                col = r * B_GROUP_W + hf * LANES
                ref[0, :, col:col + LANES] = rows.astype(BF16)

    flat = [(qa_ref, 0), (qa_ref, 1), (ka_ref, 0), (ka_ref, 1)]
    grouped = [(qb0_ref, 0), (qb1_ref, 1), (qb2_ref, 2), (kb0_ref, 0), (kb1_ref, 1), (kb2_ref, 2)]
    def qk_epilogue(c, y2):
        halves = norm_rope(y2, c)
        if c < len(flat):
            ref, t = flat[c]
            for hf in range(2):
                ref[0, :, t * 256 + hf * LANES:t * 256 + (hf + 1) * LANES] = halves[hf].astype(BF16)
        else:
            ref, group = grouped[c - len(flat)]
            store_dilated(ref, group, jnp.concatenate(halves, axis=1))

    def vat_epilogue(c, v):
        vat_ref[0, c * 256:(c + 1) * 256, :] = v.astype(BF16)

    def gate_epilogue(c, g):
        ref = ga_ref if c < 4 else gb_ref
        ref[0, :, (c % 4) * 256:(c % 4 + 1) * 256] = _sigmoid(g).astype(BF16)

    jobs = []
    for c in range(len(flat) + len(grouped)):
        jobs.append((lambda c=c: _dot(hb, wqk_ref[:, c * 256:(c + 1) * 256]),
                     functools.partial(qk_epilogue, c)))
    for c in range(wvat_ref.shape[0] // 256):
        jobs.append((lambda c=c: _dot_nt(wvat_ref[c * 256:(c + 1) * 256, :], hb),
                     functools.partial(vat_epilogue, c)))
    for group, ref in enumerate((vb0_ref, vb1_ref, vb2_ref)):
        jobs.append((lambda g=group: _dot(hb, wvb_ref[:, g * 256:(g + 1) * 256]),
                     functools.partial(store_dilated, ref, group)))
    for c in range(wg_ref.shape[1] // 256):
        jobs.append((lambda c=c: _dot(hb, wg_ref[:, c * 256:(c + 1) * 256]),
                     functools.partial(gate_epilogue, c)))
    pending = None
    for matmul, epilogue in jobs:
        res = matmul()
        if pending is not None:
            pending[0](pending[1])
        pending = (epilogue, res)
    pending[0](pending[1])


def _inproj(x, mod_l, cos_l, sin_l, gain, wqk, wvat, wvb, wg, seg, tm):
    bsz, seq, d = x.shape
    a_v_w = wvat.shape[0]
    tok = lambda w: pl.BlockSpec((1, tm, w), lambda b, i: (b, i, 0))
    tok_t = pl.BlockSpec((1, a_v_w, tm), lambda b, i: (b, 0, i))
    const = lambda shape: pl.BlockSpec(shape, lambda b, i: (0,) * len(shape),
                                       pipeline_mode=pl.Buffered(1))
    row = lambda w: jax.ShapeDtypeStruct((bsz, seq, w), BF16)
    dil_specs = [pl.BlockSpec((1, tm // dl, dl * B_GROUP_W), lambda b, i: (b, i, 0)) for _, dl in B_PAIRS]
    dil_shapes = [jax.ShapeDtypeStruct((bsz, seq // dl, dl * B_GROUP_W), BF16) for _, dl in B_PAIRS]
    outs = pl.pallas_call(
        _inproj_kernel,
        grid=(bsz, seq // tm),
        in_specs=[
            tok(d),
            pl.BlockSpec((1, 6, d), lambda b, i: (b, 0, 0)),
            tok(LANES), tok(LANES),
            const(gain.shape), const(wqk.shape), const(wvat.shape), const(wvb.shape), const(wg.shape),
            const(seg.shape),
        ],
        out_specs=[tok(512), tok(512)] + dil_specs + dil_specs + [tok_t] + dil_specs + [tok(d), tok(d)],
        out_shape=[row(512), row(512)] + dil_shapes + dil_shapes
                  + [jax.ShapeDtypeStruct((bsz, a_v_w, seq), BF16)] + dil_shapes + [row(d), row(d)],
        scratch_shapes=[pltpu.VMEM((B_GROUP_W // LANES, tm, LANES), F32)],
        compiler_params=pltpu.CompilerParams(dimension_semantics=("parallel", "parallel")),
        name="in_proj",
    )(x, mod_l, cos_l, sin_l, gain, wqk, wvat, wvb, wg, seg)
    qa, ka = outs[0], outs[1]
    qb, kb, vat, vb, ga, gb = outs[2:5], outs[5:8], outs[8], outs[9:12], outs[12], outs[13]
    return qa, ka, qb, kb, vat, vb, ga, gb


SCORE_BOUND_NO_SHIFT = 64.0
KV_CHUNK = 256


def _attn_a_kernel(bound_ref, lam_ref, sub_ref, q_ref, k_ref, vt_ref, o_ref,
                   m_ref, l_ref, acc_ref, pa_ref, pb_ref, *, tk, lam_init):
    q = q_ref[0]
    tq = q.shape[0]
    seq = k_ref.shape[1]
    lane = lax.broadcasted_iota(jnp.int32, q.shape, 1)
    zero = jnp.zeros_like(q)
    q_maps = (jnp.where(lane < HEAD_DIM, q, zero), jnp.where(lane >= HEAD_DIM, q, zero))
    n_chunk = tk // KV_CHUNK

    acc_ref[...] = jnp.zeros(acc_ref.shape, F32)
    l_ref[...] = jnp.zeros(l_ref.shape, F32)

    def scores_exp(tile, p_ref):
        start = pl.multiple_of(tile * tk, tk)
        k = k_ref[0, pl.ds(start, tk), :]
        for mi in range(2):
            p = jnp.exp2(_dot_nt(k, q_maps[mi]))
            l_ref[mi] += jnp.sum(p.reshape(tk // 8, 8, tq), axis=0)
            p_ref[mi] = p.astype(BF16)

    def weighted_values(tile, p_ref):
        start = pl.multiple_of(tile * tk, tk)
        vt = vt_ref[0, :, pl.ds(start, tk)]
        for mi in range(2):
            acc_ref[mi] += _dot(vt, p_ref[mi])

    n_tiles = seq // tk

    def no_shift_pair(jj, carry):
        t = 2 * jj
        scores_exp(t + 1, pb_ref)
        weighted_values(t, pa_ref)
        scores_exp(t + 2, pa_ref)
        weighted_values(t + 1, pb_ref)
        return carry

    def no_shift_loop():
        scores_exp(0, pa_ref)
        lax.fori_loop(0, n_tiles // 2 - 1, no_shift_pair, 0)
        scores_exp(n_tiles - 1, pb_ref)
        weighted_values(n_tiles - 2, pa_ref)
        weighted_values(n_tiles - 1, pb_ref)

    def online_max_body(j, carry):
        for c in range(n_chunk):
            start = pl.multiple_of(j * tk + c * KV_CHUNK, KV_CHUNK)
            k = k_ref[0, pl.ds(start, KV_CHUNK), :]
            vt = vt_ref[0, :, pl.ds(start, KV_CHUNK)]
            for mi in range(2):
                s = _dot_nt(k, q_maps[mi])
                m_old = m_ref[mi]
                m_new = jnp.maximum(m_old, jnp.max(s, axis=0, keepdims=True))
                alpha = jnp.exp2(m_old - m_new)
                p = jnp.exp2(s - m_new[0:1])
                l_ref[mi] = alpha * l_ref[mi] + jnp.sum(p.reshape(KV_CHUNK // 8, 8, tq), axis=0)
                acc_ref[mi] = alpha[0:1] * acc_ref[mi] + _dot(vt, p.astype(BF16))
                m_ref[mi] = m_new
        return carry

    no_shift = bound_ref[0] <= SCORE_BOUND_NO_SHIFT

    @pl.when(no_shift)
    def _():
        no_shift_loop()

    @pl.when(jnp.logical_not(no_shift))
    def _():
        m_ref[...] = jnp.full(m_ref.shape, -jnp.inf, F32)
        lax.fori_loop(0, seq // tk, online_max_body, 0)

    lam_p = lam_ref[...]
    s1 = jnp.sum(lam_p[0:1] * lam_p[1:2], axis=-1, keepdims=True)
    s2 = jnp.sum(lam_p[2:3] * lam_p[3:4], axis=-1, keepdims=True)
    lam = jnp.exp(s1) - jnp.exp(s2) + lam_init
    l0 = jnp.sum(l_ref[0], axis=0, keepdims=True)
    l1 = jnp.sum(l_ref[1], axis=0, keepdims=True)
    ot = acc_ref[0] / l0 - lam * (acc_ref[1] / l1)
    o = ot.T
    msq = jnp.mean(o * o, axis=-1, keepdims=True)
    o = o * lax.rsqrt(msq + EPS) * sub_ref[...] * (1.0 - lam_init)
    o_ref[0] = o.astype(BF16)


def _attn_a(score_bound, lam_p, subln, qa, ka, vat, lam_init, tq, tk):
    bsz, seq, _ = qa.shape
    return pl.pallas_call(
        functools.partial(_attn_a_kernel, tk=tk, lam_init=lam_init),
        grid=(bsz, A_HEADS, seq // tq),
        in_specs=[
            pl.BlockSpec(memory_space=pltpu.SMEM),
            pl.BlockSpec(lam_p.shape, lambda b, h, i: (0, 0)),
            pl.BlockSpec(subln.shape, lambda b, h, i: (0, 0)),
            pl.BlockSpec((1, tq, LANES), lambda b, h, i: (b, i, h)),
            pl.BlockSpec((1, seq, LANES), lambda b, h, i: (b, 0, h)),
            pl.BlockSpec((1, LANES, seq), lambda b, h, i: (b, h, 0)),
        ],
        out_specs=pl.BlockSpec((1, tq, LANES), lambda b, h, i: (b, i, h)),
        out_shape=jax.ShapeDtypeStruct((bsz, seq, A_HEADS * LANES), BF16),
        scratch_shapes=[
            pltpu.VMEM((2, 8, tq), F32),
            pltpu.VMEM((2, 8, tq), F32),
            pltpu.VMEM((2, LANES, tq), F32),
            pltpu.VMEM((2, tk, tq), BF16),
            pltpu.VMEM((2, tk, tq), BF16),
        ],
        compiler_params=pltpu.CompilerParams(dimension_semantics=("parallel", "parallel", "parallel")),
        name="diff_attn",
    )(score_bound, lam_p, subln, qa, ka, vat)


def _attn_b_kernel(q_ref, k_ref, v_ref, o_ref, lse_ref, *, radius):
    tq = q_ref.shape[1]
    length = k_ref.shape[1]
    win = tq + 2 * radius
    t0 = pl.program_id(2) * tq
    start = pl.multiple_of(jnp.clip(t0 - radius, 0, length - win), radius)
    kw = k_ref[0, pl.ds(start, win), :]
    vw = v_ref[0, pl.ds(start, win), :]
    q = q_ref[0]

    qpos = t0 + lax.broadcasted_iota(jnp.int32, (tq, win), 0)
    kpos = start + lax.broadcasted_iota(jnp.int32, (tq, win), 1)
    valid = jnp.abs(kpos - qpos) <= radius
    lane = lax.broadcasted_iota(jnp.int32, (tq, LANES), 1)
    low = lane < HEAD_DIM

    heads = [(c, half) for c in range(B_GROUP_W // LANES) for half in range(2)]
    tile = lambda a, c: a[:, c * LANES:(c + 1) * LANES]
    scores = []
    for c, half in heads:
        qc = tile(q, c)
        qm = jnp.where(low if half == 0 else jnp.logical_not(low), qc, jnp.zeros_like(qc))
        scores.append(jnp.where(valid, _dot_nt(qm, tile(kw, c)), -jnp.inf))
    probs, sums, lses = [], [], []
    for s in scores:
        m = jnp.max(s, axis=-1, keepdims=True)
        p = jnp.exp2(s - m)
        l = jnp.sum(p, axis=-1, keepdims=True)
        probs.append(p.astype(BF16))
        sums.append(l)
        lses.append((m + jnp.log2(l)) * LN2)
    outs = [_dot(probs[i], tile(vw, c)) / sums[i] for i, (c, _) in enumerate(heads)]
    for c in range(B_GROUP_W // LANES):
        o_ref[0, :, c * LANES:(c + 1) * LANES] = jnp.where(low, outs[2 * c], outs[2 * c + 1]).astype(BF16)
        lse_ref[0, :, c * LANES:(c + 1) * LANES] = jnp.where(low, lses[2 * c], lses[2 * c + 1])


def _attn_b(qg, kg, vg, group, tq):
    window, dilation = B_PAIRS[group]
    radius = window // (2 * dilation)
    bsz, length, _ = qg.shape
    tq = min(tq, length - 2 * radius)
    return pl.pallas_call(
        functools.partial(_attn_b_kernel, radius=radius),
        grid=(bsz, dilation, length // tq),
        in_specs=[
            pl.BlockSpec((1, tq, B_GROUP_W), lambda b, r, i: (b, i, r)),
            pl.BlockSpec((1, length, B_GROUP_W), lambda b, r, i: (b, 0, r)),
            pl.BlockSpec((1, length, B_GROUP_W), lambda b, r, i: (b, 0, r)),
        ],
        out_specs=[pl.BlockSpec((1, tq, B_GROUP_W), lambda b, r, i: (b, i, r))] * 2,
        out_shape=[jax.ShapeDtypeStruct((bsz, length, dilation * B_GROUP_W), BF16),
                   jax.ShapeDtypeStruct((bsz, length, dilation * B_GROUP_W), F32)],
        compiler_params=pltpu.CompilerParams(dimension_semantics=("parallel", "parallel", "parallel")),
        name=f"band_attn_g{group}",
    )(qg, kg, vg)


def _merge_kernel(x_ref, mod_ref, oa_ref, ob0_ref, ls0_ref, ob1_ref, ls1_ref, ob2_ref, ls2_ref,
                  ga_ref, gb_ref, wpa_ref, wpb_ref, wo_ref, wr2_ref, br_ref,
                  x1_ref, h2_ref, route_ref, so1_ref, sl1_ref, so2_ref, sl2_ref):
    tm = x_ref.shape[1]

    def token_major(o_ref, l_ref, group, so_ref, sl_ref):
        dil = B_PAIRS[group][1]
        n_hf = B_GROUP_W // LANES
        for r in range(dil):
            for hf in range(n_hf):
                cols = slice(r * B_GROUP_W + hf * LANES, r * B_GROUP_W + (hf + 1) * LANES)
                so_ref[hf, pl.ds(r, tm // dil, stride=dil), :] = o_ref[0, :, cols].astype(F32)
                sl_ref[hf, pl.ds(r, tm // dil, stride=dil), :] = l_ref[0, :, cols]
        return (jnp.concatenate([so_ref[hf] for hf in range(n_hf)], axis=1),
                jnp.concatenate([sl_ref[hf] for hf in range(n_hf)], axis=1))

    o0, ls0 = ob0_ref[0].astype(F32), ls0_ref[0]
    o1, ls1 = token_major(ob1_ref, ls1_ref, 1, so1_ref, sl1_ref)
    o2, ls2 = token_major(ob2_ref, ls2_ref, 2, so2_ref, sl2_ref)
    mx = jnp.maximum(jnp.maximum(ls0, ls1), ls2)
    e0, e1, e2 = jnp.exp(ls0 - mx), jnp.exp(ls1 - mx), jnp.exp(ls2 - mx)
    ob = ((e0 * o0 + e1 * o1 + e2 * o2) / (e0 + e1 + e2)).astype(BF16)

    n_part = 2
    rows = [slice(p * (tm // n_part), (p + 1) * (tm // n_part)) for p in range(n_part)]
    pa = [_dot(oa_ref[0, r, :], wpa_ref[...]) for r in rows]
    pb = [_dot(ob[r], wpb_ref[...]) for r in rows]
    merged = [(ga_ref[0, r, :].astype(F32) * pa[p] + gb_ref[0, r, :].astype(F32) * pb[p]).astype(BF16)
              for p, r in enumerate(rows)]
    y = [_dot(merged[p], wo_ref[...]) for p in range(n_part)]
    for p, r in enumerate(rows):
        x1 = x_ref[0, r, :] + mod_ref[0, 2:3, :] * y[p]
        x1_ref[0, r, :] = x1
        ms = jnp.mean(x1 * x1, axis=-1, keepdims=True)
        h2 = x1 * lax.rsqrt(ms + EPS) * (1.0 + mod_ref[0, 4:5, :]) + mod_ref[0, 3:4, :]
        h2_hi = h2.astype(BF16)
        h2_ref[0, r, :] = h2_hi
        h2_lo = (h2 - h2_hi.astype(F32)).astype(BF16)
        both = _dot(h2_hi, wr2_ref[...])
        logits = (both[:, :ROUTER_W] + both[:, ROUTER_W:] + _dot(h2_lo, wr2_ref[:, :ROUTER_W])) + br_ref[...]
        route_ref[0, r, :] = _route(logits)


def _route(logits):
    lane = lax.broadcasted_iota(jnp.int32, logits.shape, 1)
    neg = -jnp.inf
    big = ROUTER_W
    is_grp = (lane >= N_EXPERTS) & (lane < N_EXPERTS + N_GROUPS)
    lg = jnp.where(is_grp, logits, neg)
    mg = jnp.max(lg, axis=-1, keepdims=True)
    g_lane = jnp.min(jnp.where(lg == mg, lane, big), axis=-1, keepdims=True)
    g_val = 1.0 / jnp.sum(jnp.exp(lg - mg), axis=-1, keepdims=True)
    lo = (g_lane - N_EXPERTS) * EXPERTS_PER_GROUP
    in_grp = (lane >= lo) & (lane < lo + EXPERTS_PER_GROUP)
    le = jnp.where(in_grp, logits, neg)
    m1 = jnp.max(le, axis=-1, keepdims=True)
    i1 = jnp.min(jnp.where(le == m1, lane, big), axis=-1, keepdims=True)
    le2 = jnp.where(lane == i1, neg, le)
    m2 = jnp.max(le2, axis=-1, keepdims=True)
    i2 = jnp.min(jnp.where(le2 == m2, lane, big), axis=-1, keepdims=True)
    e = jnp.exp(m2 - m1)
    w1 = g_val / (1.0 + e)
    w2 = g_val * e / (1.0 + e)
    return jnp.where(lane == 0, i1.astype(F32),
                     jnp.where(lane == 1, i2.astype(F32),
                               jnp.where(lane == 2, w1, jnp.where(lane == 3, w2, 0.0))))


def _merge(x, mod_l, oa, obs, ga, gb, wpa, wpb, wo, wr, br, tm):
    bsz, seq, d = x.shape
    tok = lambda w: pl.BlockSpec((1, tm, w), lambda b, i: (b, i, 0))
    const = lambda shape: pl.BlockSpec(shape, lambda b, i: (0,) * len(shape),
                                       pipeline_mode=pl.Buffered(1))
    ob_args, ob_specs = [], []
    for (o, lse), (_, dl) in zip(obs, B_PAIRS):
        ob_args += [o, lse]
        ob_specs += [pl.BlockSpec((1, tm // dl, dl * B_GROUP_W), lambda b, i: (b, i, 0))] * 2
    return pl.pallas_call(
        _merge_kernel,
        grid=(bsz, seq // tm),
        in_specs=[tok(d), pl.BlockSpec((1, 6, d), lambda b, i: (b, 0, 0)), tok(oa.shape[-1])]
                 + ob_specs + [tok(d), tok(d)]
                 + [const(w.shape) for w in (wpa, wpb, wo, wr, br)],
        out_specs=[tok(d), tok(d), tok(ROUTER_W)],
        out_shape=[jax.ShapeDtypeStruct((bsz, seq, d), F32),
                   jax.ShapeDtypeStruct((bsz, seq, d), BF16),
                   jax.ShapeDtypeStruct((bsz, seq, ROUTER_W), F32)],
        scratch_shapes=[pltpu.VMEM((B_GROUP_W // LANES, tm, LANES), F32)] * 4,
        compiler_params=pltpu.CompilerParams(dimension_semantics=("parallel", "parallel")),
        name="merge_proj",
    )(x, mod_l, oa, *ob_args, ga, gb, wpa, wpb, wo, wr, br)


MOE_BLOCK = 1024
SEG_ALIGN = 16
ROW_CHUNK = 128
PERM_CHUNK = 256
EXPERTS_PER_STEP = 4
SORTED_ROWS = -(-(2 * MOE_BLOCK + N_EXPERTS * (SEG_ALIGN - 1)) // PERM_CHUNK) * PERM_CHUNK
SORTED_ROWS_ALLOC = SORTED_ROWS + ROW_CHUNK


def _plan_kernel(route_ref, posc_ref, posr_ref, base_ref, npad_ref):
    r = route_ref[0]
    tb = r.shape[0]
    lane = lax.broadcasted_iota(jnp.int32, r.shape, 1).astype(F32)
    oh1 = jnp.where(lane == r[:, 0:1], 1.0, 0.0)
    oh2 = jnp.where(lane == r[:, 1:2], 1.0, 0.0)
    cnt1 = jnp.sum(oh1, axis=0, keepdims=True)
    cnt2 = jnp.sum(oh2, axis=0, keepdims=True)
    npad = jnp.floor((cnt1 + cnt2 + (SEG_ALIGN - 1)) * (1.0 / SEG_ALIGN)) * SEG_ALIGN
    ri = lax.broadcasted_iota(jnp.int32, (LANES, LANES), 0)
    ci = lax.broadcasted_iota(jnp.int32, (LANES, LANES), 1)
    upper = jnp.where(ri < ci, 1.0, 0.0)
    npad8 = jnp.broadcast_to(npad, (8, LANES))
    base = jnp.dot(npad8, upper, precision=HIGHEST, preferred_element_type=F32)[0:1]

    ti = lax.broadcasted_iota(jnp.int32, (tb, tb), 0)
    tj = lax.broadcasted_iota(jnp.int32, (tb, tb), 1)
    before = jnp.where(tj < ti, 1.0, 0.0).astype(BF16)
    pre1 = _dot(before, oh1.astype(BF16))
    pre2 = _dot(before, oh2.astype(BF16))
    pos1 = jnp.sum(oh1 * (base + pre1), axis=-1, keepdims=True)
    pos2 = jnp.sum(oh2 * (base + cnt1 + pre2), axis=-1, keepdims=True)
    packed = jnp.where(lane == 0.0, pos1, jnp.where(lane == 1.0, pos2, jnp.where(lane >= 2.0, r, 0.0)))
    posc_ref[0] = packed
    posr_ref[0] = packed.T[0:8, :]
    base_ref[0] = base
    npad_ref[0] = npad


def _moe_plan(route):
    n_blk, tb, _ = route.shape
    vec = pl.BlockSpec((1, 1, LANES), lambda i: (i, 0, 0))
    return pl.pallas_call(
        _plan_kernel,
        grid=(n_blk,),
        in_specs=[pl.BlockSpec((1, tb, ROUTER_W), lambda i: (i, 0, 0))],
        out_specs=[pl.BlockSpec((1, tb, LANES), lambda i: (i, 0, 0)),
                   pl.BlockSpec((1, 8, tb), lambda i: (i, 0, 0)), vec, vec],
        out_shape=[jax.ShapeDtypeStruct((n_blk, tb, LANES), F32),
                   jax.ShapeDtypeStruct((n_blk, 8, tb), F32),
                   jax.ShapeDtypeStruct((n_blk, 1, LANES), F32),
                   jax.ShapeDtypeStruct((n_blk, 1, LANES), F32)],
        compiler_params=pltpu.CompilerParams(dimension_semantics=("parallel",)),
        name="moe_plan",
    )(route)


def _moe_kernel(base_ref, npad_ref, x1_ref, mod_ref, h_ref, posc_ref, posr_ref, wg_ref, wu_ref, wd_ref,
                o_ref, xs_ref, ys_ref):
    blk = pl.program_id(0)
    e = pl.program_id(1)
    tb = h_ref.shape[1]

    @pl.when(e == 0)
    def _():
        pr = posr_ref[0]
        h = h_ref[0]
        for c in range(SORTED_ROWS // PERM_CHUNK):
            rid = (c * PERM_CHUNK + lax.broadcasted_iota(jnp.int32, (PERM_CHUNK, tb), 0)).astype(F32)
            sel = jnp.where(rid == pr[0:1], 1.0, jnp.where(rid == pr[1:2], 1.0, 0.0)).astype(BF16)
            xs_ref[c * PERM_CHUNK:(c + 1) * PERM_CHUNK, :] = _dot(sel, h).astype(BF16)
        xs_ref[SORTED_ROWS:, :] = jnp.zeros((ROW_CHUNK, xs_ref.shape[1]), BF16)
        ys_ref[...] = jnp.zeros(ys_ref.shape, BF16)

    def gate_up(r0, j):
        xc = xs_ref[pl.ds(r0, ROW_CHUNK), :]
        return _dot(xc, wg_ref[j]), _dot(xc, wu_ref[j])

    def down(r0, j, a, u):
        hid = (a * _sigmoid(a)) * u
        ys_ref[pl.ds(r0, ROW_CHUNK), :] = _dot(hid.astype(BF16), wd_ref[j]).astype(BF16)

    n_here = wg_ref.shape[0]
    starts = [pl.multiple_of(base_ref[blk, e * n_here + j], SEG_ALIGN) for j in range(n_here)]
    first = [gate_up(starts[j], j) for j in range(n_here)]
    for j in range(n_here):
        down(starts[j], j, *first[j])
    for j in range(n_here):
        seg_rows = npad_ref[blk, e * n_here + j]

        def more(c, carry, j=j, seg_rows=seg_rows):
            r0 = starts[j] + jnp.minimum(c * ROW_CHUNK, seg_rows - ROW_CHUNK)
            r0 = pl.multiple_of(r0, SEG_ALIGN)
            down(r0, j, *gate_up(r0, j))
            return carry

        lax.fori_loop(1, (seg_rows + ROW_CHUNK - 1) // ROW_CHUNK, more, 0)

    @pl.when(e == pl.num_programs(1) - 1)
    def _():
        ys = ys_ref[...]
        for c in range(tb // PERM_CHUNK):
            pc = posc_ref[0, c * PERM_CHUNK:(c + 1) * PERM_CHUNK, :]
            rid = lax.broadcasted_iota(jnp.int32, (PERM_CHUNK, SORTED_ROWS_ALLOC), 1).astype(F32)
            wsel = (jnp.where(rid == pc[:, 0:1], pc[:, 2:3], 0.0)
                    + jnp.where(rid == pc[:, 1:2], pc[:, 3:4], 0.0)).astype(BF16)
            y = _dot(wsel, ys)
            rows = slice(c * PERM_CHUNK, (c + 1) * PERM_CHUNK)
            o_ref[0, rows, :] = x1_ref[0, rows, :] + mod_ref[0, 5:6, :] * y


def _moe(x1, mod_l, h2, route, weg, weu, wed):
    bsz, seq, d = x1.shape
    n_e, _, d_e = weg.shape
    tb = min(MOE_BLOCK, seq)
    assert tb == MOE_BLOCK, "sorted-row capacity is sized for MOE_BLOCK tokens"
    per_batch = seq // tb
    n_blk = bsz * per_batch
    blocked = lambda a: a.reshape(n_blk, tb, a.shape[-1])
    posc, posr, base, npad = _moe_plan(blocked(route))
    base_i = base.reshape(n_blk, LANES).astype(jnp.int32)
    npad_i = npad.reshape(n_blk, LANES).astype(jnp.int32)

    tok = lambda w: pl.BlockSpec((1, tb, w), lambda i, e, *_: (i, 0, 0))
    grid_spec = pltpu.PrefetchScalarGridSpec(
        num_scalar_prefetch=2,
        grid=(n_blk, n_e // EXPERTS_PER_STEP),
        in_specs=[tok(d),
                  pl.BlockSpec((1, 6, d), lambda i, e, *_: (i // per_batch, 0, 0)),
                  tok(d), tok(LANES),
                  pl.BlockSpec((1, 8, tb), lambda i, e, *_: (i, 0, 0)),
                  pl.BlockSpec((EXPERTS_PER_STEP, d, d_e), lambda i, e, *_: (e, 0, 0)),
                  pl.BlockSpec((EXPERTS_PER_STEP, d, d_e), lambda i, e, *_: (e, 0, 0)),
                  pl.BlockSpec((EXPERTS_PER_STEP, d_e, d), lambda i, e, *_: (e, 0, 0))],
        out_specs=tok(d),
        scratch_shapes=[pltpu.VMEM((SORTED_ROWS_ALLOC, d), BF16),
                        pltpu.VMEM((SORTED_ROWS_ALLOC, d), BF16)],
    )
    out = pl.pallas_call(
        _moe_kernel,
        grid_spec=grid_spec,
        out_shape=jax.ShapeDtypeStruct((n_blk, tb, d), F32),
        compiler_params=pltpu.CompilerParams(dimension_semantics=("parallel", "arbitrary")),
        name="moe_experts",
    )(base_i, npad_i, blocked(x1), mod_l, blocked(h2), posc, posr, weg, weu, wed)
    return out.reshape(bsz, seq, d)


def _tiles(seq):
    return dict(
        tm_proj=min(512, seq),
        tq_a=min(512, seq),
        tk_a=min(1024, seq),
        tq_b=256,
        tm_merge=min(512, seq),
    )


def kernel(x, c, positions, w_ada, b_ada, w_in, qn_a, kn_a, lam_q1, lam_k1, lam_q2, lam_k2,
           subln_a, qn_b, kn_b, w_pa, w_pb, w_o, w_r1, b_r1, w_r2, b_r2,
           w_e_gate, w_e_up, w_e_down):
    depth = w_ada.shape[0]
    bsz, seq, d = x.shape
    t = _tiles(seq)

    mod = _ada(c, w_ada, b_ada).reshape(depth, bsz, 6, d)
    cos_l, sin_l = _rope_tables(positions)
    seg = jnp.kron(jnp.eye(256 // HEAD_DIM, dtype=F32),
                   jnp.full((HEAD_DIM, HEAD_DIM), 1.0 / HEAD_DIM, F32)).astype(BF16)
    q_scale = HEAD_DIM ** -0.5 * LOG2E

    for layer in range(depth):
        lam_init = 0.8 - 0.6 * math.exp(-0.3 * layer)
        w = w_in[layer]
        wqk = jnp.concatenate([w[:, 0:1024], w[:, 1536:3072]], axis=1).astype(BF16)
        wvat = w[:, 1024:1536].T.astype(BF16)
        wvb = w[:, 3072:3840].astype(BF16)
        wg = w[:, 3840:].astype(BF16)
        gain = jnp.concatenate([
            jnp.tile(qn_a[layer] * q_scale, 8), jnp.tile(kn_a[layer], 8),
            jnp.tile(qn_b[layer] * q_scale, 12), jnp.tile(kn_b[layer], 12)]).reshape(1, -1)

        qa, ka, qb, kb, vat, vb, ga, gb = _inproj(x, mod[layer], cos_l, sin_l, gain, wqk, wvat, wvb, wg, seg,
                                                  t["tm_proj"])

        score_bound = (1.01 * HEAD_DIM * q_scale * jnp.max(jnp.abs(qn_a[layer]))
                       * jnp.max(jnp.abs(kn_a[layer]))).reshape(1)
        lam_p = jnp.stack([lam_q1[layer], lam_k1[layer], lam_q2[layer], lam_k2[layer]])
        oa = _attn_a(score_bound, lam_p, subln_a[layer].reshape(1, -1), qa, ka, vat, lam_init,
                     t["tq_a"], t["tk_a"])
        obs = [_attn_b(qb[g], kb[g], vb[g], g, t["tq_b"]) for g in range(B_GROUPS)]

        wr = jnp.zeros((d, ROUTER_W), F32)
        wr = wr.at[:, :N_EXPERTS].set(w_r2[layer]).at[:, N_EXPERTS:N_EXPERTS + N_GROUPS].set(w_r1[layer])
        br = jnp.zeros((1, ROUTER_W), F32)
        br = br.at[0, :N_EXPERTS].set(b_r2[layer]).at[0, N_EXPERTS:N_EXPERTS + N_GROUPS].set(b_r1[layer])
        wr_hi = wr.astype(BF16)
        wr2 = jnp.concatenate([wr_hi, (wr - wr_hi.astype(F32)).astype(BF16)], axis=1)
        x1, h2, route = _merge(x, mod[layer], oa, obs, ga, gb,
                               w_pa[layer].astype(BF16), w_pb[layer].astype(BF16), w_o[layer].astype(BF16),
                               wr2, br, t["tm_merge"])

        x = _moe(x1, mod[layer], h2, route,
                 w_e_gate[layer].astype(BF16), w_e_up[layer].astype(BF16), w_e_down[layer].astype(BF16))
    return x
```

```python
import functools
import math

import jax
import jax.numpy as jnp
from jax import lax
from jax.experimental import pallas as pl
from jax.experimental.pallas import tpu as pltpu

EPS = 1e-6
ROPE_THETA = 10000.0
LOG2E = math.log2(math.e)
LN2 = math.log(2.0)

A_HEADS = 4
HEAD_DIM = 64
LANES = 128
B_PAIRS = ((128, 1), (512, 4), (2048, 16))
B_GROUPS = len(B_PAIRS)
B_GROUP_W = 256
N_GROUPS = 4
EXPERTS_PER_GROUP = 8
N_EXPERTS = N_GROUPS * EXPERTS_PER_GROUP
ROUTER_W = 128

F32 = jnp.float32
BF16 = jnp.bfloat16
HIGHEST = lax.Precision.HIGHEST


def _dot(a, b):
    return jnp.dot(a, b, preferred_element_type=F32)


def _dot_nt(a, b):
    return lax.dot_general(a, b, (((1,), (1,)), ((), ())), preferred_element_type=F32)


def _sigmoid(x):
    return 1.0 / (1.0 + jnp.exp(-x))


def _ada_kernel(c_ref, w_ref, b_ref, o_ref):
    c = c_ref[...]
    c_act = c * _sigmoid(c)
    o_ref[0] = jnp.dot(c_act, w_ref[0], precision=HIGHEST, preferred_element_type=F32) + b_ref[0]


def _ada(c, w_ada, b_ada):
    depth, d, six_d = w_ada.shape
    bsz = c.shape[0]
    n_col = six_d // d
    return pl.pallas_call(
        _ada_kernel,
        grid=(depth, n_col),
        in_specs=[
            pl.BlockSpec((bsz, d), lambda l, j: (0, 0)),
            pl.BlockSpec((1, d, d), lambda l, j: (l, 0, j)),
            pl.BlockSpec((1, 1, d), lambda l, j: (l, 0, j)),
        ],
        out_specs=pl.BlockSpec((1, bsz, d), lambda l, j: (l, 0, j)),
        out_shape=jax.ShapeDtypeStruct((depth, bsz, six_d), F32),
        name="ada_mod",
    )(c, w_ada, b_ada.reshape(depth, 1, six_d))


def _rope_kernel(pos_ref, f_ref, cos_ref, sin_ref):
    ang = pos_ref[0].astype(F32) * f_ref[...]
    cos_ref[0] = jnp.cos(ang)
    sin_ref[0] = jnp.sin(ang)


def _rope_tables(positions):
    bsz, seq = positions.shape
    half = HEAD_DIM // 2
    inv_freq = ROPE_THETA ** (-jnp.arange(0, HEAD_DIM, 2, dtype=F32) / HEAD_DIM)
    cos_t, sin_t = pl.pallas_call(
        _rope_kernel,
        grid=(bsz,),
        in_specs=[
            pl.BlockSpec((1, 1, seq), lambda b: (b, 0, 0)),
            pl.BlockSpec((half, 1), lambda b: (0, 0)),
        ],
        out_specs=[pl.BlockSpec((1, half, seq), lambda b: (b, 0, 0))] * 2,
        out_shape=[jax.ShapeDtypeStruct((bsz, half, seq), F32)] * 2,
        name="rope_tables",
    )(positions.reshape(bsz, 1, seq), inv_freq.reshape(half, 1))
    cos = cos_t.transpose(0, 2, 1)
    sin = sin_t.transpose(0, 2, 1)
    cos_l = jnp.concatenate([cos, cos, cos, cos], axis=-1)
    sin_l = jnp.concatenate([-sin, sin, -sin, sin], axis=-1)
    return cos_l, sin_l


def _inproj_kernel(x_ref, mod_ref, cos_ref, sin_ref, gain_ref, wqk_ref, wvat_ref, wvb_ref, wg_ref, seg_ref,
                   qa_ref, ka_ref, qb0_ref, qb1_ref, qb2_ref, kb0_ref, kb1_ref, kb2_ref,
                   vat_ref, vb0_ref, vb1_ref, vb2_ref, ga_ref, gb_ref, stage_ref):
    x = x_ref[0]
    tm = x.shape[0]
    ms = jnp.mean(x * x, axis=-1, keepdims=True)
    h = x * lax.rsqrt(ms + EPS) * (1.0 + mod_ref[0, 1:2, :]) + mod_ref[0, 0:1, :]
    hb = h.astype(BF16)

    cos = cos_ref[0]
    sin = sin_ref[0]
    seg = seg_ref[...]
    lane = lax.broadcasted_iota(jnp.int32, cos.shape, 1)
    first_half = (lane % HEAD_DIM) < (HEAD_DIM // 2)

    def norm_rope(y2, c):
        msq = _dot((y2 * y2).astype(BF16), seg)
        yn2 = y2 * lax.rsqrt(msq + EPS) * gain_ref[:, c * 256:(c + 1) * 256]
        out = []
        for hf in range(2):
            yn = yn2[:, hf * LANES:(hf + 1) * LANES]
            partner = jnp.where(first_half, pltpu.roll(yn, 96, axis=1), pltpu.roll(yn, 32, axis=1))
            out.append(yn * cos + partner * sin)
        return out

    def store_dilated(ref, group, val):
        dil = B_PAIRS[group][1]
        if dil == 1:
            ref[0] = val.astype(BF16)
            return
        for hf in range(B_GROUP_W // LANES):
            stage_ref[hf] = val[:, hf * LANES:(hf + 1) * LANES]
        for r in range(dil):
            for hf in range(B_GROUP_W // LANES):
                rows = stage_ref[hf, pl.ds(r, tm // dil, stride=dil), :]
                col = r * B_GROUP_W + hf * LANES
                ref[0, :, col:col + LANES] = rows.astype(BF16)

    flat = [(qa_ref, 0), (qa_ref, 1), (ka_ref, 0), (ka_ref, 1)]
    grouped = [(qb0_ref, 0), (qb1_ref, 1), (qb2_ref, 2), (kb0_ref, 0), (kb1_ref, 1), (kb2_ref, 2)]
    def qk_epilogue(c, y2):
        halves = norm_rope(y2, c)
        if c < len(flat):
            ref, t = flat[c]
            for hf in range(2):
                ref[0, :, t * 256 + hf * LANES:t * 256 + (hf + 1) * LANES] = halves[hf].astype(BF16)
        else:
            ref, group = grouped[c - len(flat)]
            store_dilated(ref, group, jnp.concatenate(halves, axis=1))

    def vat_epilogue(c, v):
        vat_ref[0, c * 256:(c + 1) * 256, :] = v.astype(BF16)

    def gate_epilogue(c, g):
        ref = ga_ref if c < 4 else gb_ref
        ref[0, :, (c % 4) * 256:(c % 4 + 1) * 256] = _sigmoid(g).astype(BF16)

    jobs = []
    for c in range(len(flat) + len(grouped)):
        jobs.append((lambda c=c: _dot(hb, wqk_ref[:, c * 256:(c + 1) * 256]),
                     functools.partial(qk_epilogue, c)))
    for c in range(wvat_ref.shape[0] // 256):
        jobs.append((lambda c=c: _dot_nt(wvat_ref[c * 256:(c + 1) * 256, :], hb),
                     functools.partial(vat_epilogue, c)))
    for group, ref in enumerate((vb0_ref, vb1_ref, vb2_ref)):
        jobs.append((lambda g=group: _dot(hb, wvb_ref[:, g * 256:(g + 1) * 256]),
                     functools.partial(store_dilated, ref, group)))
    for c in range(wg_ref.shape[1] // 256):
        jobs.append((lambda c=c: _dot(hb, wg_ref[:, c * 256:(c + 1) * 256]),
                     functools.partial(gate_epilogue, c)))
    pending = None
    for matmul, epilogue in jobs:
        res = matmul()
        if pending is not None:
            pending[0](pending[1])
        pending = (epilogue, res)
    pending[0](pending[1])


def _inproj(x, mod_l, cos_l, sin_l, gain, wqk, wvat, wvb, wg, seg, tm):
    bsz, seq, d = x.shape
    a_v_w = wvat.shape[0]
    tok = lambda w: pl.BlockSpec((1, tm, w), lambda b, i: (b, i, 0))
    tok_t = pl.BlockSpec((1, a_v_w, tm), lambda b, i: (b, 0, i))
    const = lambda shape: pl.BlockSpec(shape, lambda b, i: (0,) * len(shape),
                                       pipeline_mode=pl.Buffered(1))
    row = lambda w: jax.ShapeDtypeStruct((bsz, seq, w), BF16)
    dil_specs = [pl.BlockSpec((1, tm // dl, dl * B_GROUP_W), lambda b, i: (b, i, 0)) for _, dl in B_PAIRS]
    dil_shapes = [jax.ShapeDtypeStruct((bsz, seq // dl, dl * B_GROUP_W), BF16) for _, dl in B_PAIRS]
    outs = pl.pallas_call(
        _inproj_kernel,
        grid=(bsz, seq // tm),
        in_specs=[
            tok(d),
            pl.BlockSpec((1, 6, d), lambda b, i: (b, 0, 0)),
            tok(LANES), tok(LANES),
            const(gain.shape), const(wqk.shape), const(wvat.shape), const(wvb.shape), const(wg.shape),
            const(seg.shape),
        ],
        out_specs=[tok(512), tok(512)] + dil_specs + dil_specs + [tok_t] + dil_specs + [tok(d), tok(d)],
        out_shape=[row(512), row(512)] + dil_shapes + dil_shapes
                  + [jax.ShapeDtypeStruct((bsz, a_v_w, seq), BF16)] + dil_shapes + [row(d), row(d)],
        scratch_shapes=[pltpu.VMEM((B_GROUP_W // LANES, tm, LANES), F32)],
        compiler_params=pltpu.CompilerParams(dimension_semantics=("parallel", "parallel")),
        name="in_proj",
    )(x, mod_l, cos_l, sin_l, gain, wqk, wvat, wvb, wg, seg)
    qa, ka = outs[0], outs[1]
    qb, kb, vat, vb, ga, gb = outs[2:5], outs[5:8], outs[8], outs[9:12], outs[12], outs[13]
    return qa, ka, qb, kb, vat, vb, ga, gb


SCORE_BOUND_NO_SHIFT = 64.0
KV_CHUNK = 256


def _attn_a_kernel(bound_ref, lam_ref, sub_ref, q_ref, k_ref, vt_ref, o_ref,
                   m_ref, l_ref, acc_ref, pa_ref, pb_ref, *, tk, lam_init):
    q = q_ref[0]
    tq = q.shape[0]
    seq = k_ref.shape[1]
    lane = lax.broadcasted_iota(jnp.int32, q.shape, 1)
    zero = jnp.zeros_like(q)
    q_maps = (jnp.where(lane < HEAD_DIM, q, zero), jnp.where(lane >= HEAD_DIM, q, zero))
    n_chunk = tk // KV_CHUNK

    acc_ref[...] = jnp.zeros(acc_ref.shape, F32)
    l_ref[...] = jnp.zeros(l_ref.shape, F32)

    def scores_exp(tile, p_ref):
        start = pl.multiple_of(tile * tk, tk)
        k = k_ref[0, pl.ds(start, tk), :]
        for mi in range(2):
            p = jnp.exp2(_dot_nt(k, q_maps[mi]))
            l_ref[mi] += jnp.sum(p.reshape(tk // 8, 8, tq), axis=0)
            p_ref[mi] = p.astype(BF16)

    def weighted_values(tile, p_ref):
        start = pl.multiple_of(tile * tk, tk)
        vt = vt_ref[0, :, pl.ds(start, tk)]
        for mi in range(2):
            acc_ref[mi] += _dot(vt, p_ref[mi])

    n_tiles = seq // tk

    def no_shift_pair(jj, carry):
        t = 2 * jj
        scores_exp(t + 1, pb_ref)
        weighted_values(t, pa_ref)
        scores_exp(t + 2, pa_ref)
        weighted_values(t + 1, pb_ref)
        return carry

    def no_shift_loop():
        scores_exp(0, pa_ref)
        lax.fori_loop(0, n_tiles // 2 - 1, no_shift_pair, 0)
        scores_exp(n_tiles - 1, pb_ref)
        weighted_values(n_tiles - 2, pa_ref)
        weighted_values(n_tiles - 1, pb_ref)

    def online_max_body(j, carry):
        for c in range(n_chunk):
            start = pl.multiple_of(j * tk + c * KV_CHUNK, KV_CHUNK)
            k = k_ref[0, pl.ds(start, KV_CHUNK), :]
            vt = vt_ref[0, :, pl.ds(start, KV_CHUNK)]
            for mi in range(2):
                s = _dot_nt(k, q_maps[mi])
                m_old = m_ref[mi]
                m_new = jnp.maximum(m_old, jnp.max(s, axis=0, keepdims=True))
                alpha = jnp.exp2(m_old - m_new)
                p = jnp.exp2(s - m_new[0:1])
                l_ref[mi] = alpha * l_ref[mi] + jnp.sum(p.reshape(KV_CHUNK // 8, 8, tq), axis=0)
                acc_ref[mi] = alpha[0:1] * acc_ref[mi] + _dot(vt, p.astype(BF16))
                m_ref[mi] = m_new
        return carry

    no_shift = bound_ref[0] <= SCORE_BOUND_NO_SHIFT

    @pl.when(no_shift)
    def _():
        no_shift_loop()

    @pl.when(jnp.logical_not(no_shift))
    def _():
        m_ref[...] = jnp.full(m_ref.shape, -jnp.inf, F32)
        lax.fori_loop(0, seq // tk, online_max_body, 0)

    lam_p = lam_ref[...]
    s1 = jnp.sum(lam_p[0:1] * lam_p[1:2], axis=-1, keepdims=True)
    s2 = jnp.sum(lam_p[2:3] * lam_p[3:4], axis=-1, keepdims=True)
    lam = jnp.exp(s1) - jnp.exp(s2) + lam_init
    l0 = jnp.sum(l_ref[0], axis=0, keepdims=True)
    l1 = jnp.sum(l_ref[1], axis=0, keepdims=True)
    ot = acc_ref[0] / l0 - lam * (acc_ref[1] / l1)
    o = ot.T
    msq = jnp.mean(o * o, axis=-1, keepdims=True)
    o = o * lax.rsqrt(msq + EPS) * sub_ref[...] * (1.0 - lam_init)
    o_ref[0] = o.astype(BF16)


def _attn_a(score_bound, lam_p, subln, qa, ka, vat, lam_init, tq, tk):
    bsz, seq, _ = qa.shape
    return pl.pallas_call(
        functools.partial(_attn_a_kernel, tk=tk, lam_init=lam_init),
        grid=(bsz, A_HEADS, seq // tq),
        in_specs=[
            pl.BlockSpec(memory_space=pltpu.SMEM),
            pl.BlockSpec(lam_p.shape, lambda b, h, i: (0, 0)),
            pl.BlockSpec(subln.shape, lambda b, h, i: (0, 0)),
            pl.BlockSpec((1, tq, LANES), lambda b, h, i: (b, i, h)),
            pl.BlockSpec((1, seq, LANES), lambda b, h, i: (b, 0, h)),
            pl.BlockSpec((1, LANES, seq), lambda b, h, i: (b, h, 0)),
        ],
        out_specs=pl.BlockSpec((1, tq, LANES), lambda b, h, i: (b, i, h)),
        out_shape=jax.ShapeDtypeStruct((bsz, seq, A_HEADS * LANES), BF16),
        scratch_shapes=[
            pltpu.VMEM((2, 8, tq), F32),
            pltpu.VMEM((2, 8, tq), F32),
            pltpu.VMEM((2, LANES, tq), F32),
            pltpu.VMEM((2, tk, tq), BF16),
            pltpu.VMEM((2, tk, tq), BF16),
        ],
        compiler_params=pltpu.CompilerParams(dimension_semantics=("parallel", "parallel", "parallel")),
        name="diff_attn",
    )(score_bound, lam_p, subln, qa, ka, vat)


def _attn_b_kernel(q_ref, k_ref, v_ref, o_ref, lse_ref, *, radius):
    tq = q_ref.shape[1]
    length = k_ref.shape[1]
    win = tq + 2 * radius
    t0 = pl.program_id(2) * tq
    start = pl.multiple_of(jnp.clip(t0 - radius, 0, length - win), radius)
    kw = k_ref[0, pl.ds(start, win), :]
    vw = v_ref[0, pl.ds(start, win), :]
    q = q_ref[0]

    qpos = t0 + lax.broadcasted_iota(jnp.int32, (tq, win), 0)
    kpos = start + lax.broadcasted_iota(jnp.int32, (tq, win), 1)
    valid = jnp.abs(kpos - qpos) <= radius
    lane = lax.broadcasted_iota(jnp.int32, (tq, LANES), 1)
    low = lane < HEAD_DIM

    heads = [(c, half) for c in range(B_GROUP_W // LANES) for half in range(2)]
    tile = lambda a, c: a[:, c * LANES:(c + 1) * LANES]
    scores = []
    for c, half in heads:
        qc = tile(q, c)
        qm = jnp.where(low if half == 0 else jnp.logical_not(low), qc, jnp.zeros_like(qc))
        scores.append(jnp.where(valid, _dot_nt(qm, tile(kw, c)), -jnp.inf))
    probs, sums, lses = [], [], []
    for s in scores:
        m = jnp.max(s, axis=-1, keepdims=True)
        p = jnp.exp2(s - m)
        l = jnp.sum(p, axis=-1, keepdims=True)
        probs.append(p.astype(BF16))
        sums.append(l)
        lses.append((m + jnp.log2(l)) * LN2)
    outs = [_dot(probs[i], tile(vw, c)) / sums[i] for i, (c, _) in enumerate(heads)]
    for c in range(B_GROUP_W // LANES):
        o_ref[0, :, c * LANES:(c + 1) * LANES] = jnp.where(low, outs[2 * c], outs[2 * c + 1]).astype(BF16)
        lse_ref[0, :, c * LANES:(c + 1) * LANES] = jnp.where(low, lses[2 * c], lses[2 * c + 1])


def _attn_b(qg, kg, vg, group, tq):
    window, dilation = B_PAIRS[group]
    radius = window // (2 * dilation)
    bsz, length, _ = qg.shape
    tq = min(tq, length - 2 * radius)
    return pl.pallas_call(
        functools.partial(_attn_b_kernel, radius=radius),
        grid=(bsz, dilation, length // tq),
        in_specs=[
            pl.BlockSpec((1, tq, B_GROUP_W), lambda b, r, i: (b, i, r)),
            pl.BlockSpec((1, length, B_GROUP_W), lambda b, r, i: (b, 0, r)),
            pl.BlockSpec((1, length, B_GROUP_W), lambda b, r, i: (b, 0, r)),
        ],
        out_specs=[pl.BlockSpec((1, tq, B_GROUP_W), lambda b, r, i: (b, i, r))] * 2,
        out_shape=[jax.ShapeDtypeStruct((bsz, length, dilation * B_GROUP_W), BF16),
                   jax.ShapeDtypeStruct((bsz, length, dilation * B_GROUP_W), F32)],
        compiler_params=pltpu.CompilerParams(dimension_semantics=("parallel", "parallel", "parallel")),
        name=f"band_attn_g{group}",
    )(qg, kg, vg)


def _merge_kernel(x_ref, mod_ref, oa_ref, ob0_ref, ls0_ref, ob1_ref, ls1_ref, ob2_ref, ls2_ref,
                  ga_ref, gb_ref, wpa_ref, wpb_ref, wo_ref, wr2_ref, br_ref,
                  x1_ref, h2_ref, route_ref, so1_ref, sl1_ref, so2_ref, sl2_ref):
    tm = x_ref.shape[1]

    def token_major(o_ref, l_ref, group, so_ref, sl_ref):
        dil = B_PAIRS[group][1]
        n_hf = B_GROUP_W // LANES
        for r in range(dil):
            for hf in range(n_hf):
                cols = slice(r * B_GROUP_W + hf * LANES, r * B_GROUP_W + (hf + 1) * LANES)
                so_ref[hf, pl.ds(r, tm // dil, stride=dil), :] = o_ref[0, :, cols].astype(F32)
                sl_ref[hf, pl.ds(r, tm // dil, stride=dil), :] = l_ref[0, :, cols]
        return (jnp.concatenate([so_ref[hf] for hf in range(n_hf)], axis=1),
                jnp.concatenate([sl_ref[hf] for hf in range(n_hf)], axis=1))

    o0, ls0 = ob0_ref[0].astype(F32), ls0_ref[0]
    o1, ls1 = token_major(ob1_ref, ls1_ref, 1, so1_ref, sl1_ref)
    o2, ls2 = token_major(ob2_ref, ls2_ref, 2, so2_ref, sl2_ref)
    mx = jnp.maximum(jnp.maximum(ls0, ls1), ls2)
    e0, e1, e2 = jnp.exp(ls0 - mx), jnp.exp(ls1 - mx), jnp.exp(ls2 - mx)
    ob = ((e0 * o0 + e1 * o1 + e2 * o2) / (e0 + e1 + e2)).astype(BF16)

    n_part = 2
    rows = [slice(p * (tm // n_part), (p + 1) * (tm // n_part)) for p in range(n_part)]
    pa = [_dot(oa_ref[0, r, :], wpa_ref[...]) for r in rows]
    pb = [_dot(ob[r], wpb_ref[...]) for r in rows]
    merged = [(ga_ref[0, r, :].astype(F32) * pa[p] + gb_ref[0, r, :].astype(F32) * pb[p]).astype(BF16)
              for p, r in enumerate(rows)]
    y = [_dot(merged[p], wo_ref[...]) for p in range(n_part)]
    for p, r in enumerate(rows):
        x1 = x_ref[0, r, :] + mod_ref[0, 2:3, :] * y[p]
        x1_ref[0, r, :] = x1
        ms = jnp.mean(x1 * x1, axis=-1, keepdims=True)
        h2 = x1 * lax.rsqrt(ms + EPS) * (1.0 + mod_ref[0, 4:5, :]) + mod_ref[0, 3:4, :]
        h2_hi = h2.astype(BF16)
        h2_ref[0, r, :] = h2_hi
        h2_lo = (h2 - h2_hi.astype(F32)).astype(BF16)
        both = _dot(h2_hi, wr2_ref[...])
        logits = (both[:, :ROUTER_W] + both[:, ROUTER_W:] + _dot(h2_lo, wr2_ref[:, :ROUTER_W])) + br_ref[...]
        route_ref[0, r, :] = _route(logits)


def _route(logits):
    lane = lax.broadcasted_iota(jnp.int32, logits.shape, 1)
    neg = -jnp.inf
    big = ROUTER_W
    is_grp = (lane >= N_EXPERTS) & (lane < N_EXPERTS + N_GROUPS)
    lg = jnp.where(is_grp, logits, neg)
    mg = jnp.max(lg, axis=-1, keepdims=True)
    g_lane = jnp.min(jnp.where(lg == mg, lane, big), axis=-1, keepdims=True)
    g_val = 1.0 / jnp.sum(jnp.exp(lg - mg), axis=-1, keepdims=True)
    lo = (g_lane - N_EXPERTS) * EXPERTS_PER_GROUP
    in_grp = (lane >= lo) & (lane < lo + EXPERTS_PER_GROUP)
    le = jnp.where(in_grp, logits, neg)
    m1 = jnp.max(le, axis=-1, keepdims=True)
    i1 = jnp.min(jnp.where(le == m1, lane, big), axis=-1, keepdims=True)
    le2 = jnp.where(lane == i1, neg, le)
    m2 = jnp.max(le2, axis=-1, keepdims=True)
    i2 = jnp.min(jnp.where(le2 == m2, lane, big), axis=-1, keepdims=True)
    e = jnp.exp(m2 - m1)
    w1 = g_val / (1.0 + e)
    w2 = g_val * e / (1.0 + e)
    return jnp.where(lane == 0, i1.astype(F32),
                     jnp.where(lane == 1, i2.astype(F32),
                               jnp.where(lane == 2, w1, jnp.where(lane == 3, w2, 0.0))))


def _merge(x, mod_l, oa, obs, ga, gb, wpa, wpb, wo, wr, br, tm):
    bsz, seq, d = x.shape
    tok = lambda w: pl.BlockSpec((1, tm, w), lambda b, i: (b, i, 0))
    const = lambda shape: pl.BlockSpec(shape, lambda b, i: (0,) * len(shape),
                                       pipeline_mode=pl.Buffered(1))
    ob_args, ob_specs = [], []
    for (o, lse), (_, dl) in zip(obs, B_PAIRS):
        ob_args += [o, lse]
        ob_specs += [pl.BlockSpec((1, tm // dl, dl * B_GROUP_W), lambda b, i: (b, i, 0))] * 2
    return pl.pallas_call(
        _merge_kernel,
        grid=(bsz, seq // tm),
        in_specs=[tok(d), pl.BlockSpec((1, 6, d), lambda b, i: (b, 0, 0)), tok(oa.shape[-1])]
                 + ob_specs + [tok(d), tok(d)]
                 + [const(w.shape) for w in (wpa, wpb, wo, wr, br)],
        out_specs=[tok(d), tok(d), tok(ROUTER_W)],
        out_shape=[jax.ShapeDtypeStruct((bsz, seq, d), F32),
                   jax.ShapeDtypeStruct((bsz, seq, d), BF16),
                   jax.ShapeDtypeStruct((bsz, seq, ROUTER_W), F32)],
        scratch_shapes=[pltpu.VMEM((B_GROUP_W // LANES, tm, LANES), F32)] * 4,
        compiler_params=pltpu.CompilerParams(dimension_semantics=("parallel", "parallel")),
        name="merge_proj",
    )(x, mod_l, oa, *ob_args, ga, gb, wpa, wpb, wo, wr, br)


MOE_BLOCK = 1024
SEG_ALIGN = 16
ROW_CHUNK = 128
PERM_CHUNK = 256
EXPERTS_PER_STEP = 4
SORTED_ROWS = -(-(2 * MOE_BLOCK + N_EXPERTS * (SEG_ALIGN - 1)) // PERM_CHUNK) * PERM_CHUNK
SORTED_ROWS_ALLOC = SORTED_ROWS + ROW_CHUNK


def _plan_kernel(route_ref, posc_ref, posr_ref, base_ref, npad_ref):
    r = route_ref[0]
    tb = r.shape[0]
    lane = lax.broadcasted_iota(jnp.int32, r.shape, 1).astype(F32)
    oh1 = jnp.where(lane == r[:, 0:1], 1.0, 0.0)
    oh2 = jnp.where(lane == r[:, 1:2], 1.0, 0.0)
    cnt1 = jnp.sum(oh1, axis=0, keepdims=True)
    cnt2 = jnp.sum(oh2, axis=0, keepdims=True)
    npad = jnp.floor((cnt1 + cnt2 + (SEG_ALIGN - 1)) * (1.0 / SEG_ALIGN)) * SEG_ALIGN
    ri = lax.broadcasted_iota(jnp.int32, (LANES, LANES), 0)
    ci = lax.broadcasted_iota(jnp.int32, (LANES, LANES), 1)
    upper = jnp.where(ri < ci, 1.0, 0.0)
    npad8 = jnp.broadcast_to(npad, (8, LANES))
    base = jnp.dot(npad8, upper, precision=HIGHEST, preferred_element_type=F32)[0:1]

    ti = lax.broadcasted_iota(jnp.int32, (tb, tb), 0)
    tj = lax.broadcasted_iota(jnp.int32, (tb, tb), 1)
    before = jnp.where(tj < ti, 1.0, 0.0).astype(BF16)
    pre1 = _dot(before, oh1.astype(BF16))
    pre2 = _dot(before, oh2.astype(BF16))
    pos1 = jnp.sum(oh1 * (base + pre1), axis=-1, keepdims=True)
    pos2 = jnp.sum(oh2 * (base + cnt1 + pre2), axis=-1, keepdims=True)
    packed = jnp.where(lane == 0.0, pos1, jnp.where(lane == 1.0, pos2, jnp.where(lane >= 2.0, r, 0.0)))
    posc_ref[0] = packed
    posr_ref[0] = packed.T[0:8, :]
    base_ref[0] = base
    npad_ref[0] = npad


def _moe_plan(route):
    n_blk, tb, _ = route.shape
    vec = pl.BlockSpec((1, 1, LANES), lambda i: (i, 0, 0))
    return pl.pallas_call(
        _plan_kernel,
        grid=(n_blk,),
        in_specs=[pl.BlockSpec((1, tb, ROUTER_W), lambda i: (i, 0, 0))],
        out_specs=[pl.BlockSpec((1, tb, LANES), lambda i: (i, 0, 0)),
                   pl.BlockSpec((1, 8, tb), lambda i: (i, 0, 0)), vec, vec],
        out_shape=[jax.ShapeDtypeStruct((n_blk, tb, LANES), F32),
                   jax.ShapeDtypeStruct((n_blk, 8, tb), F32),
                   jax.ShapeDtypeStruct((n_blk, 1, LANES), F32),
                   jax.ShapeDtypeStruct((n_blk, 1, LANES), F32)],
        compiler_params=pltpu.CompilerParams(dimension_semantics=("parallel",)),
        name="moe_plan",
    )(route)


def _moe_kernel(base_ref, npad_ref, x1_ref, mod_ref, h_ref, posc_ref, posr_ref, wg_ref, wu_ref, wd_ref,
                o_ref, xs_ref, ys_ref, gathered_ref):
    blk = pl.program_id(0)
    e = pl.program_id(1)
    tb = h_ref.shape[1]

    n_here = wg_ref.shape[0]
    n_experts = pl.num_programs(1) * n_here

    @pl.when(e == 0)
    def _():
        gathered_ref[0] = 0
        xs_ref[SORTED_ROWS:, :] = jnp.zeros((ROW_CHUNK, xs_ref.shape[1]), BF16)
        ys_ref[...] = jnp.zeros(ys_ref.shape, BF16)

    nxt = jnp.minimum((e + 2) * n_here, n_experts)
    need_rows = jnp.minimum(base_ref[blk, nxt] + ROW_CHUNK, SORTED_ROWS)
    need_chunks = (need_rows + PERM_CHUNK - 1) // PERM_CHUNK

    def gather_chunk(c, carry):
        r0 = pl.multiple_of(c * PERM_CHUNK, PERM_CHUNK)
        rid = (r0 + lax.broadcasted_iota(jnp.int32, (PERM_CHUNK, tb), 0)).astype(F32)
        pr = posr_ref[0]
        sel = jnp.where(rid == pr[0:1], 1.0, jnp.where(rid == pr[1:2], 1.0, 0.0)).astype(BF16)
        xs_ref[pl.ds(r0, PERM_CHUNK), :] = _dot(sel, h_ref[0]).astype(BF16)
        return carry

    lax.fori_loop(gathered_ref[0], need_chunks, gather_chunk, 0)
    gathered_ref[0] = jnp.maximum(gathered_ref[0], need_chunks)

    def gate_up(r0, j):
        xc = xs_ref[pl.ds(r0, ROW_CHUNK), :]
        return _dot(xc, wg_ref[j]), _dot(xc, wu_ref[j])

    def down(r0, j, a, u):
        hid = (a * _sigmoid(a)) * u
        ys_ref[pl.ds(r0, ROW_CHUNK), :] = _dot(hid.astype(BF16), wd_ref[j]).astype(BF16)

    starts = [pl.multiple_of(base_ref[blk, e * n_here + j], SEG_ALIGN) for j in range(n_here)]
    first = [gate_up(starts[j], j) for j in range(n_here)]
    for j in range(n_here):
        down(starts[j], j, *first[j])
    for j in range(n_here):
        seg_rows = npad_ref[blk, e * n_here + j]

        def more(c, carry, j=j, seg_rows=seg_rows):
            r0 = starts[j] + jnp.minimum(c * ROW_CHUNK, seg_rows - ROW_CHUNK)
            r0 = pl.multiple_of(r0, SEG_ALIGN)
            down(r0, j, *gate_up(r0, j))
            return carry

        lax.fori_loop(1, (seg_rows + ROW_CHUNK - 1) // ROW_CHUNK, more, 0)

    @pl.when(e == pl.num_programs(1) - 1)
    def _():
        ys = ys_ref[...]
        for c in range(tb // PERM_CHUNK):
            pc = posc_ref[0, c * PERM_CHUNK:(c + 1) * PERM_CHUNK, :]
            rid = lax.broadcasted_iota(jnp.int32, (PERM_CHUNK, SORTED_ROWS_ALLOC), 1).astype(F32)
            wsel = (jnp.where(rid == pc[:, 0:1], pc[:, 2:3], 0.0)
                    + jnp.where(rid == pc[:, 1:2], pc[:, 3:4], 0.0)).astype(BF16)
            y = _dot(wsel, ys)
            rows = slice(c * PERM_CHUNK, (c + 1) * PERM_CHUNK)
            o_ref[0, rows, :] = x1_ref[0, rows, :] + mod_ref[0, 5:6, :] * y


def _moe(x1, mod_l, h2, route, weg, weu, wed):
    bsz, seq, d = x1.shape
    n_e, _, d_e = weg.shape
    tb = min(MOE_BLOCK, seq)
    assert tb == MOE_BLOCK, "sorted-row capacity is sized for MOE_BLOCK tokens"
    per_batch = seq // tb
    n_blk = bsz * per_batch
    blocked = lambda a: a.reshape(n_blk, tb, a.shape[-1])
    posc, posr, base, npad = _moe_plan(blocked(route))
    base_i = base.reshape(n_blk, LANES).astype(jnp.int32)
    npad_i = npad.reshape(n_blk, LANES).astype(jnp.int32)

    tok = lambda w: pl.BlockSpec((1, tb, w), lambda i, e, *_: (i, 0, 0))
    grid_spec = pltpu.PrefetchScalarGridSpec(
        num_scalar_prefetch=2,
        grid=(n_blk, n_e // EXPERTS_PER_STEP),
        in_specs=[tok(d),
                  pl.BlockSpec((1, 6, d), lambda i, e, *_: (i // per_batch, 0, 0)),
                  tok(d), tok(LANES),
                  pl.BlockSpec((1, 8, tb), lambda i, e, *_: (i, 0, 0)),
                  pl.BlockSpec((EXPERTS_PER_STEP, d, d_e), lambda i, e, *_: (e, 0, 0)),
                  pl.BlockSpec((EXPERTS_PER_STEP, d, d_e), lambda i, e, *_: (e, 0, 0)),
                  pl.BlockSpec((EXPERTS_PER_STEP, d_e, d), lambda i, e, *_: (e, 0, 0))],
        out_specs=tok(d),
        scratch_shapes=[pltpu.VMEM((SORTED_ROWS_ALLOC, d), BF16),
                        pltpu.VMEM((SORTED_ROWS_ALLOC, d), BF16),
                        pltpu.SMEM((1,), jnp.int32)],
    )
    out = pl.pallas_call(
        _moe_kernel,
        grid_spec=grid_spec,
        out_shape=jax.ShapeDtypeStruct((n_blk, tb, d), F32),
        compiler_params=pltpu.CompilerParams(dimension_semantics=("parallel", "arbitrary")),
        name="moe_experts",
    )(base_i, npad_i, blocked(x1), mod_l, blocked(h2), posc, posr, weg, weu, wed)
    return out.reshape(bsz, seq, d)


def _tiles(seq):
    return dict(
        tm_proj=min(512, seq),
        tq_a=min(1024, seq),
        tk_a=min(1024, seq),
        tq_b=256,
        tm_merge=min(512, seq),
    )


def kernel(x, c, positions, w_ada, b_ada, w_in, qn_a, kn_a, lam_q1, lam_k1, lam_q2, lam_k2,
           subln_a, qn_b, kn_b, w_pa, w_pb, w_o, w_r1, b_r1, w_r2, b_r2,
           w_e_gate, w_e_up, w_e_down):
    depth = w_ada.shape[0]
    bsz, seq, d = x.shape
    t = _tiles(seq)

    mod = _ada(c, w_ada, b_ada).reshape(depth, bsz, 6, d)
    cos_l, sin_l = _rope_tables(positions)
    seg = jnp.kron(jnp.eye(256 // HEAD_DIM, dtype=F32),
                   jnp.full((HEAD_DIM, HEAD_DIM), 1.0 / HEAD_DIM, F32)).astype(BF16)
    q_scale = HEAD_DIM ** -0.5 * LOG2E

    for layer in range(depth):
        lam_init = 0.8 - 0.6 * math.exp(-0.3 * layer)
        w = w_in[layer]
        wqk = jnp.concatenate([w[:, 0:1024], w[:, 1536:3072]], axis=1).astype(BF16)
        wvat = w[:, 1024:1536].T.astype(BF16)
        wvb = w[:, 3072:3840].astype(BF16)
        wg = w[:, 3840:].astype(BF16)
        gain = jnp.concatenate([
            jnp.tile(qn_a[layer] * q_scale, 8), jnp.tile(kn_a[layer], 8),
            jnp.tile(qn_b[layer] * q_scale, 12), jnp.tile(kn_b[layer], 12)]).reshape(1, -1)

        qa, ka, qb, kb, vat, vb, ga, gb = _inproj(x, mod[layer], cos_l, sin_l, gain, wqk, wvat, wvb, wg, seg,
                                                  t["tm_proj"])

        score_bound = (1.01 * HEAD_DIM * q_scale * jnp.max(jnp.abs(qn_a[layer]))
                       * jnp.max(jnp.abs(kn_a[layer]))).reshape(1)
        lam_p = jnp.stack([lam_q1[layer], lam_k1[layer], lam_q2[layer], lam_k2[layer]])
        oa = _attn_a(score_bound, lam_p, subln_a[layer].reshape(1, -1), qa, ka, vat, lam_init,
                     t["tq_a"], t["tk_a"])
        obs = [_attn_b(qb[g], kb[g], vb[g], g, t["tq_b"]) for g in range(B_GROUPS)]

        wr = jnp.zeros((d, ROUTER_W), F32)
        wr = wr.at[:, :N_EXPERTS].set(w_r2[layer]).at[:, N_EXPERTS:N_EXPERTS + N_GROUPS].set(w_r1[layer])
        br = jnp.zeros((1, ROUTER_W), F32)
        br = br.at[0, :N_EXPERTS].set(b_r2[layer]).at[0, N_EXPERTS:N_EXPERTS + N_GROUPS].set(b_r1[layer])
        wr_hi = wr.astype(BF16)
        wr2 = jnp.concatenate([wr_hi, (wr - wr_hi.astype(F32)).astype(BF16)], axis=1)
        x1, h2, route = _merge(x, mod[layer], oa, obs, ga, gb,
                               w_pa[layer].astype(BF16), w_pb[layer].astype(BF16), w_o[layer].astype(BF16),
                               wr2, br, t["tm_merge"])

        x = _moe(x1, mod[layer], h2, route,
                 w_e_gate[layer].astype(BF16), w_e_up[layer].astype(BF16), w_e_down[layer].astype(BF16))
    return x
```

```python
import functools
import math

import jax
import jax.numpy as jnp
from jax import lax
from jax.experimental import pallas as pl
from jax.experimental.pallas import tpu as pltpu

EPS = 1e-6
ROPE_THETA = 10000.0
LOG2E = math.log2(math.e)
LN2 = math.log(2.0)

A_HEADS = 4
HEAD_DIM = 64
LANES = 128
B_PAIRS = ((128, 1), (512, 4), (2048, 16))
B_GROUPS = len(B_PAIRS)
B_GROUP_W = 256
N_GROUPS = 4
EXPERTS_PER_GROUP = 8
N_EXPERTS = N_GROUPS * EXPERTS_PER_GROUP
ROUTER_W = 128

IN_COL_VA = 1024
IN_COLS_QK = (0, 256, 512, 768, 1536, 1792, 2048, 2304, 2560, 2816)
IN_COL_VB = 3072
IN_COL_GATES = 3840

F32 = jnp.float32
BF16 = jnp.bfloat16
HIGHEST = lax.Precision.HIGHEST


def _dot(a, b):
    return jnp.dot(a, b, preferred_element_type=F32)


def _dot_nt(a, b):
    return lax.dot_general(a, b, (((1,), (1,)), ((), ())), preferred_element_type=F32)


def _sigmoid(x):
    return 1.0 / (1.0 + jnp.exp(-x))


def _ada_kernel(c_ref, w_ref, b_ref, o_ref):
    c = c_ref[...]
    c_act = c * _sigmoid(c)
    o_ref[0] = jnp.dot(c_act, w_ref[0], precision=HIGHEST, preferred_element_type=F32) + b_ref[0]


def _ada(c, w_ada, b_ada):
    depth, d, six_d = w_ada.shape
    bsz = c.shape[0]
    n_col = six_d // d
    return pl.pallas_call(
        _ada_kernel,
        grid=(depth, n_col),
        in_specs=[
            pl.BlockSpec((bsz, d), lambda l, j: (0, 0)),
            pl.BlockSpec((1, d, d), lambda l, j: (l, 0, j)),
            pl.BlockSpec((1, 1, d), lambda l, j: (l, 0, j)),
        ],
        out_specs=pl.BlockSpec((1, bsz, d), lambda l, j: (l, 0, j)),
        out_shape=jax.ShapeDtypeStruct((depth, bsz, six_d), F32),
        name="ada_mod",
    )(c, w_ada, b_ada.reshape(depth, 1, six_d))


def _rope_kernel(pos_ref, f_ref, cos_ref, sin_ref):
    ang = pos_ref[0].astype(F32) * f_ref[...]
    cos = jnp.cos(ang)
    sin = jnp.sin(ang)
    cos_ref[0] = jnp.concatenate([cos, cos, cos, cos], axis=0).T
    sin_ref[0] = jnp.concatenate([-sin, sin, -sin, sin], axis=0).T


def _rope_tables(positions, ts):
    bsz, seq = positions.shape
    half = HEAD_DIM // 2
    inv_freq = ROPE_THETA ** (-jnp.arange(0, HEAD_DIM, 2, dtype=F32) / HEAD_DIM)
    return pl.pallas_call(
        _rope_kernel,
        grid=(bsz, seq // ts),
        in_specs=[
            pl.BlockSpec((1, 1, ts), lambda b, i: (b, 0, i)),
            pl.BlockSpec((half, 1), lambda b, i: (0, 0)),
        ],
        out_specs=[pl.BlockSpec((1, ts, LANES), lambda b, i: (b, i, 0))] * 2,
        out_shape=[jax.ShapeDtypeStruct((bsz, seq, LANES), F32)] * 2,
        name="rope_tables",
    )(positions.reshape(bsz, 1, seq), inv_freq.reshape(half, 1))


def _inproj_kernel(x_ref, mod_ref, cos_ref, sin_ref, gain_ref, w_ref, wvat_ref, seg_ref,
                   qa_ref, ka_ref, qb0_ref, qb1_ref, qb2_ref, kb0_ref, kb1_ref, kb2_ref,
                   vat_ref, vb0_ref, vb1_ref, vb2_ref, ga_ref, gb_ref, stage_ref):
    x = x_ref[0]
    tm = x.shape[0]
    ms = jnp.mean(x * x, axis=-1, keepdims=True)
    h = x * lax.rsqrt(ms + EPS) * (1.0 + mod_ref[0, 1:2, :]) + mod_ref[0, 0:1, :]
    hb = h.astype(BF16)

    cos = cos_ref[0]
    sin = sin_ref[0]
    seg = seg_ref[...]
    lane = lax.broadcasted_iota(jnp.int32, cos.shape, 1)
    first_half = (lane % HEAD_DIM) < (HEAD_DIM // 2)

    def norm_rope(y2, c):
        msq = _dot((y2 * y2).astype(BF16), seg)
        yn2 = y2 * lax.rsqrt(msq + EPS) * gain_ref[:, c * 256:(c + 1) * 256]
        out = []
        for hf in range(2):
            yn = yn2[:, hf * LANES:(hf + 1) * LANES]
            partner = jnp.where(first_half, pltpu.roll(yn, 96, axis=1), pltpu.roll(yn, 32, axis=1))
            out.append(yn * cos + partner * sin)
        return out

    def store_dilated(ref, group, val):
        dil = B_PAIRS[group][1]
        if dil == 1:
            ref[0] = val.astype(BF16)
            return
        for hf in range(B_GROUP_W // LANES):
            stage_ref[hf] = val[:, hf * LANES:(hf + 1) * LANES]
        for r in range(dil):
            for hf in range(B_GROUP_W // LANES):
                rows = stage_ref[hf, pl.ds(r, tm // dil, stride=dil), :]
                col = r * B_GROUP_W + hf * LANES
                ref[0, :, col:col + LANES] = rows.astype(BF16)

    flat = [(qa_ref, 0), (qa_ref, 1), (ka_ref, 0), (ka_ref, 1)]
    grouped = [(qb0_ref, 0), (qb1_ref, 1), (qb2_ref, 2), (kb0_ref, 0), (kb1_ref, 1), (kb2_ref, 2)]
    def qk_epilogue(c, y2):
        halves = norm_rope(y2, c)
        if c < len(flat):
            ref, t = flat[c]
            for hf in range(2):
                ref[0, :, t * 256 + hf * LANES:t * 256 + (hf + 1) * LANES] = halves[hf].astype(BF16)
        else:
            ref, group = grouped[c - len(flat)]
            store_dilated(ref, group, jnp.concatenate(halves, axis=1))

    def vat_epilogue(c, v):
        vat_ref[0, c * 256:(c + 1) * 256, :] = v.astype(BF16)

    def gate_epilogue(c, g):
        ref = ga_ref if c < 4 else gb_ref
        ref[0, :, (c % 4) * 256:(c % 4 + 1) * 256] = _sigmoid(g).astype(BF16)

    proj = lambda col: _dot(hb, w_ref[:, col:col + 256])
    jobs = []
    for c, col in enumerate(IN_COLS_QK):
        jobs.append((functools.partial(proj, col), functools.partial(qk_epilogue, c)))
    for c in range(wvat_ref.shape[0] // 256):
        jobs.append((lambda c=c: _dot_nt(wvat_ref[c * 256:(c + 1) * 256, :], hb),
                     functools.partial(vat_epilogue, c)))
    for group, ref in enumerate((vb0_ref, vb1_ref, vb2_ref)):
        jobs.append((functools.partial(proj, IN_COL_VB + group * 256),
                     functools.partial(store_dilated, ref, group)))
    for c in range((w_ref.shape[1] - IN_COL_GATES) // 256):
        jobs.append((functools.partial(proj, IN_COL_GATES + c * 256), functools.partial(gate_epilogue, c)))
    pending = None
    for matmul, epilogue in jobs:
        res = matmul()
        if pending is not None:
            pending[0](pending[1])
        pending = (epilogue, res)
    pending[0](pending[1])


def _inproj(x, mod_l, cos_l, sin_l, gain, w, wvat, seg, tm):
    bsz, seq, d = x.shape
    a_v_w = wvat.shape[0]
    tok = lambda w: pl.BlockSpec((1, tm, w), lambda b, i: (b, i, 0))
    tok_t = pl.BlockSpec((1, a_v_w, tm), lambda b, i: (b, 0, i))
    const = lambda shape: pl.BlockSpec(shape, lambda b, i: (0,) * len(shape),
                                       pipeline_mode=pl.Buffered(1))
    row = lambda w: jax.ShapeDtypeStruct((bsz, seq, w), BF16)
    dil_specs = [pl.BlockSpec((1, tm // dl, dl * B_GROUP_W), lambda b, i: (b, i, 0)) for _, dl in B_PAIRS]
    dil_shapes = [jax.ShapeDtypeStruct((bsz, seq // dl, dl * B_GROUP_W), BF16) for _, dl in B_PAIRS]
    outs = pl.pallas_call(
        _inproj_kernel,
        grid=(bsz, seq // tm),
        in_specs=[
            tok(d),
            pl.BlockSpec((1, 6, d), lambda b, i: (b, 0, 0)),
            tok(LANES), tok(LANES),
            const(gain.shape), const(w.shape), const(wvat.shape), const(seg.shape),
        ],
        out_specs=[tok(512), tok(512)] + dil_specs + dil_specs + [tok_t] + dil_specs + [tok(d), tok(d)],
        out_shape=[row(512), row(512)] + dil_shapes + dil_shapes
                  + [jax.ShapeDtypeStruct((bsz, a_v_w, seq), BF16)] + dil_shapes + [row(d), row(d)],
        scratch_shapes=[pltpu.VMEM((B_GROUP_W // LANES, tm, LANES), F32)],
        compiler_params=pltpu.CompilerParams(dimension_semantics=("parallel", "parallel")),
        name="in_proj",
    )(x, mod_l, cos_l, sin_l, gain, w, wvat, seg)
    qa, ka = outs[0], outs[1]
    qb, kb, vat, vb, ga, gb = outs[2:5], outs[5:8], outs[8], outs[9:12], outs[12], outs[13]
    return qa, ka, qb, kb, vat, vb, ga, gb


SCORE_BOUND_NO_SHIFT = 64.0
KV_CHUNK = 256


def _attn_a_kernel(bound_ref, lam_ref, sub_ref, q_ref, k_ref, vt_ref, o_ref,
                   m_ref, l_ref, acc_ref, pa_ref, pb_ref, *, tk, lam_init):
    q = q_ref[0]
    tq = q.shape[0]
    seq = k_ref.shape[1]
    lane = lax.broadcasted_iota(jnp.int32, q.shape, 1)
    zero = jnp.zeros_like(q)
    q_maps = (jnp.where(lane < HEAD_DIM, q, zero), jnp.where(lane >= HEAD_DIM, q, zero))
    n_chunk = tk // KV_CHUNK

    acc_ref[...] = jnp.zeros(acc_ref.shape, F32)
    l_ref[...] = jnp.zeros(l_ref.shape, F32)

    def scores_exp(tile, p_ref):
        start = pl.multiple_of(tile * tk, tk)
        k = k_ref[0, pl.ds(start, tk), :]
        for mi in range(2):
            p = jnp.exp2(_dot_nt(k, q_maps[mi]))
            l_ref[mi] += jnp.sum(p.reshape(tk // 8, 8, tq), axis=0)
            p_ref[mi] = p.astype(BF16)

    def weighted_values(tile, p_ref):
        start = pl.multiple_of(tile * tk, tk)
        vt = vt_ref[0, :, pl.ds(start, tk)]
        for mi in range(2):
            acc_ref[mi] += _dot(vt, p_ref[mi])

    n_tiles = seq // tk

    def no_shift_pair(jj, carry):
        t = 2 * jj
        scores_exp(t + 1, pb_ref)
        weighted_values(t, pa_ref)
        scores_exp(t + 2, pa_ref)
        weighted_values(t + 1, pb_ref)
        return carry

    def no_shift_loop():
        scores_exp(0, pa_ref)
        lax.fori_loop(0, n_tiles // 2 - 1, no_shift_pair, 0)
        scores_exp(n_tiles - 1, pb_ref)
        weighted_values(n_tiles - 2, pa_ref)
        weighted_values(n_tiles - 1, pb_ref)

    def online_max_body(j, carry):
        for c in range(n_chunk):
            start = pl.multiple_of(j * tk + c * KV_CHUNK, KV_CHUNK)
            k = k_ref[0, pl.ds(start, KV_CHUNK), :]
            vt = vt_ref[0, :, pl.ds(start, KV_CHUNK)]
            for mi in range(2):
                s = _dot_nt(k, q_maps[mi])
                m_old = m_ref[mi]
                m_new = jnp.maximum(m_old, jnp.max(s, axis=0, keepdims=True))
                alpha = jnp.exp2(m_old - m_new)
                p = jnp.exp2(s - m_new[0:1])
                l_ref[mi] = alpha * l_ref[mi] + jnp.sum(p.reshape(KV_CHUNK // 8, 8, tq), axis=0)
                acc_ref[mi] = alpha[0:1] * acc_ref[mi] + _dot(vt, p.astype(BF16))
                m_ref[mi] = m_new
        return carry

    no_shift = bound_ref[0] <= SCORE_BOUND_NO_SHIFT

    @pl.when(no_shift)
    def _():
        no_shift_loop()

    @pl.when(jnp.logical_not(no_shift))
    def _():
        m_ref[...] = jnp.full(m_ref.shape, -jnp.inf, F32)
        lax.fori_loop(0, seq // tk, online_max_body, 0)

    lam_p = lam_ref[...]
    s1 = jnp.sum(lam_p[0:1] * lam_p[1:2], axis=-1, keepdims=True)
    s2 = jnp.sum(lam_p[2:3] * lam_p[3:4], axis=-1, keepdims=True)
    lam = jnp.exp(s1) - jnp.exp(s2) + lam_init
    l0 = jnp.sum(l_ref[0], axis=0, keepdims=True)
    l1 = jnp.sum(l_ref[1], axis=0, keepdims=True)
    ot = acc_ref[0] / l0 - lam * (acc_ref[1] / l1)
    o = ot.T
    msq = jnp.mean(o * o, axis=-1, keepdims=True)
    o = o * lax.rsqrt(msq + EPS) * sub_ref[...] * (1.0 - lam_init)
    o_ref[0] = o.astype(BF16)


def _attn_a(score_bound, lam_p, subln, qa, ka, vat, lam_init, tq, tk):
    bsz, seq, _ = qa.shape
    return pl.pallas_call(
        functools.partial(_attn_a_kernel, tk=tk, lam_init=lam_init),
        grid=(bsz, A_HEADS, seq // tq),
        in_specs=[
            pl.BlockSpec(memory_space=pltpu.SMEM),
            pl.BlockSpec(lam_p.shape, lambda b, h, i: (0, 0)),
            pl.BlockSpec(subln.shape, lambda b, h, i: (0, 0)),
            pl.BlockSpec((1, tq, LANES), lambda b, h, i: (b, i, h)),
            pl.BlockSpec((1, seq, LANES), lambda b, h, i: (b, 0, h)),
            pl.BlockSpec((1, LANES, seq), lambda b, h, i: (b, h, 0)),
        ],
        out_specs=pl.BlockSpec((1, tq, LANES), lambda b, h, i: (b, i, h)),
        out_shape=jax.ShapeDtypeStruct((bsz, seq, A_HEADS * LANES), BF16),
        scratch_shapes=[
            pltpu.VMEM((2, 8, tq), F32),
            pltpu.VMEM((2, 8, tq), F32),
            pltpu.VMEM((2, LANES, tq), F32),
            pltpu.VMEM((2, tk, tq), BF16),
            pltpu.VMEM((2, tk, tq), BF16),
        ],
        compiler_params=pltpu.CompilerParams(dimension_semantics=("parallel", "parallel", "parallel")),
        name="diff_attn",
    )(score_bound, lam_p, subln, qa, ka, vat)


def _attn_b_kernel(q_ref, k_ref, v_ref, o_ref, lse_ref, *, radius):
    tq = q_ref.shape[1]
    length = k_ref.shape[1]
    win = tq + 2 * radius
    t0 = pl.program_id(2) * tq
    start = pl.multiple_of(jnp.clip(t0 - radius, 0, length - win), radius)
    kw = k_ref[0, pl.ds(start, win), :]
    vw = v_ref[0, pl.ds(start, win), :]
    q = q_ref[0]

    qpos = t0 + lax.broadcasted_iota(jnp.int32, (tq, win), 0)
    kpos = start + lax.broadcasted_iota(jnp.int32, (tq, win), 1)
    valid = jnp.abs(kpos - qpos) <= radius
    lane = lax.broadcasted_iota(jnp.int32, (tq, LANES), 1)
    low = lane < HEAD_DIM

    heads = [(c, half) for c in range(B_GROUP_W // LANES) for half in range(2)]
    tile = lambda a, c: a[:, c * LANES:(c + 1) * LANES]
    scores = []
    for c, half in heads:
        qc = tile(q, c)
        qm = jnp.where(low if half == 0 else jnp.logical_not(low), qc, jnp.zeros_like(qc))
        scores.append(jnp.where(valid, _dot_nt(qm, tile(kw, c)), -jnp.inf))
    probs, sums, lses = [], [], []
    for s in scores:
        m = jnp.max(s, axis=-1, keepdims=True)
        p = jnp.exp2(s - m)
        l = jnp.sum(p, axis=-1, keepdims=True)
        probs.append(p.astype(BF16))
        sums.append(l)
        lses.append((m + jnp.log2(l)) * LN2)
    outs = [_dot(probs[i], tile(vw, c)) / sums[i] for i, (c, _) in enumerate(heads)]
    for c in range(B_GROUP_W // LANES):
        o_ref[0, :, c * LANES:(c + 1) * LANES] = jnp.where(low, outs[2 * c], outs[2 * c + 1]).astype(BF16)
        lse_ref[0, :, c * LANES:(c + 1) * LANES] = jnp.where(low, lses[2 * c], lses[2 * c + 1])


def _attn_b(qg, kg, vg, group, tq):
    window, dilation = B_PAIRS[group]
    radius = window // (2 * dilation)
    bsz, length, _ = qg.shape
    tq = min(tq, length - 2 * radius)
    return pl.pallas_call(
        functools.partial(_attn_b_kernel, radius=radius),
        grid=(bsz, dilation, length // tq),
        in_specs=[
            pl.BlockSpec((1, tq, B_GROUP_W), lambda b, r, i: (b, i, r)),
            pl.BlockSpec((1, length, B_GROUP_W), lambda b, r, i: (b, 0, r)),
            pl.BlockSpec((1, length, B_GROUP_W), lambda b, r, i: (b, 0, r)),
        ],
        out_specs=[pl.BlockSpec((1, tq, B_GROUP_W), lambda b, r, i: (b, i, r))] * 2,
        out_shape=[jax.ShapeDtypeStruct((bsz, length, dilation * B_GROUP_W), BF16),
                   jax.ShapeDtypeStruct((bsz, length, dilation * B_GROUP_W), F32)],
        compiler_params=pltpu.CompilerParams(dimension_semantics=("parallel", "parallel", "parallel")),
        name=f"band_attn_g{group}",
    )(qg, kg, vg)


def _merge_kernel(x_ref, mod_ref, oa_ref, ob0_ref, ls0_ref, ob1_ref, ls1_ref, ob2_ref, ls2_ref,
                  ga_ref, gb_ref, wpa_ref, wpb_ref, wo_ref, wr2_ref, br_ref,
                  x1_ref, h2_ref, route_ref, so1_ref, sl1_ref, so2_ref, sl2_ref):
    tm = x_ref.shape[1]

    def token_major(o_ref, l_ref, group, so_ref, sl_ref):
        dil = B_PAIRS[group][1]
        n_hf = B_GROUP_W // LANES
        for r in range(dil):
            for hf in range(n_hf):
                cols = slice(r * B_GROUP_W + hf * LANES, r * B_GROUP_W + (hf + 1) * LANES)
                so_ref[hf, pl.ds(r, tm // dil, stride=dil), :] = o_ref[0, :, cols].astype(F32)
                sl_ref[hf, pl.ds(r, tm // dil, stride=dil), :] = l_ref[0, :, cols]
        return (jnp.concatenate([so_ref[hf] for hf in range(n_hf)], axis=1),
                jnp.concatenate([sl_ref[hf] for hf in range(n_hf)], axis=1))

    o0, ls0 = ob0_ref[0].astype(F32), ls0_ref[0]
    o1, ls1 = token_major(ob1_ref, ls1_ref, 1, so1_ref, sl1_ref)
    o2, ls2 = token_major(ob2_ref, ls2_ref, 2, so2_ref, sl2_ref)
    mx = jnp.maximum(jnp.maximum(ls0, ls1), ls2)
    e0, e1, e2 = jnp.exp(ls0 - mx), jnp.exp(ls1 - mx), jnp.exp(ls2 - mx)
    ob = ((e0 * o0 + e1 * o1 + e2 * o2) / (e0 + e1 + e2)).astype(BF16)

    n_part = 2
    rows = [slice(p * (tm // n_part), (p + 1) * (tm // n_part)) for p in range(n_part)]
    pa = [_dot(oa_ref[0, r, :], wpa_ref[...]) for r in rows]
    pb = [_dot(ob[r], wpb_ref[...]) for r in rows]
    merged = [(ga_ref[0, r, :].astype(F32) * pa[p] + gb_ref[0, r, :].astype(F32) * pb[p]).astype(BF16)
              for p, r in enumerate(rows)]
    y = [_dot(merged[p], wo_ref[...]) for p in range(n_part)]
    for p, r in enumerate(rows):
        x1 = x_ref[0, r, :] + mod_ref[0, 2:3, :] * y[p]
        x1_ref[0, r, :] = x1
        ms = jnp.mean(x1 * x1, axis=-1, keepdims=True)
        h2 = x1 * lax.rsqrt(ms + EPS) * (1.0 + mod_ref[0, 4:5, :]) + mod_ref[0, 3:4, :]
        h2_hi = h2.astype(BF16)
        h2_ref[0, r, :] = h2_hi
        h2_lo = (h2 - h2_hi.astype(F32)).astype(BF16)
        both = _dot(h2_hi, wr2_ref[...])
        logits = (both[:, :ROUTER_W] + both[:, ROUTER_W:] + _dot(h2_lo, wr2_ref[:, :ROUTER_W])) + br_ref[...]
        route_ref[0, r, :] = _route(logits)


def _route(logits):
    lane = lax.broadcasted_iota(jnp.int32, logits.shape, 1)
    neg = -jnp.inf
    big = ROUTER_W
    is_grp = (lane >= N_EXPERTS) & (lane < N_EXPERTS + N_GROUPS)
    lg = jnp.where(is_grp, logits, neg)
    mg = jnp.max(lg, axis=-1, keepdims=True)
    g_lane = jnp.min(jnp.where(lg == mg, lane, big), axis=-1, keepdims=True)
    g_val = 1.0 / jnp.sum(jnp.exp(lg - mg), axis=-1, keepdims=True)
    lo = (g_lane - N_EXPERTS) * EXPERTS_PER_GROUP
    in_grp = (lane >= lo) & (lane < lo + EXPERTS_PER_GROUP)
    le = jnp.where(in_grp, logits, neg)
    m1 = jnp.max(le, axis=-1, keepdims=True)
    i1 = jnp.min(jnp.where(le == m1, lane, big), axis=-1, keepdims=True)
    le2 = jnp.where(lane == i1, neg, le)
    m2 = jnp.max(le2, axis=-1, keepdims=True)
    i2 = jnp.min(jnp.where(le2 == m2, lane, big), axis=-1, keepdims=True)
    e = jnp.exp(m2 - m1)
    w1 = g_val / (1.0 + e)
    w2 = g_val * e / (1.0 + e)
    return jnp.where(lane == 0, i1.astype(F32),
                     jnp.where(lane == 1, i2.astype(F32),
                               jnp.where(lane == 2, w1, jnp.where(lane == 3, w2, 0.0))))


def _merge(x, mod_l, oa, obs, ga, gb, wpa, wpb, wo, wr, br, tm):
    bsz, seq, d = x.shape
    tok = lambda w: pl.BlockSpec((1, tm, w), lambda b, i: (b, i, 0))
    const = lambda shape: pl.BlockSpec(shape, lambda b, i: (0,) * len(shape),
                                       pipeline_mode=pl.Buffered(1))
    ob_args, ob_specs = [], []
    for (o, lse), (_, dl) in zip(obs, B_PAIRS):
        ob_args += [o, lse]
        ob_specs += [pl.BlockSpec((1, tm // dl, dl * B_GROUP_W), lambda b, i: (b, i, 0))] * 2
    return pl.pallas_call(
        _merge_kernel,
        grid=(bsz, seq // tm),
        in_specs=[tok(d), pl.BlockSpec((1, 6, d), lambda b, i: (b, 0, 0)), tok(oa.shape[-1])]
                 + ob_specs + [tok(d), tok(d)]
                 + [const(w.shape) for w in (wpa, wpb, wo, wr, br)],
        out_specs=[tok(d), tok(d), tok(ROUTER_W)],
        out_shape=[jax.ShapeDtypeStruct((bsz, seq, d), F32),
                   jax.ShapeDtypeStruct((bsz, seq, d), BF16),
                   jax.ShapeDtypeStruct((bsz, seq, ROUTER_W), F32)],
        scratch_shapes=[pltpu.VMEM((B_GROUP_W // LANES, tm, LANES), F32)] * 4,
        compiler_params=pltpu.CompilerParams(dimension_semantics=("parallel", "parallel")),
        name="merge_proj",
    )(x, mod_l, oa, *ob_args, ga, gb, wpa, wpb, wo, wr, br)


MOE_BLOCK = 1024
SEG_ALIGN = 16
ROW_CHUNK = 128
PERM_CHUNK = 256
EXPERTS_PER_STEP = 4
SORTED_ROWS = -(-(2 * MOE_BLOCK + N_EXPERTS * (SEG_ALIGN - 1)) // PERM_CHUNK) * PERM_CHUNK
SORTED_ROWS_ALLOC = SORTED_ROWS + ROW_CHUNK


def _plan_kernel(route_ref, before_ref, posc_ref, posr_ref, base_ref, npad_ref):
    r = route_ref[0]
    lane = lax.broadcasted_iota(jnp.int32, r.shape, 1).astype(F32)
    oh1 = jnp.where(lane == r[:, 0:1], 1.0, 0.0)
    oh2 = jnp.where(lane == r[:, 1:2], 1.0, 0.0)
    cnt1 = jnp.sum(oh1, axis=0, keepdims=True)
    cnt2 = jnp.sum(oh2, axis=0, keepdims=True)
    npad = jnp.floor((cnt1 + cnt2 + (SEG_ALIGN - 1)) * (1.0 / SEG_ALIGN)) * SEG_ALIGN
    ri = lax.broadcasted_iota(jnp.int32, (LANES, LANES), 0)
    ci = lax.broadcasted_iota(jnp.int32, (LANES, LANES), 1)
    upper = jnp.where(ri < ci, 1.0, 0.0)
    npad8 = jnp.broadcast_to(npad, (8, LANES))
    base = jnp.dot(npad8, upper, precision=HIGHEST, preferred_element_type=F32)[0:1]

    pre = _dot(before_ref[...], jnp.concatenate([oh1, oh2], axis=1).astype(BF16))
    pre1, pre2 = pre[:, :LANES], pre[:, LANES:]
    pos1 = jnp.sum(oh1 * (base + pre1), axis=-1, keepdims=True)
    pos2 = jnp.sum(oh2 * (base + cnt1 + pre2), axis=-1, keepdims=True)
    packed = jnp.where(lane == 0.0, pos1, jnp.where(lane == 1.0, pos2, jnp.where(lane >= 2.0, r, 0.0)))
    posc_ref[0] = packed
    posr_ref[0] = packed.T[0:8, :]
    base_ref[0] = base
    npad_ref[0] = npad


def _moe_plan(route):
    n_blk, tb, _ = route.shape
    vec = pl.BlockSpec((1, 1, LANES), lambda i: (i, 0, 0))
    before = jnp.tril(jnp.ones((tb, tb), BF16), -1)
    return pl.pallas_call(
        _plan_kernel,
        grid=(n_blk,),
        in_specs=[pl.BlockSpec((1, tb, ROUTER_W), lambda i: (i, 0, 0)),
                  pl.BlockSpec((tb, tb), lambda i: (0, 0), pipeline_mode=pl.Buffered(1))],
        out_specs=[pl.BlockSpec((1, tb, LANES), lambda i: (i, 0, 0)),
                   pl.BlockSpec((1, 8, tb), lambda i: (i, 0, 0)), vec, vec],
        out_shape=[jax.ShapeDtypeStruct((n_blk, tb, LANES), F32),
                   jax.ShapeDtypeStruct((n_blk, 8, tb), F32),
                   jax.ShapeDtypeStruct((n_blk, 1, LANES), F32),
                   jax.ShapeDtypeStruct((n_blk, 1, LANES), F32)],
        compiler_params=pltpu.CompilerParams(dimension_semantics=("parallel",)),
        name="moe_plan",
    )(route, before)


def _moe_kernel(base_ref, npad_ref, x1_ref, mod_ref, h_ref, posc_ref, posr_ref, wg_ref, wu_ref, wd_ref,
                o_ref, xs_ref, ys_ref, gathered_ref):
    blk = pl.program_id(0)
    e = pl.program_id(1)
    tb = h_ref.shape[1]

    n_here = wg_ref.shape[0]
    n_experts = pl.num_programs(1) * n_here

    @pl.when(e == 0)
    def _():
        gathered_ref[0] = 0
        xs_ref[SORTED_ROWS:, :] = jnp.zeros((ROW_CHUNK, xs_ref.shape[1]), BF16)
        ys_ref[...] = jnp.zeros(ys_ref.shape, BF16)

    nxt = jnp.minimum((e + 2) * n_here, n_experts)
    need_rows = jnp.minimum(base_ref[blk, nxt] + ROW_CHUNK, SORTED_ROWS)
    need_chunks = (need_rows + PERM_CHUNK - 1) // PERM_CHUNK

    def gather_chunk(c, carry):
        r0 = pl.multiple_of(c * PERM_CHUNK, PERM_CHUNK)
        rid = (r0 + lax.broadcasted_iota(jnp.int32, (PERM_CHUNK, tb), 0)).astype(F32)
        pr = posr_ref[0]
        sel = jnp.where(rid == pr[0:1], 1.0, jnp.where(rid == pr[1:2], 1.0, 0.0)).astype(BF16)
        xs_ref[pl.ds(r0, PERM_CHUNK), :] = _dot(sel, h_ref[0]).astype(BF16)
        return carry

    lax.fori_loop(gathered_ref[0], need_chunks, gather_chunk, 0)
    gathered_ref[0] = jnp.maximum(gathered_ref[0], need_chunks)

    def gate_up(r0, j):
        xc = xs_ref[pl.ds(r0, ROW_CHUNK), :]
        return _dot(xc, wg_ref[j]), _dot(xc, wu_ref[j])

    def down(r0, j, a, u):
        hid = (a * _sigmoid(a)) * u
        ys_ref[pl.ds(r0, ROW_CHUNK), :] = _dot(hid.astype(BF16), wd_ref[j]).astype(BF16)

    starts = [pl.multiple_of(base_ref[blk, e * n_here + j], SEG_ALIGN) for j in range(n_here)]
    first = [gate_up(starts[j], j) for j in range(n_here)]
    for j in range(n_here):
        down(starts[j], j, *first[j])
    for j in range(n_here):
        seg_rows = npad_ref[blk, e * n_here + j]

        def more(c, carry, j=j, seg_rows=seg_rows):
            r0 = starts[j] + jnp.minimum(c * ROW_CHUNK, seg_rows - ROW_CHUNK)
            r0 = pl.multiple_of(r0, SEG_ALIGN)
            down(r0, j, *gate_up(r0, j))
            return carry

        lax.fori_loop(1, (seg_rows + ROW_CHUNK - 1) // ROW_CHUNK, more, 0)

    @pl.when(e == pl.num_programs(1) - 1)
    def _():
        ys = ys_ref[...]
        for c in range(tb // PERM_CHUNK):
            pc = posc_ref[0, c * PERM_CHUNK:(c + 1) * PERM_CHUNK, :]
            rid = lax.broadcasted_iota(jnp.int32, (PERM_CHUNK, SORTED_ROWS_ALLOC), 1).astype(F32)
            wsel = (jnp.where(rid == pc[:, 0:1], pc[:, 2:3], 0.0)
                    + jnp.where(rid == pc[:, 1:2], pc[:, 3:4], 0.0)).astype(BF16)
            y = _dot(wsel, ys)
            rows = slice(c * PERM_CHUNK, (c + 1) * PERM_CHUNK)
            o_ref[0, rows, :] = x1_ref[0, rows, :] + mod_ref[0, 5:6, :] * y


def _moe(x1, mod_l, h2, route, weg, weu, wed):
    bsz, seq, d = x1.shape
    n_e, _, d_e = weg.shape
    tb = min(MOE_BLOCK, seq)
    assert tb == MOE_BLOCK, "sorted-row capacity is sized for MOE_BLOCK tokens"
    per_batch = seq // tb
    n_blk = bsz * per_batch
    blocked = lambda a: a.reshape(n_blk, tb, a.shape[-1])
    posc, posr, base, npad = _moe_plan(blocked(route))
    base_i = base.reshape(n_blk, LANES).astype(jnp.int32)
    npad_i = npad.reshape(n_blk, LANES).astype(jnp.int32)

    tok = lambda w: pl.BlockSpec((1, tb, w), lambda i, e, *_: (i, 0, 0))
    grid_spec = pltpu.PrefetchScalarGridSpec(
        num_scalar_prefetch=2,
        grid=(n_blk, n_e // EXPERTS_PER_STEP),
        in_specs=[tok(d),
                  pl.BlockSpec((1, 6, d), lambda i, e, *_: (i // per_batch, 0, 0)),
                  tok(d), tok(LANES),
                  pl.BlockSpec((1, 8, tb), lambda i, e, *_: (i, 0, 0)),
                  pl.BlockSpec((EXPERTS_PER_STEP, d, d_e), lambda i, e, *_: (e, 0, 0)),
                  pl.BlockSpec((EXPERTS_PER_STEP, d, d_e), lambda i, e, *_: (e, 0, 0)),
                  pl.BlockSpec((EXPERTS_PER_STEP, d_e, d), lambda i, e, *_: (e, 0, 0))],
        out_specs=tok(d),
        scratch_shapes=[pltpu.VMEM((SORTED_ROWS_ALLOC, d), BF16),
                        pltpu.VMEM((SORTED_ROWS_ALLOC, d), BF16),
                        pltpu.SMEM((1,), jnp.int32)],
    )
    out = pl.pallas_call(
        _moe_kernel,
        grid_spec=grid_spec,
        out_shape=jax.ShapeDtypeStruct((n_blk, tb, d), F32),
        compiler_params=pltpu.CompilerParams(dimension_semantics=("parallel", "arbitrary")),
        name="moe_experts",
    )(base_i, npad_i, blocked(x1), mod_l, blocked(h2), posc, posr, weg, weu, wed)
    return out.reshape(bsz, seq, d)


def _tiles(seq):
    return dict(
        ts_rope=min(1024, seq),
        tm_proj=min(512, seq),
        tq_a=min(1024, seq),
        tk_a=min(1024, seq),
        tq_b=256,
        tm_merge=min(512, seq),
    )


def kernel(x, c, positions, w_ada, b_ada, w_in, qn_a, kn_a, lam_q1, lam_k1, lam_q2, lam_k2,
           subln_a, qn_b, kn_b, w_pa, w_pb, w_o, w_r1, b_r1, w_r2, b_r2,
           w_e_gate, w_e_up, w_e_down):
    depth = w_ada.shape[0]
    bsz, seq, d = x.shape
    t = _tiles(seq)

    mod = _ada(c, w_ada, b_ada).reshape(depth, bsz, 6, d)
    cos_l, sin_l = _rope_tables(positions, t["ts_rope"])
    seg = jnp.kron(jnp.eye(256 // HEAD_DIM, dtype=F32),
                   jnp.full((HEAD_DIM, HEAD_DIM), 1.0 / HEAD_DIM, F32)).astype(BF16)
    q_scale = HEAD_DIM ** -0.5 * LOG2E

    for layer in range(depth):
        lam_init = 0.8 - 0.6 * math.exp(-0.3 * layer)
        w = w_in[layer].astype(BF16)
        wvat = w[:, IN_COL_VA:IN_COL_VA + 512].T
        gain = jnp.concatenate([
            jnp.tile(qn_a[layer] * q_scale, 8), jnp.tile(kn_a[layer], 8),
            jnp.tile(qn_b[layer] * q_scale, 12), jnp.tile(kn_b[layer], 12)]).reshape(1, -1)

        qa, ka, qb, kb, vat, vb, ga, gb = _inproj(x, mod[layer], cos_l, sin_l, gain, w, wvat, seg,
                                                  t["tm_proj"])

        score_bound = (1.01 * HEAD_DIM * q_scale * jnp.max(jnp.abs(qn_a[layer]))
                       * jnp.max(jnp.abs(kn_a[layer]))).reshape(1)
        lam_p = jnp.stack([lam_q1[layer], lam_k1[layer], lam_q2[layer], lam_k2[layer]])
        oa = _attn_a(score_bound, lam_p, subln_a[layer].reshape(1, -1), qa, ka, vat, lam_init,
                     t["tq_a"], t["tk_a"])
        obs = [_attn_b(qb[g], kb[g], vb[g], g, t["tq_b"]) for g in range(B_GROUPS)]

        wr = jnp.zeros((d, ROUTER_W), F32)
        wr = wr.at[:, :N_EXPERTS].set(w_r2[layer]).at[:, N_EXPERTS:N_EXPERTS + N_GROUPS].set(w_r1[layer])
        br = jnp.zeros((1, ROUTER_W), F32)
        br = br.at[0, :N_EXPERTS].set(b_r2[layer]).at[0, N_EXPERTS:N_EXPERTS + N_GROUPS].set(b_r1[layer])
        wr_hi = wr.astype(BF16)
        wr2 = jnp.concatenate([wr_hi, (wr - wr_hi.astype(F32)).astype(BF16)], axis=1)
        x1, h2, route = _merge(x, mod[layer], oa, obs, ga, gb,
                               w_pa[layer].astype(BF16), w_pb[layer].astype(BF16), w_o[layer].astype(BF16),
                               wr2, br, t["tm_merge"])

        x = _moe(x1, mod[layer], h2, route,
                 w_e_gate[layer].astype(BF16), w_e_up[layer].astype(BF16), w_e_down[layer].astype(BF16))
    return x
```

```python
import functools
import math

import jax
import jax.numpy as jnp
from jax import lax
from jax.experimental import pallas as pl
from jax.experimental.pallas import tpu as pltpu

EPS = 1e-6
ROPE_THETA = 10000.0
LOG2E = math.log2(math.e)
LN2 = math.log(2.0)

A_HEADS = 4
HEAD_DIM = 64
LANES = 128
B_PAIRS = ((128, 1), (512, 4), (2048, 16))
B_GROUPS = len(B_PAIRS)
B_GROUP_W = 256
N_GROUPS = 4
EXPERTS_PER_GROUP = 8
N_EXPERTS = N_GROUPS * EXPERTS_PER_GROUP
ROUTER_W = 128

IN_COL_VA = 1024
IN_COLS_QK = (0, 256, 512, 768, 1536, 1792, 2048, 2304, 2560, 2816)
IN_COL_VB = 3072
IN_COL_GATES = 3840

F32 = jnp.float32
BF16 = jnp.bfloat16
HIGHEST = lax.Precision.HIGHEST


def _dot(a, b):
    return jnp.dot(a, b, preferred_element_type=F32)


def _dot_nt(a, b):
    return lax.dot_general(a, b, (((1,), (1,)), ((), ())), preferred_element_type=F32)


def _sigmoid(x):
    return 1.0 / (1.0 + jnp.exp(-x))


def _ada_kernel(c_ref, w_ref, b_ref, o_ref):
    w = w_ref[0]
    for b in range(c_ref.shape[0]):
        c = c_ref[b]
        c_act = c * _sigmoid(c)
        o_ref[0, b:b + 1, :] = jnp.sum(c_act * w, axis=0, keepdims=True) + b_ref[0]


def _ada(c, w_ada, b_ada):
    depth, d, six_d = w_ada.shape
    bsz = c.shape[0]
    n_col = six_d // d
    return pl.pallas_call(
        _ada_kernel,
        grid=(depth, n_col),
        in_specs=[
            pl.BlockSpec((bsz, d, 1), lambda l, j: (0, 0, 0)),
            pl.BlockSpec((1, d, d), lambda l, j: (l, 0, j)),
            pl.BlockSpec((1, 1, d), lambda l, j: (l, 0, j)),
        ],
        out_specs=pl.BlockSpec((1, bsz, d), lambda l, j: (l, 0, j)),
        out_shape=jax.ShapeDtypeStruct((depth, bsz, six_d), F32),
        name="ada_mod",
    )(c.reshape(bsz, d, 1), w_ada, b_ada.reshape(depth, 1, six_d))


def _rope_kernel(pos_ref, f_ref, cos_ref, sin_ref):
    ang = pos_ref[0].astype(F32) * f_ref[...]
    cos = jnp.cos(ang)
    sin = jnp.sin(ang)
    cos_ref[0] = jnp.concatenate([cos, cos, cos, cos], axis=0).T
    sin_ref[0] = jnp.concatenate([-sin, sin, -sin, sin], axis=0).T


def _rope_tables(positions, ts):
    bsz, seq = positions.shape
    half = HEAD_DIM // 2
    inv_freq = ROPE_THETA ** (-jnp.arange(0, HEAD_DIM, 2, dtype=F32) / HEAD_DIM)
    return pl.pallas_call(
        _rope_kernel,
        grid=(bsz, seq // ts),
        in_specs=[
            pl.BlockSpec((1, 1, ts), lambda b, i: (b, 0, i)),
            pl.BlockSpec((half, 1), lambda b, i: (0, 0)),
        ],
        out_specs=[pl.BlockSpec((1, ts, LANES), lambda b, i: (b, i, 0))] * 2,
        out_shape=[jax.ShapeDtypeStruct((bsz, seq, LANES), F32)] * 2,
        name="rope_tables",
    )(positions.reshape(bsz, 1, seq), inv_freq.reshape(half, 1))


def _inproj_kernel(x_ref, mod_ref, cos_ref, sin_ref, gain_ref, w_ref, wvat_ref, seg_ref,
                   qa_ref, ka_ref, qb0_ref, qb1_ref, qb2_ref, kb0_ref, kb1_ref, kb2_ref,
                   vat_ref, vb0_ref, vb1_ref, vb2_ref, ga_ref, gb_ref, stage_ref):
    x = x_ref[0]
    tm = x.shape[0]
    ms = jnp.mean(x * x, axis=-1, keepdims=True)
    h = x * lax.rsqrt(ms + EPS) * (1.0 + mod_ref[0, 1:2, :]) + mod_ref[0, 0:1, :]
    hb = h.astype(BF16)

    cos = cos_ref[0]
    sin = sin_ref[0]
    seg = seg_ref[...]
    lane = lax.broadcasted_iota(jnp.int32, cos.shape, 1)
    first_half = (lane % HEAD_DIM) < (HEAD_DIM // 2)

    def norm_rope(y2, c):
        msq = _dot((y2 * y2).astype(BF16), seg)
        yn2 = y2 * lax.rsqrt(msq + EPS) * gain_ref[:, c * 256:(c + 1) * 256]
        out = []
        for hf in range(2):
            yn = yn2[:, hf * LANES:(hf + 1) * LANES]
            partner = jnp.where(first_half, pltpu.roll(yn, 96, axis=1), pltpu.roll(yn, 32, axis=1))
            out.append(yn * cos + partner * sin)
        return out

    def store_dilated(ref, group, val):
        dil = B_PAIRS[group][1]
        if dil == 1:
            ref[0] = val.astype(BF16)
            return
        for hf in range(B_GROUP_W // LANES):
            stage_ref[hf] = val[:, hf * LANES:(hf + 1) * LANES]
        for r in range(dil):
            for hf in range(B_GROUP_W // LANES):
                rows = stage_ref[hf, pl.ds(r, tm // dil, stride=dil), :]
                col = r * B_GROUP_W + hf * LANES
                ref[0, :, col:col + LANES] = rows.astype(BF16)

    flat = [(qa_ref, 0), (qa_ref, 1), (ka_ref, 0), (ka_ref, 1)]
    grouped = [(qb0_ref, 0), (qb1_ref, 1), (qb2_ref, 2), (kb0_ref, 0), (kb1_ref, 1), (kb2_ref, 2)]
    def qk_epilogue(c, y2):
        halves = norm_rope(y2, c)
        if c < len(flat):
            ref, t = flat[c]
            for hf in range(2):
                ref[0, :, t * 256 + hf * LANES:t * 256 + (hf + 1) * LANES] = halves[hf].astype(BF16)
        else:
            ref, group = grouped[c - len(flat)]
            store_dilated(ref, group, jnp.concatenate(halves, axis=1))

    def vat_epilogue(c, v):
        vat_ref[0, c * 256:(c + 1) * 256, :] = v.astype(BF16)

    def gate_epilogue(c, g):
        ref = ga_ref if c < 4 else gb_ref
        ref[0, :, (c % 4) * 256:(c % 4 + 1) * 256] = _sigmoid(g).astype(BF16)

    proj = lambda col: _dot(hb, w_ref[0, :, col:col + 256])
    jobs = []
    for c, col in enumerate(IN_COLS_QK):
        jobs.append((functools.partial(proj, col), functools.partial(qk_epilogue, c)))
    for c in range(wvat_ref.shape[1] // 256):
        jobs.append((lambda c=c: _dot_nt(wvat_ref[0, c * 256:(c + 1) * 256, :], hb),
                     functools.partial(vat_epilogue, c)))
    for group, ref in enumerate((vb0_ref, vb1_ref, vb2_ref)):
        jobs.append((functools.partial(proj, IN_COL_VB + group * 256),
                     functools.partial(store_dilated, ref, group)))
    for c in range((w_ref.shape[2] - IN_COL_GATES) // 256):
        jobs.append((functools.partial(proj, IN_COL_GATES + c * 256), functools.partial(gate_epilogue, c)))
    pending = None
    for matmul, epilogue in jobs:
        res = matmul()
        if pending is not None:
            pending[0](pending[1])
        pending = (epilogue, res)
    pending[0](pending[1])


def _inproj(x, mod_l, cos_l, sin_l, gain, w_all, wvat_all, seg, layer, tm):
    bsz, seq, d = x.shape
    a_v_w = wvat_all.shape[1]
    tok = lambda w: pl.BlockSpec((1, tm, w), lambda b, i: (b, i, 0))
    tok_t = pl.BlockSpec((1, a_v_w, tm), lambda b, i: (b, 0, i))
    const = lambda shape: pl.BlockSpec(shape, lambda b, i: (0,) * len(shape),
                                       pipeline_mode=pl.Buffered(1))
    of_layer = lambda a: pl.BlockSpec((1,) + a.shape[1:], lambda b, i: (layer, 0, 0),
                                      pipeline_mode=pl.Buffered(1))
    row = lambda w: jax.ShapeDtypeStruct((bsz, seq, w), BF16)
    dil_specs = [pl.BlockSpec((1, tm // dl, dl * B_GROUP_W), lambda b, i: (b, i, 0)) for _, dl in B_PAIRS]
    dil_shapes = [jax.ShapeDtypeStruct((bsz, seq // dl, dl * B_GROUP_W), BF16) for _, dl in B_PAIRS]
    outs = pl.pallas_call(
        _inproj_kernel,
        grid=(bsz, seq // tm),
        in_specs=[
            tok(d),
            pl.BlockSpec((1, 6, d), lambda b, i: (b, 0, 0)),
            tok(LANES), tok(LANES),
            const(gain.shape), of_layer(w_all), of_layer(wvat_all), const(seg.shape),
        ],
        out_specs=[tok(512), tok(512)] + dil_specs + dil_specs + [tok_t] + dil_specs + [tok(d), tok(d)],
        out_shape=[row(512), row(512)] + dil_shapes + dil_shapes
                  + [jax.ShapeDtypeStruct((bsz, a_v_w, seq), BF16)] + dil_shapes + [row(d), row(d)],
        scratch_shapes=[pltpu.VMEM((B_GROUP_W // LANES, tm, LANES), F32)],
        compiler_params=pltpu.CompilerParams(dimension_semantics=("parallel", "parallel")),
        name="in_proj",
    )(x, mod_l, cos_l, sin_l, gain, w_all, wvat_all, seg)
    qa, ka = outs[0], outs[1]
    qb, kb, vat, vb, ga, gb = outs[2:5], outs[5:8], outs[8], outs[9:12], outs[12], outs[13]
    return qa, ka, qb, kb, vat, vb, ga, gb


SCORE_BOUND_NO_SHIFT = 64.0
KV_CHUNK = 256


def _attn_a_kernel(bound_ref, lam_ref, sub_ref, q_ref, k_ref, vt_ref, o_ref,
                   m_ref, l_ref, acc_ref, pa_ref, pb_ref, *, tk, lam_init):
    q = q_ref[0]
    tq = q.shape[0]
    seq = k_ref.shape[1]
    lane = lax.broadcasted_iota(jnp.int32, q.shape, 1)
    zero = jnp.zeros_like(q)
    q_maps = (jnp.where(lane < HEAD_DIM, q, zero), jnp.where(lane >= HEAD_DIM, q, zero))
    n_chunk = tk // KV_CHUNK

    acc_ref[...] = jnp.zeros(acc_ref.shape, F32)
    l_ref[...] = jnp.zeros(l_ref.shape, F32)

    def scores_exp(tile, p_ref):
        start = pl.multiple_of(tile * tk, tk)
        k = k_ref[0, pl.ds(start, tk), :]
        for mi in range(2):
            p = jnp.exp2(_dot_nt(k, q_maps[mi]))
            l_ref[mi] += jnp.sum(p.reshape(tk // 8, 8, tq), axis=0)
            p_ref[mi] = p.astype(BF16)

    def weighted_values(tile, p_ref):
        start = pl.multiple_of(tile * tk, tk)
        vt = vt_ref[0, :, pl.ds(start, tk)]
        for mi in range(2):
            acc_ref[mi] += _dot(vt, p_ref[mi])

    n_tiles = seq // tk

    def no_shift_pair(jj, carry):
        t = 2 * jj
        scores_exp(t + 1, pb_ref)
        weighted_values(t, pa_ref)
        scores_exp(t + 2, pa_ref)
        weighted_values(t + 1, pb_ref)
        return carry

    def no_shift_loop():
        scores_exp(0, pa_ref)
        lax.fori_loop(0, n_tiles // 2 - 1, no_shift_pair, 0)
        scores_exp(n_tiles - 1, pb_ref)
        weighted_values(n_tiles - 2, pa_ref)
        weighted_values(n_tiles - 1, pb_ref)

    def online_max_body(j, carry):
        for c in range(n_chunk):
            start = pl.multiple_of(j * tk + c * KV_CHUNK, KV_CHUNK)
            k = k_ref[0, pl.ds(start, KV_CHUNK), :]
            vt = vt_ref[0, :, pl.ds(start, KV_CHUNK)]
            for mi in range(2):
                s = _dot_nt(k, q_maps[mi])
                m_old = m_ref[mi]
                m_new = jnp.maximum(m_old, jnp.max(s, axis=0, keepdims=True))
                alpha = jnp.exp2(m_old - m_new)
                p = jnp.exp2(s - m_new[0:1])
                l_ref[mi] = alpha * l_ref[mi] + jnp.sum(p.reshape(KV_CHUNK // 8, 8, tq), axis=0)
                acc_ref[mi] = alpha[0:1] * acc_ref[mi] + _dot(vt, p.astype(BF16))
                m_ref[mi] = m_new
        return carry

    no_shift = bound_ref[0] <= SCORE_BOUND_NO_SHIFT

    @pl.when(no_shift)
    def _():
        no_shift_loop()

    @pl.when(jnp.logical_not(no_shift))
    def _():
        m_ref[...] = jnp.full(m_ref.shape, -jnp.inf, F32)
        lax.fori_loop(0, seq // tk, online_max_body, 0)

    lam_p = lam_ref[...]
    s1 = jnp.sum(lam_p[0:1] * lam_p[1:2], axis=-1, keepdims=True)
    s2 = jnp.sum(lam_p[2:3] * lam_p[3:4], axis=-1, keepdims=True)
    lam = jnp.exp(s1) - jnp.exp(s2) + lam_init
    l0 = jnp.sum(l_ref[0], axis=0, keepdims=True)
    l1 = jnp.sum(l_ref[1], axis=0, keepdims=True)
    ot = acc_ref[0] / l0 - lam * (acc_ref[1] / l1)
    o = ot.T
    msq = jnp.mean(o * o, axis=-1, keepdims=True)
    o = o * lax.rsqrt(msq + EPS) * sub_ref[...] * (1.0 - lam_init)
    o_ref[0] = o.astype(BF16)


def _attn_a(score_bound, lam_p, subln, qa, ka, vat, lam_init, tq, tk):
    bsz, seq, _ = qa.shape
    return pl.pallas_call(
        functools.partial(_attn_a_kernel, tk=tk, lam_init=lam_init),
        grid=(bsz, A_HEADS, seq // tq),
        in_specs=[
            pl.BlockSpec(memory_space=pltpu.SMEM),
            pl.BlockSpec(lam_p.shape, lambda b, h, i: (0, 0)),
            pl.BlockSpec(subln.shape, lambda b, h, i: (0, 0)),
            pl.BlockSpec((1, tq, LANES), lambda b, h, i: (b, i, h)),
            pl.BlockSpec((1, seq, LANES), lambda b, h, i: (b, 0, h)),
            pl.BlockSpec((1, LANES, seq), lambda b, h, i: (b, h, 0)),
        ],
        out_specs=pl.BlockSpec((1, tq, LANES), lambda b, h, i: (b, i, h)),
        out_shape=jax.ShapeDtypeStruct((bsz, seq, A_HEADS * LANES), BF16),
        scratch_shapes=[
            pltpu.VMEM((2, 8, tq), F32),
            pltpu.VMEM((2, 8, tq), F32),
            pltpu.VMEM((2, LANES, tq), F32),
            pltpu.VMEM((2, tk, tq), BF16),
            pltpu.VMEM((2, tk, tq), BF16),
        ],
        compiler_params=pltpu.CompilerParams(dimension_semantics=("parallel", "parallel", "parallel")),
        name="diff_attn",
    )(score_bound, lam_p, subln, qa, ka, vat)


def _attn_b_kernel(q_ref, k_ref, v_ref, o_ref, lse_ref, *, radius):
    tq = q_ref.shape[1]
    length = k_ref.shape[1]
    win = tq + 2 * radius
    t0 = pl.program_id(2) * tq
    start = pl.multiple_of(jnp.clip(t0 - radius, 0, length - win), radius)
    kw = k_ref[0, pl.ds(start, win), :]
    vw = v_ref[0, pl.ds(start, win), :]
    q = q_ref[0]

    qpos = t0 + lax.broadcasted_iota(jnp.int32, (tq, win), 0)
    kpos = start + lax.broadcasted_iota(jnp.int32, (tq, win), 1)
    valid = jnp.abs(kpos - qpos) <= radius
    lane = lax.broadcasted_iota(jnp.int32, (tq, LANES), 1)
    low = lane < HEAD_DIM

    heads = [(c, half) for c in range(B_GROUP_W // LANES) for half in range(2)]
    tile = lambda a, c: a[:, c * LANES:(c + 1) * LANES]
    scores = []
    for c, half in heads:
        qc = tile(q, c)
        qm = jnp.where(low if half == 0 else jnp.logical_not(low), qc, jnp.zeros_like(qc))
        scores.append(jnp.where(valid, _dot_nt(qm, tile(kw, c)), -jnp.inf))
    probs, sums, lses = [], [], []
    for s in scores:
        m = jnp.max(s, axis=-1, keepdims=True)
        p = jnp.exp2(s - m)
        l = jnp.sum(p, axis=-1, keepdims=True)
        probs.append(p.astype(BF16))
        sums.append(l)
        lses.append((m + jnp.log2(l)) * LN2)
    outs = [_dot(probs[i], tile(vw, c)) / sums[i] for i, (c, _) in enumerate(heads)]
    for c in range(B_GROUP_W // LANES):
        o_ref[0, :, c * LANES:(c + 1) * LANES] = jnp.where(low, outs[2 * c], outs[2 * c + 1]).astype(BF16)
        lse_ref[0, :, c * LANES:(c + 1) * LANES] = jnp.where(low, lses[2 * c], lses[2 * c + 1])


def _attn_b(qg, kg, vg, group, tq):
    window, dilation = B_PAIRS[group]
    radius = window // (2 * dilation)
    bsz, length, _ = qg.shape
    tq = min(tq, length - 2 * radius)
    return pl.pallas_call(
        functools.partial(_attn_b_kernel, radius=radius),
        grid=(bsz, dilation, length // tq),
        in_specs=[
            pl.BlockSpec((1, tq, B_GROUP_W), lambda b, r, i: (b, i, r)),
            pl.BlockSpec((1, length, B_GROUP_W), lambda b, r, i: (b, 0, r)),
            pl.BlockSpec((1, length, B_GROUP_W), lambda b, r, i: (b, 0, r)),
        ],
        out_specs=[pl.BlockSpec((1, tq, B_GROUP_W), lambda b, r, i: (b, i, r))] * 2,
        out_shape=[jax.ShapeDtypeStruct((bsz, length, dilation * B_GROUP_W), BF16),
                   jax.ShapeDtypeStruct((bsz, length, dilation * B_GROUP_W), F32)],
        compiler_params=pltpu.CompilerParams(dimension_semantics=("parallel", "parallel", "parallel")),
        name=f"band_attn_g{group}",
    )(qg, kg, vg)


def _merge_kernel(x_ref, mod_ref, oa_ref, ob0_ref, ls0_ref, ob1_ref, ls1_ref, ob2_ref, ls2_ref,
                  ga_ref, gb_ref, wpa_ref, wpb_ref, wo_ref, wr2_ref, br_ref,
                  x1_ref, h2_ref, route_ref, so1_ref, sl1_ref, so2_ref, sl2_ref):
    tm = x_ref.shape[1]

    def token_major(o_ref, l_ref, group, so_ref, sl_ref):
        dil = B_PAIRS[group][1]
        n_hf = B_GROUP_W // LANES
        for r in range(dil):
            for hf in range(n_hf):
                cols = slice(r * B_GROUP_W + hf * LANES, r * B_GROUP_W + (hf + 1) * LANES)
                so_ref[hf, pl.ds(r, tm // dil, stride=dil), :] = o_ref[0, :, cols].astype(F32)
                sl_ref[hf, pl.ds(r, tm // dil, stride=dil), :] = l_ref[0, :, cols]
        return (jnp.concatenate([so_ref[hf] for hf in range(n_hf)], axis=1),
                jnp.concatenate([sl_ref[hf] for hf in range(n_hf)], axis=1))

    o0, ls0 = ob0_ref[0].astype(F32), ls0_ref[0]
    o1, ls1 = token_major(ob1_ref, ls1_ref, 1, so1_ref, sl1_ref)
    o2, ls2 = token_major(ob2_ref, ls2_ref, 2, so2_ref, sl2_ref)
    mx = jnp.maximum(jnp.maximum(ls0, ls1), ls2)
    e0, e1, e2 = jnp.exp(ls0 - mx), jnp.exp(ls1 - mx), jnp.exp(ls2 - mx)
    ob = ((e0 * o0 + e1 * o1 + e2 * o2) / (e0 + e1 + e2)).astype(BF16)

    n_part = 2
    rows = [slice(p * (tm // n_part), (p + 1) * (tm // n_part)) for p in range(n_part)]
    pa = [_dot(oa_ref[0, r, :], wpa_ref[...]) for r in rows]
    pb = [_dot(ob[r], wpb_ref[...]) for r in rows]
    merged = [(ga_ref[0, r, :].astype(F32) * pa[p] + gb_ref[0, r, :].astype(F32) * pb[p]).astype(BF16)
              for p, r in enumerate(rows)]
    y = [_dot(merged[p], wo_ref[...]) for p in range(n_part)]
    for p, r in enumerate(rows):
        x1 = x_ref[0, r, :] + mod_ref[0, 2:3, :] * y[p]
        x1_ref[0, r, :] = x1
        ms = jnp.mean(x1 * x1, axis=-1, keepdims=True)
        h2 = x1 * lax.rsqrt(ms + EPS) * (1.0 + mod_ref[0, 4:5, :]) + mod_ref[0, 3:4, :]
        h2_hi = h2.astype(BF16)
        h2_ref[0, r, :] = h2_hi
        h2_lo = (h2 - h2_hi.astype(F32)).astype(BF16)
        both = _dot(h2_hi, wr2_ref[...])
        logits = (both[:, :ROUTER_W] + both[:, ROUTER_W:] + _dot(h2_lo, wr2_ref[:, :ROUTER_W])) + br_ref[...]
        route_ref[0, r, :] = _route(logits)


def _route(logits):
    lane = lax.broadcasted_iota(jnp.int32, logits.shape, 1)
    neg = -jnp.inf
    big = ROUTER_W
    is_grp = (lane >= N_EXPERTS) & (lane < N_EXPERTS + N_GROUPS)
    lg = jnp.where(is_grp, logits, neg)
    mg = jnp.max(lg, axis=-1, keepdims=True)
    g_lane = jnp.min(jnp.where(lg == mg, lane, big), axis=-1, keepdims=True)
    g_val = 1.0 / jnp.sum(jnp.exp(lg - mg), axis=-1, keepdims=True)
    lo = (g_lane - N_EXPERTS) * EXPERTS_PER_GROUP
    in_grp = (lane >= lo) & (lane < lo + EXPERTS_PER_GROUP)
    le = jnp.where(in_grp, logits, neg)
    m1 = jnp.max(le, axis=-1, keepdims=True)
    i1 = jnp.min(jnp.where(le == m1, lane, big), axis=-1, keepdims=True)
    le2 = jnp.where(lane == i1, neg, le)
    m2 = jnp.max(le2, axis=-1, keepdims=True)
    i2 = jnp.min(jnp.where(le2 == m2, lane, big), axis=-1, keepdims=True)
    e = jnp.exp(m2 - m1)
    w1 = g_val / (1.0 + e)
    w2 = g_val * e / (1.0 + e)
    return jnp.where(lane == 0, i1.astype(F32),
                     jnp.where(lane == 1, i2.astype(F32),
                               jnp.where(lane == 2, w1, jnp.where(lane == 3, w2, 0.0))))


def _merge(x, mod_l, oa, obs, ga, gb, wpa, wpb, wo, wr, br, tm):
    bsz, seq, d = x.shape
    tok = lambda w: pl.BlockSpec((1, tm, w), lambda b, i: (b, i, 0))
    const = lambda shape: pl.BlockSpec(shape, lambda b, i: (0,) * len(shape),
                                       pipeline_mode=pl.Buffered(1))
    ob_args, ob_specs = [], []
    for (o, lse), (_, dl) in zip(obs, B_PAIRS):
        ob_args += [o, lse]
        ob_specs += [pl.BlockSpec((1, tm // dl, dl * B_GROUP_W), lambda b, i: (b, i, 0))] * 2
    return pl.pallas_call(
        _merge_kernel,
        grid=(bsz, seq // tm),
        in_specs=[tok(d), pl.BlockSpec((1, 6, d), lambda b, i: (b, 0, 0)), tok(oa.shape[-1])]
                 + ob_specs + [tok(d), tok(d)]
                 + [const(w.shape) for w in (wpa, wpb, wo, wr, br)],
        out_specs=[tok(d), tok(d), tok(ROUTER_W)],
        out_shape=[jax.ShapeDtypeStruct((bsz, seq, d), F32),
                   jax.ShapeDtypeStruct((bsz, seq, d), BF16),
                   jax.ShapeDtypeStruct((bsz, seq, ROUTER_W), F32)],
        scratch_shapes=[pltpu.VMEM((B_GROUP_W // LANES, tm, LANES), F32)] * 4,
        compiler_params=pltpu.CompilerParams(dimension_semantics=("parallel", "parallel")),
        name="merge_proj",
    )(x, mod_l, oa, *ob_args, ga, gb, wpa, wpb, wo, wr, br)


MOE_BLOCK = 1024
SEG_ALIGN = 16
ROW_CHUNK = 128
PERM_CHUNK = 256
EXPERTS_PER_STEP = 4
SORTED_ROWS = -(-(2 * MOE_BLOCK + N_EXPERTS * (SEG_ALIGN - 1)) // PERM_CHUNK) * PERM_CHUNK
SORTED_ROWS_ALLOC = SORTED_ROWS + ROW_CHUNK


def _plan_kernel(route_ref, before_ref, posc_ref, posr_ref, base_ref, npad_ref):
    r = route_ref[0]
    lane = lax.broadcasted_iota(jnp.int32, r.shape, 1).astype(F32)
    oh1 = jnp.where(lane == r[:, 0:1], 1.0, 0.0)
    oh2 = jnp.where(lane == r[:, 1:2], 1.0, 0.0)
    cnt1 = jnp.sum(oh1, axis=0, keepdims=True)
    cnt2 = jnp.sum(oh2, axis=0, keepdims=True)
    npad = jnp.floor((cnt1 + cnt2 + (SEG_ALIGN - 1)) * (1.0 / SEG_ALIGN)) * SEG_ALIGN
    ri = lax.broadcasted_iota(jnp.int32, (LANES, LANES), 0)
    ci = lax.broadcasted_iota(jnp.int32, (LANES, LANES), 1)
    upper = jnp.where(ri < ci, 1.0, 0.0)
    npad8 = jnp.broadcast_to(npad, (8, LANES))
    base = jnp.dot(npad8, upper, precision=HIGHEST, preferred_element_type=F32)[0:1]

    pre = _dot(before_ref[...], jnp.concatenate([oh1, oh2], axis=1).astype(BF16))
    pre1, pre2 = pre[:, :LANES], pre[:, LANES:]
    pos1 = jnp.sum(oh1 * (base + pre1), axis=-1, keepdims=True)
    pos2 = jnp.sum(oh2 * (base + cnt1 + pre2), axis=-1, keepdims=True)
    packed = jnp.where(lane == 0.0, pos1, jnp.where(lane == 1.0, pos2, jnp.where(lane >= 2.0, r, 0.0)))
    posc_ref[0] = packed
    posr_ref[0] = packed.T[0:8, :]
    base_ref[0] = base
    npad_ref[0] = npad


def _moe_plan(route):
    n_blk, tb, _ = route.shape
    vec = pl.BlockSpec((1, 1, LANES), lambda i: (i, 0, 0))
    before = jnp.tril(jnp.ones((tb, tb), BF16), -1)
    return pl.pallas_call(
        _plan_kernel,
        grid=(n_blk,),
        in_specs=[pl.BlockSpec((1, tb, ROUTER_W), lambda i: (i, 0, 0)),
                  pl.BlockSpec((tb, tb), lambda i: (0, 0), pipeline_mode=pl.Buffered(1))],
        out_specs=[pl.BlockSpec((1, tb, LANES), lambda i: (i, 0, 0)),
                   pl.BlockSpec((1, 8, tb), lambda i: (i, 0, 0)), vec, vec],
        out_shape=[jax.ShapeDtypeStruct((n_blk, tb, LANES), F32),
                   jax.ShapeDtypeStruct((n_blk, 8, tb), F32),
                   jax.ShapeDtypeStruct((n_blk, 1, LANES), F32),
                   jax.ShapeDtypeStruct((n_blk, 1, LANES), F32)],
        compiler_params=pltpu.CompilerParams(dimension_semantics=("parallel",)),
        name="moe_plan",
    )(route, before)


def _moe_kernel(base_ref, npad_ref, x1_ref, mod_ref, h_ref, posc_ref, posr_ref, wg_ref, wu_ref, wd_ref,
                o_ref, xs_ref, ys_ref, gathered_ref):
    blk = pl.program_id(0)
    e = pl.program_id(1)
    tb = h_ref.shape[1]

    n_here = wg_ref.shape[0]
    n_experts = pl.num_programs(1) * n_here

    @pl.when(e == 0)
    def _():
        gathered_ref[0] = 0
        xs_ref[SORTED_ROWS:, :] = jnp.zeros((ROW_CHUNK, xs_ref.shape[1]), BF16)
        ys_ref[...] = jnp.zeros(ys_ref.shape, BF16)

    nxt = jnp.minimum((e + 2) * n_here, n_experts)
    need_rows = jnp.minimum(base_ref[blk, nxt] + ROW_CHUNK, SORTED_ROWS)
    need_chunks = (need_rows + PERM_CHUNK - 1) // PERM_CHUNK

    def gather_chunk(c, carry):
        r0 = pl.multiple_of(c * PERM_CHUNK, PERM_CHUNK)
        rid = (r0 + lax.broadcasted_iota(jnp.int32, (PERM_CHUNK, tb), 0)).astype(F32)
        pr = posr_ref[0]
        sel = jnp.where(rid == pr[0:1], 1.0, jnp.where(rid == pr[1:2], 1.0, 0.0)).astype(BF16)
        xs_ref[pl.ds(r0, PERM_CHUNK), :] = _dot(sel, h_ref[0]).astype(BF16)
        return carry

    lax.fori_loop(gathered_ref[0], need_chunks, gather_chunk, 0)
    gathered_ref[0] = jnp.maximum(gathered_ref[0], need_chunks)

    def gate_up(r0, j):
        xc = xs_ref[pl.ds(r0, ROW_CHUNK), :]
        return _dot(xc, wg_ref[j]), _dot(xc, wu_ref[j])

    def down(r0, j, a, u):
        hid = (a * _sigmoid(a)) * u
        ys_ref[pl.ds(r0, ROW_CHUNK), :] = _dot(hid.astype(BF16), wd_ref[j]).astype(BF16)

    starts = [pl.multiple_of(base_ref[blk, e * n_here + j], SEG_ALIGN) for j in range(n_here)]
    first = [gate_up(starts[j], j) for j in range(n_here)]
    for j in range(n_here):
        down(starts[j], j, *first[j])
    for j in range(n_here):
        seg_rows = npad_ref[blk, e * n_here + j]

        def more(c, carry, j=j, seg_rows=seg_rows):
            r0 = starts[j] + jnp.minimum(c * ROW_CHUNK, seg_rows - ROW_CHUNK)
            r0 = pl.multiple_of(r0, SEG_ALIGN)
            down(r0, j, *gate_up(r0, j))
            return carry

        lax.fori_loop(1, (seg_rows + ROW_CHUNK - 1) // ROW_CHUNK, more, 0)

    @pl.when(e == pl.num_programs(1) - 1)
    def _():
        ys = ys_ref[...]
        for c in range(tb // PERM_CHUNK):
            pc = posc_ref[0, c * PERM_CHUNK:(c + 1) * PERM_CHUNK, :]
            rid = lax.broadcasted_iota(jnp.int32, (PERM_CHUNK, SORTED_ROWS_ALLOC), 1).astype(F32)
            wsel = (jnp.where(rid == pc[:, 0:1], pc[:, 2:3], 0.0)
                    + jnp.where(rid == pc[:, 1:2], pc[:, 3:4], 0.0)).astype(BF16)
            y = _dot(wsel, ys)
            rows = slice(c * PERM_CHUNK, (c + 1) * PERM_CHUNK)
            o_ref[0, rows, :] = x1_ref[0, rows, :] + mod_ref[0, 5:6, :] * y


def _moe(x1, mod_l, h2, route, weg, weu, wed, layer):
    bsz, seq, d = x1.shape
    n_e, d_e = N_EXPERTS, weg.shape[-1]
    e0 = layer * (n_e // EXPERTS_PER_STEP)
    tb = min(MOE_BLOCK, seq)
    assert tb == MOE_BLOCK, "sorted-row capacity is sized for MOE_BLOCK tokens"
    per_batch = seq // tb
    n_blk = bsz * per_batch
    blocked = lambda a: a.reshape(n_blk, tb, a.shape[-1])
    posc, posr, base, npad = _moe_plan(blocked(route))
    base_i = base.reshape(n_blk, LANES).astype(jnp.int32)
    npad_i = npad.reshape(n_blk, LANES).astype(jnp.int32)

    tok = lambda w: pl.BlockSpec((1, tb, w), lambda i, e, *_: (i, 0, 0))
    grid_spec = pltpu.PrefetchScalarGridSpec(
        num_scalar_prefetch=2,
        grid=(n_blk, n_e // EXPERTS_PER_STEP),
        in_specs=[tok(d),
                  pl.BlockSpec((1, 6, d), lambda i, e, *_: (i // per_batch, 0, 0)),
                  tok(d), tok(LANES),
                  pl.BlockSpec((1, 8, tb), lambda i, e, *_: (i, 0, 0)),
                  pl.BlockSpec((EXPERTS_PER_STEP, d, d_e), lambda i, e, *_: (e0 + e, 0, 0)),
                  pl.BlockSpec((EXPERTS_PER_STEP, d, d_e), lambda i, e, *_: (e0 + e, 0, 0)),
                  pl.BlockSpec((EXPERTS_PER_STEP, d_e, d), lambda i, e, *_: (e0 + e, 0, 0))],
        out_specs=tok(d),
        scratch_shapes=[pltpu.VMEM((SORTED_ROWS_ALLOC, d), BF16),
                        pltpu.VMEM((SORTED_ROWS_ALLOC, d), BF16),
                        pltpu.SMEM((1,), jnp.int32)],
    )
    out = pl.pallas_call(
        _moe_kernel,
        grid_spec=grid_spec,
        out_shape=jax.ShapeDtypeStruct((n_blk, tb, d), F32),
        compiler_params=pltpu.CompilerParams(dimension_semantics=("parallel", "arbitrary")),
        name="moe_experts",
    )(base_i, npad_i, blocked(x1), mod_l, blocked(h2), posc, posr, weg, weu, wed)
    return out.reshape(bsz, seq, d)


def _tiles(seq):
    return dict(
        ts_rope=min(1024, seq),
        tm_proj=min(512, seq),
        tq_a=min(1024, seq),
        tk_a=min(1024, seq),
        tq_b=256,
        tm_merge=min(512, seq),
    )


def kernel(x, c, positions, w_ada, b_ada, w_in, qn_a, kn_a, lam_q1, lam_k1, lam_q2, lam_k2,
           subln_a, qn_b, kn_b, w_pa, w_pb, w_o, w_r1, b_r1, w_r2, b_r2,
           w_e_gate, w_e_up, w_e_down):
    depth = w_ada.shape[0]
    bsz, seq, d = x.shape
    t = _tiles(seq)

    mod = _ada(c, w_ada, b_ada).reshape(depth, bsz, 6, d)
    cos_l, sin_l = _rope_tables(positions, t["ts_rope"])
    seg = jnp.kron(jnp.eye(256 // HEAD_DIM, dtype=F32),
                   jnp.full((HEAD_DIM, HEAD_DIM), 1.0 / HEAD_DIM, F32)).astype(BF16)
    q_scale = HEAD_DIM ** -0.5 * LOG2E

    w_in_b = w_in.astype(BF16)
    wvat_b = w_in[:, :, IN_COL_VA:IN_COL_VA + 512].transpose(0, 2, 1).astype(BF16)
    experts = lambda a: a.astype(BF16).reshape((-1,) + a.shape[2:])
    weg_b, weu_b, wed_b = experts(w_e_gate), experts(w_e_up), experts(w_e_down)

    for layer in range(depth):
        lam_init = 0.8 - 0.6 * math.exp(-0.3 * layer)
        gain = jnp.concatenate([
            jnp.tile(qn_a[layer] * q_scale, 8), jnp.tile(kn_a[layer], 8),
            jnp.tile(qn_b[layer] * q_scale, 12), jnp.tile(kn_b[layer], 12)]).reshape(1, -1)

        qa, ka, qb, kb, vat, vb, ga, gb = _inproj(x, mod[layer], cos_l, sin_l, gain, w_in_b, wvat_b, seg,
                                                  layer, t["tm_proj"])

        score_bound = (1.01 * HEAD_DIM * q_scale * jnp.max(jnp.abs(qn_a[layer]))
                       * jnp.max(jnp.abs(kn_a[layer]))).reshape(1)
        lam_p = jnp.stack([lam_q1[layer], lam_k1[layer], lam_q2[layer], lam_k2[layer]])
        oa = _attn_a(score_bound, lam_p, subln_a[layer].reshape(1, -1), qa, ka, vat, lam_init,
                     t["tq_a"], t["tk_a"])
        obs = [_attn_b(qb[g], kb[g], vb[g], g, t["tq_b"]) for g in range(B_GROUPS)]

        wr = jnp.zeros((d, ROUTER_W), F32)
        wr = wr.at[:, :N_EXPERTS].set(w_r2[layer]).at[:, N_EXPERTS:N_EXPERTS + N_GROUPS].set(w_r1[layer])
        br = jnp.zeros((1, ROUTER_W), F32)
        br = br.at[0, :N_EXPERTS].set(b_r2[layer]).at[0, N_EXPERTS:N_EXPERTS + N_GROUPS].set(b_r1[layer])
        wr_hi = wr.astype(BF16)
        wr2 = jnp.concatenate([wr_hi, (wr - wr_hi.astype(F32)).astype(BF16)], axis=1)
        x1, h2, route = _merge(x, mod[layer], oa, obs, ga, gb,
                               w_pa[layer].astype(BF16), w_pb[layer].astype(BF16), w_o[layer].astype(BF16),
                               wr2, br, t["tm_merge"])

        x = _moe(x1, mod[layer], h2, route, weg_b, weu_b, wed_b, layer)
    return x
```

```python
import functools
import math

import jax
import jax.numpy as jnp
from jax import lax
from jax.experimental import pallas as pl
from jax.experimental.pallas import tpu as pltpu

EPS = 1e-6
ROPE_THETA = 10000.0
LOG2E = math.log2(math.e)
LN2 = math.log(2.0)

A_HEADS = 4
HEAD_DIM = 64
LANES = 128
B_PAIRS = ((128, 1), (512, 4), (2048, 16))
B_GROUPS = len(B_PAIRS)
B_GROUP_W = 256
N_GROUPS = 4
EXPERTS_PER_GROUP = 8
N_EXPERTS = N_GROUPS * EXPERTS_PER_GROUP
ROUTER_W = 128

IN_COL_VA = 1024
IN_COLS_QK = (0, 256, 512, 768, 1536, 1792, 2048, 2304, 2560, 2816)
IN_COL_VB = 3072
IN_COL_GATES = 3840

F32 = jnp.float32
BF16 = jnp.bfloat16
HIGHEST = lax.Precision.HIGHEST


def _dot(a, b):
    return jnp.dot(a, b, preferred_element_type=F32)


def _dot_nt(a, b):
    return lax.dot_general(a, b, (((1,), (1,)), ((), ())), preferred_element_type=F32)


def _sigmoid(x):
    return 1.0 / (1.0 + jnp.exp(-x))


def _ada_kernel(c_ref, w_ref, b_ref, o_ref):
    w = w_ref[0]
    for b in range(c_ref.shape[0]):
        c = c_ref[b]
        c_act = c * _sigmoid(c)
        o_ref[0, b:b + 1, :] = jnp.sum(c_act * w, axis=0, keepdims=True) + b_ref[0]


def _ada(c, w_ada, b_ada):
    depth, d, six_d = w_ada.shape
    bsz = c.shape[0]
    n_col = six_d // d
    return pl.pallas_call(
        _ada_kernel,
        grid=(depth, n_col),
        in_specs=[
            pl.BlockSpec((bsz, d, 1), lambda l, j: (0, 0, 0)),
            pl.BlockSpec((1, d, d), lambda l, j: (l, 0, j)),
            pl.BlockSpec((1, 1, d), lambda l, j: (l, 0, j)),
        ],
        out_specs=pl.BlockSpec((1, bsz, d), lambda l, j: (l, 0, j)),
        out_shape=jax.ShapeDtypeStruct((depth, bsz, six_d), F32),
        name="ada_mod",
    )(c.reshape(bsz, d, 1), w_ada, b_ada.reshape(depth, 1, six_d))


def _rope_kernel(pos_ref, f_ref, cos_ref, sin_ref):
    ang = pos_ref[0].astype(F32) * f_ref[...]
    cos = jnp.cos(ang)
    sin = jnp.sin(ang)
    cos_ref[0] = jnp.concatenate([cos, cos, cos, cos], axis=0).T
    sin_ref[0] = jnp.concatenate([-sin, sin, -sin, sin], axis=0).T


def _rope_tables(positions, ts):
    bsz, seq = positions.shape
    half = HEAD_DIM // 2
    inv_freq = ROPE_THETA ** (-jnp.arange(0, HEAD_DIM, 2, dtype=F32) / HEAD_DIM)
    return pl.pallas_call(
        _rope_kernel,
        grid=(bsz, seq // ts),
        in_specs=[
            pl.BlockSpec((1, 1, ts), lambda b, i: (b, 0, i)),
            pl.BlockSpec((half, 1), lambda b, i: (0, 0)),
        ],
        out_specs=[pl.BlockSpec((1, ts, LANES), lambda b, i: (b, i, 0))] * 2,
        out_shape=[jax.ShapeDtypeStruct((bsz, seq, LANES), F32)] * 2,
        name="rope_tables",
    )(positions.reshape(bsz, 1, seq), inv_freq.reshape(half, 1))


def _inproj_kernel(x_ref, mod_ref, cos_ref, sin_ref, gain_ref, w_ref, wvat_ref, seg_ref,
                   qa_ref, ka_ref, qb0_ref, qb1_ref, qb2_ref, kb0_ref, kb1_ref, kb2_ref,
                   vat_ref, vb0_ref, vb1_ref, vb2_ref, ga_ref, gb_ref, stage_ref):
    x = x_ref[0]
    tm = x.shape[0]
    ms = jnp.mean(x * x, axis=-1, keepdims=True)
    h = x * lax.rsqrt(ms + EPS) * (1.0 + mod_ref[0, 1:2, :]) + mod_ref[0, 0:1, :]
    hb = h.astype(BF16)

    cos = cos_ref[0]
    sin = sin_ref[0]
    seg = seg_ref[...]
    lane = lax.broadcasted_iota(jnp.int32, cos.shape, 1)
    first_half = (lane % HEAD_DIM) < (HEAD_DIM // 2)

    def norm_rope(y2, c):
        msq = _dot((y2 * y2).astype(BF16), seg)
        yn2 = y2 * lax.rsqrt(msq + EPS) * gain_ref[:, c * 256:(c + 1) * 256]
        out = []
        for hf in range(2):
            yn = yn2[:, hf * LANES:(hf + 1) * LANES]
            partner = jnp.where(first_half, pltpu.roll(yn, 96, axis=1), pltpu.roll(yn, 32, axis=1))
            out.append(yn * cos + partner * sin)
        return out

    def store_dilated(ref, group, val):
        dil = B_PAIRS[group][1]
        if dil == 1:
            ref[0] = val.astype(BF16)
            return
        for hf in range(B_GROUP_W // LANES):
            stage_ref[hf] = val[:, hf * LANES:(hf + 1) * LANES]
        for r in range(dil):
            for hf in range(B_GROUP_W // LANES):
                rows = stage_ref[hf, pl.ds(r, tm // dil, stride=dil), :]
                col = r * B_GROUP_W + hf * LANES
                ref[0, :, col:col + LANES] = rows.astype(BF16)

    flat = [(qa_ref, 0), (qa_ref, 1), (ka_ref, 0), (ka_ref, 1)]
    grouped = [(qb0_ref, 0), (qb1_ref, 1), (qb2_ref, 2), (kb0_ref, 0), (kb1_ref, 1), (kb2_ref, 2)]
    def qk_epilogue(c, y2):
        halves = norm_rope(y2, c)
        if c < len(flat):
            ref, t = flat[c]
            for hf in range(2):
                ref[0, :, t * 256 + hf * LANES:t * 256 + (hf + 1) * LANES] = halves[hf].astype(BF16)
        else:
            ref, group = grouped[c - len(flat)]
            store_dilated(ref, group, jnp.concatenate(halves, axis=1))

    def vat_epilogue(c, v):
        vat_ref[0, c * 256:(c + 1) * 256, :] = v.astype(BF16)

    def gate_epilogue(c, g):
        ref = ga_ref if c < 4 else gb_ref
        ref[0, :, (c % 4) * 256:(c % 4 + 1) * 256] = _sigmoid(g).astype(BF16)

    proj = lambda col: _dot(hb, w_ref[0, :, col:col + 256])
    jobs = []
    for c, col in enumerate(IN_COLS_QK):
        jobs.append((functools.partial(proj, col), functools.partial(qk_epilogue, c)))
    for c in range(wvat_ref.shape[1] // 256):
        jobs.append((lambda c=c: _dot_nt(wvat_ref[0, c * 256:(c + 1) * 256, :], hb),
                     functools.partial(vat_epilogue, c)))
    for group, ref in enumerate((vb0_ref, vb1_ref, vb2_ref)):
        jobs.append((functools.partial(proj, IN_COL_VB + group * 256),
                     functools.partial(store_dilated, ref, group)))
    for c in range((w_ref.shape[2] - IN_COL_GATES) // 256):
        jobs.append((functools.partial(proj, IN_COL_GATES + c * 256), functools.partial(gate_epilogue, c)))
    pending = None
    for matmul, epilogue in jobs:
        res = matmul()
        if pending is not None:
            pending[0](pending[1])
        pending = (epilogue, res)
    pending[0](pending[1])


def _inproj(x, mod_l, cos_l, sin_l, gain, w_all, wvat_all, seg, layer, tm):
    bsz, seq, d = x.shape
    a_v_w = wvat_all.shape[1]
    tok = lambda w: pl.BlockSpec((1, tm, w), lambda b, i: (b, i, 0))
    tok_t = pl.BlockSpec((1, a_v_w, tm), lambda b, i: (b, 0, i))
    const = lambda shape: pl.BlockSpec(shape, lambda b, i: (0,) * len(shape),
                                       pipeline_mode=pl.Buffered(1))
    of_layer = lambda a: pl.BlockSpec((1,) + a.shape[1:], lambda b, i: (layer, 0, 0),
                                      pipeline_mode=pl.Buffered(1))
    row = lambda w: jax.ShapeDtypeStruct((bsz, seq, w), BF16)
    dil_specs = [pl.BlockSpec((1, tm // dl, dl * B_GROUP_W), lambda b, i: (b, i, 0)) for _, dl in B_PAIRS]
    dil_shapes = [jax.ShapeDtypeStruct((bsz, seq // dl, dl * B_GROUP_W), BF16) for _, dl in B_PAIRS]
    outs = pl.pallas_call(
        _inproj_kernel,
        grid=(bsz, seq // tm),
        in_specs=[
            tok(d),
            pl.BlockSpec((1, 6, d), lambda b, i: (b, 0, 0)),
            tok(LANES), tok(LANES),
            const(gain.shape), of_layer(w_all), of_layer(wvat_all), const(seg.shape),
        ],
        out_specs=[tok(512), tok(512)] + dil_specs + dil_specs + [tok_t] + dil_specs + [tok(d), tok(d)],
        out_shape=[row(512), row(512)] + dil_shapes + dil_shapes
                  + [jax.ShapeDtypeStruct((bsz, a_v_w, seq), BF16)] + dil_shapes + [row(d), row(d)],
        scratch_shapes=[pltpu.VMEM((B_GROUP_W // LANES, tm, LANES), F32)],
        compiler_params=pltpu.CompilerParams(dimension_semantics=("parallel", "parallel")),
        name="in_proj",
    )(x, mod_l, cos_l, sin_l, gain, w_all, wvat_all, seg)
    qa, ka = outs[0], outs[1]
    qb, kb, vat, vb, ga, gb = outs[2:5], outs[5:8], outs[8], outs[9:12], outs[12], outs[13]
    return qa, ka, qb, kb, vat, vb, ga, gb


SCORE_BOUND_NO_SHIFT = 64.0
KV_CHUNK = 256


def _attn_a_kernel(bound_ref, lam_ref, sub_ref, q_ref, k_ref, vt_ref, o_ref,
                   m_ref, l_ref, acc_ref, pa_ref, pb_ref, *, tk, lam_init):
    q = q_ref[0]
    tq = q.shape[0]
    seq = k_ref.shape[1]
    lane = lax.broadcasted_iota(jnp.int32, q.shape, 1)
    zero = jnp.zeros_like(q)
    q_maps = (jnp.where(lane < HEAD_DIM, q, zero), jnp.where(lane >= HEAD_DIM, q, zero))
    n_chunk = tk // KV_CHUNK

    acc_ref[...] = jnp.zeros(acc_ref.shape, F32)
    l_ref[...] = jnp.zeros(l_ref.shape, F32)

    def scores_exp(tile, p_ref):
        start = pl.multiple_of(tile * tk, tk)
        k = k_ref[0, pl.ds(start, tk), :]
        for mi in range(2):
            p = jnp.exp2(_dot_nt(k, q_maps[mi]))
            l_ref[mi] += jnp.sum(p.reshape(tk // 8, 8, tq), axis=0)
            p_ref[mi] = p.astype(BF16)

    def weighted_values(tile, p_ref):
        start = pl.multiple_of(tile * tk, tk)
        vt = vt_ref[0, :, pl.ds(start, tk)]
        for mi in range(2):
            acc_ref[mi] += _dot(vt, p_ref[mi])

    n_tiles = seq // tk

    def no_shift_pair(jj, carry):
        t = 2 * jj
        scores_exp(t + 1, pb_ref)
        weighted_values(t, pa_ref)
        scores_exp(t + 2, pa_ref)
        weighted_values(t + 1, pb_ref)
        return carry

    def no_shift_loop():
        scores_exp(0, pa_ref)
        lax.fori_loop(0, n_tiles // 2 - 1, no_shift_pair, 0)
        scores_exp(n_tiles - 1, pb_ref)
        weighted_values(n_tiles - 2, pa_ref)
        weighted_values(n_tiles - 1, pb_ref)

    def online_max_body(j, carry):
        for c in range(n_chunk):
            start = pl.multiple_of(j * tk + c * KV_CHUNK, KV_CHUNK)
            k = k_ref[0, pl.ds(start, KV_CHUNK), :]
            vt = vt_ref[0, :, pl.ds(start, KV_CHUNK)]
            for mi in range(2):
                s = _dot_nt(k, q_maps[mi])
                m_old = m_ref[mi]
                m_new = jnp.maximum(m_old, jnp.max(s, axis=0, keepdims=True))
                alpha = jnp.exp2(m_old - m_new)
                p = jnp.exp2(s - m_new[0:1])
                l_ref[mi] = alpha * l_ref[mi] + jnp.sum(p.reshape(KV_CHUNK // 8, 8, tq), axis=0)
                acc_ref[mi] = alpha[0:1] * acc_ref[mi] + _dot(vt, p.astype(BF16))
                m_ref[mi] = m_new
        return carry

    no_shift = bound_ref[0] <= SCORE_BOUND_NO_SHIFT

    @pl.when(no_shift)
    def _():
        no_shift_loop()

    @pl.when(jnp.logical_not(no_shift))
    def _():
        m_ref[...] = jnp.full(m_ref.shape, -jnp.inf, F32)
        lax.fori_loop(0, seq // tk, online_max_body, 0)

    lam_p = lam_ref[...]
    s1 = jnp.sum(lam_p[0:1] * lam_p[1:2], axis=-1, keepdims=True)
    s2 = jnp.sum(lam_p[2:3] * lam_p[3:4], axis=-1, keepdims=True)
    lam = jnp.exp(s1) - jnp.exp(s2) + lam_init
    l0 = jnp.sum(l_ref[0], axis=0, keepdims=True)
    l1 = jnp.sum(l_ref[1], axis=0, keepdims=True)
    ot = acc_ref[0] / l0 - lam * (acc_ref[1] / l1)
    o = ot.T
    msq = jnp.mean(o * o, axis=-1, keepdims=True)
    o = o * lax.rsqrt(msq + EPS) * sub_ref[...] * (1.0 - lam_init)
    o_ref[0] = o.astype(BF16)


def _attn_a(score_bound, lam_p, subln, qa, ka, vat, lam_init, tq, tk):
    bsz, seq, _ = qa.shape
    return pl.pallas_call(
        functools.partial(_attn_a_kernel, tk=tk, lam_init=lam_init),
        grid=(bsz, A_HEADS, seq // tq),
        in_specs=[
            pl.BlockSpec(memory_space=pltpu.SMEM),
            pl.BlockSpec(lam_p.shape, lambda b, h, i: (0, 0)),
            pl.BlockSpec(subln.shape, lambda b, h, i: (0, 0)),
            pl.BlockSpec((1, tq, LANES), lambda b, h, i: (b, i, h)),
            pl.BlockSpec((1, seq, LANES), lambda b, h, i: (b, 0, h)),
            pl.BlockSpec((1, LANES, seq), lambda b, h, i: (b, h, 0)),
        ],
        out_specs=pl.BlockSpec((1, tq, LANES), lambda b, h, i: (b, i, h)),
        out_shape=jax.ShapeDtypeStruct((bsz, seq, A_HEADS * LANES), BF16),
        scratch_shapes=[
            pltpu.VMEM((2, 8, tq), F32),
            pltpu.VMEM((2, 8, tq), F32),
            pltpu.VMEM((2, LANES, tq), F32),
            pltpu.VMEM((2, tk, tq), BF16),
            pltpu.VMEM((2, tk, tq), BF16),
        ],
        compiler_params=pltpu.CompilerParams(dimension_semantics=("parallel", "parallel", "parallel")),
        name="diff_attn",
    )(score_bound, lam_p, subln, qa, ka, vat)


def _attn_b_kernel(q_ref, k_ref, v_ref, o_ref, lse_ref, *, radius):
    length = k_ref.shape[1]
    sub = 2 * radius
    win = sub + 2 * radius
    n_sub = q_ref.shape[1] // sub
    lane = lax.broadcasted_iota(jnp.int32, (sub, LANES), 1)
    low = lane < HEAD_DIM
    tile = lambda a, c: a[:, c * LANES:(c + 1) * LANES]

    units, windows, scores = [], [], []
    for sb in range(n_sub):
        t0 = pl.program_id(2) * (n_sub * sub) + sb * sub
        start = pl.multiple_of(jnp.clip(t0 - radius, 0, length - win), radius)
        kw = k_ref[0, pl.ds(start, win), :]
        windows.append(v_ref[0, pl.ds(start, win), :])
        q = q_ref[0, sb * sub:(sb + 1) * sub, :]
        qpos = t0 + lax.broadcasted_iota(jnp.int32, (sub, win), 0)
        kpos = start + lax.broadcasted_iota(jnp.int32, (sub, win), 1)
        valid = jnp.abs(kpos - qpos) <= radius
        for c in range(B_GROUP_W // LANES):
            for half in range(2):
                qc = tile(q, c)
                qm = jnp.where(low if half == 0 else jnp.logical_not(low), qc, jnp.zeros_like(qc))
                units.append((sb, c))
                scores.append(jnp.where(valid, _dot_nt(qm, tile(kw, c)), -jnp.inf))
    probs, sums, lses = [], [], []
    for s in scores:
        m = jnp.max(s, axis=-1, keepdims=True)
        p = jnp.exp2(s - m)
        l = jnp.sum(p, axis=-1, keepdims=True)
        probs.append(p.astype(BF16))
        sums.append(l)
        lses.append((m + jnp.log2(l)) * LN2)
    outs = [_dot(probs[i], tile(windows[sb], c)) / sums[i] for i, (sb, c) in enumerate(units)]
    for i in range(0, len(units), 2):
        sb, c = units[i]
        rows, cols = slice(sb * sub, (sb + 1) * sub), slice(c * LANES, (c + 1) * LANES)
        o_ref[0, rows, cols] = jnp.where(low, outs[i], outs[i + 1]).astype(BF16)
        lse_ref[0, rows, cols] = jnp.where(low, lses[i], lses[i + 1])


def _attn_b(qg, kg, vg, group, tq):
    window, dilation = B_PAIRS[group]
    radius = window // (2 * dilation)
    bsz, length, _ = qg.shape
    tq = min(tq, length)
    assert length >= 4 * radius and tq % (2 * radius) == 0
    return pl.pallas_call(
        functools.partial(_attn_b_kernel, radius=radius),
        grid=(bsz, dilation, length // tq),
        in_specs=[
            pl.BlockSpec((1, tq, B_GROUP_W), lambda b, r, i: (b, i, r)),
            pl.BlockSpec((1, length, B_GROUP_W), lambda b, r, i: (b, 0, r)),
            pl.BlockSpec((1, length, B_GROUP_W), lambda b, r, i: (b, 0, r)),
        ],
        out_specs=[pl.BlockSpec((1, tq, B_GROUP_W), lambda b, r, i: (b, i, r))] * 2,
        out_shape=[jax.ShapeDtypeStruct((bsz, length, dilation * B_GROUP_W), BF16),
                   jax.ShapeDtypeStruct((bsz, length, dilation * B_GROUP_W), F32)],
        compiler_params=pltpu.CompilerParams(dimension_semantics=("parallel", "parallel", "parallel")),
        name=f"band_attn_g{group}",
    )(qg, kg, vg)


def _merge_kernel(x_ref, mod_ref, oa_ref, ob0_ref, ls0_ref, ob1_ref, ls1_ref, ob2_ref, ls2_ref,
                  ga_ref, gb_ref, wpa_ref, wpb_ref, wo_ref, wr2_ref, br_ref,
                  x1_ref, h2_ref, route_ref, so1_ref, sl1_ref, so2_ref, sl2_ref):
    tm = x_ref.shape[1]

    def token_major(o_ref, l_ref, group, so_ref, sl_ref):
        dil = B_PAIRS[group][1]
        n_hf = B_GROUP_W // LANES
        for r in range(dil):
            for hf in range(n_hf):
                cols = slice(r * B_GROUP_W + hf * LANES, r * B_GROUP_W + (hf + 1) * LANES)
                so_ref[hf, pl.ds(r, tm // dil, stride=dil), :] = o_ref[0, :, cols].astype(F32)
                sl_ref[hf, pl.ds(r, tm // dil, stride=dil), :] = l_ref[0, :, cols]
        return (jnp.concatenate([so_ref[hf] for hf in range(n_hf)], axis=1),
                jnp.concatenate([sl_ref[hf] for hf in range(n_hf)], axis=1))

    o0, ls0 = ob0_ref[0].astype(F32), ls0_ref[0]
    o1, ls1 = token_major(ob1_ref, ls1_ref, 1, so1_ref, sl1_ref)
    o2, ls2 = token_major(ob2_ref, ls2_ref, 2, so2_ref, sl2_ref)
    mx = jnp.maximum(jnp.maximum(ls0, ls1), ls2)
    e0, e1, e2 = jnp.exp(ls0 - mx), jnp.exp(ls1 - mx), jnp.exp(ls2 - mx)
    ob = ((e0 * o0 + e1 * o1 + e2 * o2) / (e0 + e1 + e2)).astype(BF16)

    n_part = 2
    rows = [slice(p * (tm // n_part), (p + 1) * (tm // n_part)) for p in range(n_part)]
    pa = [_dot(oa_ref[0, r, :], wpa_ref[...]) for r in rows]
    pb = [_dot(ob[r], wpb_ref[...]) for r in rows]
    merged = [(ga_ref[0, r, :].astype(F32) * pa[p] + gb_ref[0, r, :].astype(F32) * pb[p]).astype(BF16)
              for p, r in enumerate(rows)]
    y = [_dot(merged[p], wo_ref[...]) for p in range(n_part)]
    for p, r in enumerate(rows):
        x1 = x_ref[0, r, :] + mod_ref[0, 2:3, :] * y[p]
        x1_ref[0, r, :] = x1
        ms = jnp.mean(x1 * x1, axis=-1, keepdims=True)
        h2 = x1 * lax.rsqrt(ms + EPS) * (1.0 + mod_ref[0, 4:5, :]) + mod_ref[0, 3:4, :]
        h2_hi = h2.astype(BF16)
        h2_ref[0, r, :] = h2_hi
        h2_lo = (h2 - h2_hi.astype(F32)).astype(BF16)
        both = _dot(h2_hi, wr2_ref[...])
        logits = (both[:, :ROUTER_W] + both[:, ROUTER_W:] + _dot(h2_lo, wr2_ref[:, :ROUTER_W])) + br_ref[...]
        route_ref[0, r, :] = _route(logits)


def _route(logits):
    lane = lax.broadcasted_iota(jnp.int32, logits.shape, 1)
    neg = -jnp.inf
    big = ROUTER_W
    is_grp = (lane >= N_EXPERTS) & (lane < N_EXPERTS + N_GROUPS)
    lg = jnp.where(is_grp, logits, neg)
    mg = jnp.max(lg, axis=-1, keepdims=True)
    g_lane = jnp.min(jnp.where(lg == mg, lane, big), axis=-1, keepdims=True)
    g_val = 1.0 / jnp.sum(jnp.exp(lg - mg), axis=-1, keepdims=True)
    lo = (g_lane - N_EXPERTS) * EXPERTS_PER_GROUP
    in_grp = (lane >= lo) & (lane < lo + EXPERTS_PER_GROUP)
    le = jnp.where(in_grp, logits, neg)
    m1 = jnp.max(le, axis=-1, keepdims=True)
    i1 = jnp.min(jnp.where(le == m1, lane, big), axis=-1, keepdims=True)
    le2 = jnp.where(lane == i1, neg, le)
    m2 = jnp.max(le2, axis=-1, keepdims=True)
    i2 = jnp.min(jnp.where(le2 == m2, lane, big), axis=-1, keepdims=True)
    e = jnp.exp(m2 - m1)
    w1 = g_val / (1.0 + e)
    w2 = g_val * e / (1.0 + e)
    return jnp.where(lane == 0, i1.astype(F32),
                     jnp.where(lane == 1, i2.astype(F32),
                               jnp.where(lane == 2, w1, jnp.where(lane == 3, w2, 0.0))))


def _merge(x, mod_l, oa, obs, ga, gb, wpa, wpb, wo, wr, br, tm):
    bsz, seq, d = x.shape
    tok = lambda w: pl.BlockSpec((1, tm, w), lambda b, i: (b, i, 0))
    const = lambda shape: pl.BlockSpec(shape, lambda b, i: (0,) * len(shape),
                                       pipeline_mode=pl.Buffered(1))
    ob_args, ob_specs = [], []
    for (o, lse), (_, dl) in zip(obs, B_PAIRS):
        ob_args += [o, lse]
        ob_specs += [pl.BlockSpec((1, tm // dl, dl * B_GROUP_W), lambda b, i: (b, i, 0))] * 2
    return pl.pallas_call(
        _merge_kernel,
        grid=(bsz, seq // tm),
        in_specs=[tok(d), pl.BlockSpec((1, 6, d), lambda b, i: (b, 0, 0)), tok(oa.shape[-1])]
                 + ob_specs + [tok(d), tok(d)]
                 + [const(w.shape) for w in (wpa, wpb, wo, wr, br)],
        out_specs=[tok(d), tok(d), tok(ROUTER_W)],
        out_shape=[jax.ShapeDtypeStruct((bsz, seq, d), F32),
                   jax.ShapeDtypeStruct((bsz, seq, d), BF16),
                   jax.ShapeDtypeStruct((bsz, seq, ROUTER_W), F32)],
        scratch_shapes=[pltpu.VMEM((B_GROUP_W // LANES, tm, LANES), F32)] * 4,
        compiler_params=pltpu.CompilerParams(dimension_semantics=("parallel", "parallel")),
        name="merge_proj",
    )(x, mod_l, oa, *ob_args, ga, gb, wpa, wpb, wo, wr, br)


MOE_BLOCK = 1024
SEG_ALIGN = 16
ROW_CHUNK = 128
PERM_CHUNK = 256
GATHER_CHUNK = 512
EXPERTS_PER_STEP = 4
SORTED_ROWS = -(-(2 * MOE_BLOCK + N_EXPERTS * (SEG_ALIGN - 1)) // GATHER_CHUNK) * GATHER_CHUNK
SORTED_ROWS_ALLOC = SORTED_ROWS + ROW_CHUNK


def _plan_kernel(route_ref, before_ref, posc_ref, posr_ref, base_ref, npad_ref):
    r = route_ref[0]
    lane = lax.broadcasted_iota(jnp.int32, r.shape, 1).astype(F32)
    oh1 = jnp.where(lane == r[:, 0:1], 1.0, 0.0)
    oh2 = jnp.where(lane == r[:, 1:2], 1.0, 0.0)
    cnt1 = jnp.sum(oh1, axis=0, keepdims=True)
    cnt2 = jnp.sum(oh2, axis=0, keepdims=True)
    npad = jnp.floor((cnt1 + cnt2 + (SEG_ALIGN - 1)) * (1.0 / SEG_ALIGN)) * SEG_ALIGN
    ri = lax.broadcasted_iota(jnp.int32, (LANES, LANES), 0)
    ci = lax.broadcasted_iota(jnp.int32, (LANES, LANES), 1)
    upper = jnp.where(ri < ci, 1.0, 0.0)
    npad8 = jnp.broadcast_to(npad, (8, LANES))
    base = jnp.dot(npad8, upper, precision=HIGHEST, preferred_element_type=F32)[0:1]

    pre = _dot(before_ref[...], jnp.concatenate([oh1, oh2], axis=1).astype(BF16))
    pre1, pre2 = pre[:, :LANES], pre[:, LANES:]
    pos1 = jnp.sum(oh1 * (base + pre1), axis=-1, keepdims=True)
    pos2 = jnp.sum(oh2 * (base + cnt1 + pre2), axis=-1, keepdims=True)
    packed = jnp.where(lane == 0.0, pos1, jnp.where(lane == 1.0, pos2, jnp.where(lane >= 2.0, r, 0.0)))
    posc_ref[0] = packed
    posr_ref[0] = packed.T[0:8, :]
    base_ref[0] = base
    npad_ref[0] = npad


def _moe_plan(route):
    n_blk, tb, _ = route.shape
    vec = pl.BlockSpec((1, 1, LANES), lambda i: (i, 0, 0))
    before = jnp.tril(jnp.ones((tb, tb), BF16), -1)
    return pl.pallas_call(
        _plan_kernel,
        grid=(n_blk,),
        in_specs=[pl.BlockSpec((1, tb, ROUTER_W), lambda i: (i, 0, 0)),
                  pl.BlockSpec((tb, tb), lambda i: (0, 0), pipeline_mode=pl.Buffered(1))],
        out_specs=[pl.BlockSpec((1, tb, LANES), lambda i: (i, 0, 0)),
                   pl.BlockSpec((1, 8, tb), lambda i: (i, 0, 0)), vec, vec],
        out_shape=[jax.ShapeDtypeStruct((n_blk, tb, LANES), F32),
                   jax.ShapeDtypeStruct((n_blk, 8, tb), F32),
                   jax.ShapeDtypeStruct((n_blk, 1, LANES), F32),
                   jax.ShapeDtypeStruct((n_blk, 1, LANES), F32)],
        compiler_params=pltpu.CompilerParams(dimension_semantics=("parallel",)),
        name="moe_plan",
    )(route, before)


def _moe_kernel(base_ref, npad_ref, x1_ref, mod_ref, h_ref, posc_ref, posr_ref, wg_ref, wu_ref, wd_ref,
                o_ref, xs_ref, ys_ref, gathered_ref):
    blk = pl.program_id(0)
    e = pl.program_id(1)
    tb = h_ref.shape[1]

    n_here = wg_ref.shape[0]
    n_experts = pl.num_programs(1) * n_here

    @pl.when(e == 0)
    def _():
        gathered_ref[0] = 0
        xs_ref[SORTED_ROWS:, :] = jnp.zeros((ROW_CHUNK, xs_ref.shape[1]), BF16)
        ys_ref[...] = jnp.zeros(ys_ref.shape, BF16)

    nxt = jnp.minimum((e + 2) * n_here, n_experts)
    need_rows = jnp.minimum(base_ref[blk, nxt] + ROW_CHUNK, SORTED_ROWS)
    need_chunks = (need_rows + GATHER_CHUNK - 1) // GATHER_CHUNK

    def gather_chunk(c, carry):
        r0 = pl.multiple_of(c * GATHER_CHUNK, GATHER_CHUNK)
        rid = (r0 + lax.broadcasted_iota(jnp.int32, (GATHER_CHUNK, tb), 0)).astype(F32)
        pr = posr_ref[0]
        sel = jnp.where(rid == pr[0:1], 1.0, jnp.where(rid == pr[1:2], 1.0, 0.0)).astype(BF16)
        xs_ref[pl.ds(r0, GATHER_CHUNK), :] = _dot(sel, h_ref[0]).astype(BF16)
        return carry

    lax.fori_loop(gathered_ref[0], need_chunks, gather_chunk, 0)
    gathered_ref[0] = jnp.maximum(gathered_ref[0], need_chunks)

    def gate_up(r0, j):
        xc = xs_ref[pl.ds(r0, ROW_CHUNK), :]
        return _dot(xc, wg_ref[j]), _dot(xc, wu_ref[j])

    def down(r0, j, a, u):
        hid = (a * _sigmoid(a)) * u
        ys_ref[pl.ds(r0, ROW_CHUNK), :] = _dot(hid.astype(BF16), wd_ref[j]).astype(BF16)

    starts = [pl.multiple_of(base_ref[blk, e * n_here + j], SEG_ALIGN) for j in range(n_here)]
    first = [gate_up(starts[j], j) for j in range(n_here)]
    for j in range(n_here):
        down(starts[j], j, *first[j])
    for j in range(n_here):
        seg_rows = npad_ref[blk, e * n_here + j]

        def more(c, carry, j=j, seg_rows=seg_rows):
            r0 = starts[j] + jnp.minimum(c * ROW_CHUNK, seg_rows - ROW_CHUNK)
            r0 = pl.multiple_of(r0, SEG_ALIGN)
            down(r0, j, *gate_up(r0, j))
            return carry

        lax.fori_loop(1, (seg_rows + ROW_CHUNK - 1) // ROW_CHUNK, more, 0)

    @pl.when(e == pl.num_programs(1) - 1)
    def _():
        ys = ys_ref[...]
        for c in range(tb // PERM_CHUNK):
            pc = posc_ref[0, c * PERM_CHUNK:(c + 1) * PERM_CHUNK, :]
            rid = lax.broadcasted_iota(jnp.int32, (PERM_CHUNK, SORTED_ROWS_ALLOC), 1).astype(F32)
            wsel = (jnp.where(rid == pc[:, 0:1], pc[:, 2:3], 0.0)
                    + jnp.where(rid == pc[:, 1:2], pc[:, 3:4], 0.0)).astype(BF16)
            y = _dot(wsel, ys)
            rows = slice(c * PERM_CHUNK, (c + 1) * PERM_CHUNK)
            o_ref[0, rows, :] = x1_ref[0, rows, :] + mod_ref[0, 5:6, :] * y


def _moe(x1, mod_l, h2, route, weg, weu, wed, layer):
    bsz, seq, d = x1.shape
    n_e, d_e = N_EXPERTS, weg.shape[-1]
    e0 = layer * (n_e // EXPERTS_PER_STEP)
    tb = min(MOE_BLOCK, seq)
    assert tb == MOE_BLOCK, "sorted-row capacity is sized for MOE_BLOCK tokens"
    per_batch = seq // tb
    n_blk = bsz * per_batch
    blocked = lambda a: a.reshape(n_blk, tb, a.shape[-1])
    posc, posr, base, npad = _moe_plan(blocked(route))
    base_i = base.reshape(n_blk, LANES).astype(jnp.int32)
    npad_i = npad.reshape(n_blk, LANES).astype(jnp.int32)

    tok = lambda w: pl.BlockSpec((1, tb, w), lambda i, e, *_: (i, 0, 0))
    grid_spec = pltpu.PrefetchScalarGridSpec(
        num_scalar_prefetch=2,
        grid=(n_blk, n_e // EXPERTS_PER_STEP),
        in_specs=[tok(d),
                  pl.BlockSpec((1, 6, d), lambda i, e, *_: (i // per_batch, 0, 0)),
                  tok(d), tok(LANES),
                  pl.BlockSpec((1, 8, tb), lambda i, e, *_: (i, 0, 0)),
                  pl.BlockSpec((EXPERTS_PER_STEP, d, d_e), lambda i, e, *_: (e0 + e, 0, 0)),
                  pl.BlockSpec((EXPERTS_PER_STEP, d, d_e), lambda i, e, *_: (e0 + e, 0, 0)),
                  pl.BlockSpec((EXPERTS_PER_STEP, d_e, d), lambda i, e, *_: (e0 + e, 0, 0))],
        out_specs=tok(d),
        scratch_shapes=[pltpu.VMEM((SORTED_ROWS_ALLOC, d), BF16),
                        pltpu.VMEM((SORTED_ROWS_ALLOC, d), BF16),
                        pltpu.SMEM((1,), jnp.int32)],
    )
    out = pl.pallas_call(
        _moe_kernel,
        grid_spec=grid_spec,
        out_shape=jax.ShapeDtypeStruct((n_blk, tb, d), F32),
        compiler_params=pltpu.CompilerParams(dimension_semantics=("parallel", "arbitrary")),
        name="moe_experts",
    )(base_i, npad_i, blocked(x1), mod_l, blocked(h2), posc, posr, weg, weu, wed)
    return out.reshape(bsz, seq, d)


def _tiles(seq):
    return dict(
        ts_rope=min(1024, seq),
        tm_proj=min(512, seq),
        tq_a=min(1024, seq),
        tk_a=min(1024, seq),
        tq_b=512,
        tm_merge=min(512, seq),
    )


def kernel(x, c, positions, w_ada, b_ada, w_in, qn_a, kn_a, lam_q1, lam_k1, lam_q2, lam_k2,
           subln_a, qn_b, kn_b, w_pa, w_pb, w_o, w_r1, b_r1, w_r2, b_r2,
           w_e_gate, w_e_up, w_e_down):
    depth = w_ada.shape[0]
    bsz, seq, d = x.shape
    t = _tiles(seq)

    mod = _ada(c, w_ada, b_ada).reshape(depth, bsz, 6, d)
    cos_l, sin_l = _rope_tables(positions, t["ts_rope"])
    seg = jnp.kron(jnp.eye(256 // HEAD_DIM, dtype=F32),
                   jnp.full((HEAD_DIM, HEAD_DIM), 1.0 / HEAD_DIM, F32)).astype(BF16)
    q_scale = HEAD_DIM ** -0.5 * LOG2E

    w_in_b = w_in.astype(BF16)
    wvat_b = w_in[:, :, IN_COL_VA:IN_COL_VA + 512].transpose(0, 2, 1).astype(BF16)
    experts = lambda a: a.astype(BF16).reshape((-1,) + a.shape[2:])
    weg_b, weu_b, wed_b = experts(w_e_gate), experts(w_e_up), experts(w_e_down)

    for layer in range(depth):
        lam_init = 0.8 - 0.6 * math.exp(-0.3 * layer)
        gain = jnp.concatenate([
            jnp.tile(qn_a[layer] * q_scale, 8), jnp.tile(kn_a[layer], 8),
            jnp.tile(qn_b[layer] * q_scale, 12), jnp.tile(kn_b[layer], 12)]).reshape(1, -1)

        qa, ka, qb, kb, vat, vb, ga, gb = _inproj(x, mod[layer], cos_l, sin_l, gain, w_in_b, wvat_b, seg,
                                                  layer, t["tm_proj"])

        score_bound = (1.01 * HEAD_DIM * q_scale * jnp.max(jnp.abs(qn_a[layer]))
                       * jnp.max(jnp.abs(kn_a[layer]))).reshape(1)
        lam_p = jnp.stack([lam_q1[layer], lam_k1[layer], lam_q2[layer], lam_k2[layer]])
        oa = _attn_a(score_bound, lam_p, subln_a[layer].reshape(1, -1), qa, ka, vat, lam_init,
                     t["tq_a"], t["tk_a"])
        obs = [_attn_b(qb[g], kb[g], vb[g], g, t["tq_b"]) for g in range(B_GROUPS)]

        wr = jnp.zeros((d, ROUTER_W), F32)
        wr = wr.at[:, :N_EXPERTS].set(w_r2[layer]).at[:, N_EXPERTS:N_EXPERTS + N_GROUPS].set(w_r1[layer])
        br = jnp.zeros((1, ROUTER_W), F32)
        br = br.at[0, :N_EXPERTS].set(b_r2[layer]).at[0, N_EXPERTS:N_EXPERTS + N_GROUPS].set(b_r1[layer])
        wr_hi = wr.astype(BF16)
        wr2 = jnp.concatenate([wr_hi, (wr - wr_hi.astype(F32)).astype(BF16)], axis=1)
        x1, h2, route = _merge(x, mod[layer], oa, obs, ga, gb,
                               w_pa[layer].astype(BF16), w_pb[layer].astype(BF16), w_o[layer].astype(BF16),
                               wr2, br, t["tm_merge"])

        x = _moe(x1, mod[layer], h2, route, weg_b, weu_b, wed_b, layer)
    return x
```

```python
import functools
import math

import jax
import jax.numpy as jnp
from jax import lax
from jax.experimental import pallas as pl
from jax.experimental.pallas import tpu as pltpu

EPS = 1e-6
ROPE_THETA = 10000.0
LOG2E = math.log2(math.e)
LN2 = math.log(2.0)

A_HEADS = 4
HEAD_DIM = 64
LANES = 128
B_PAIRS = ((128, 1), (512, 4), (2048, 16))
B_GROUPS = len(B_PAIRS)
B_GROUP_W = 256
N_GROUPS = 4
EXPERTS_PER_GROUP = 8
N_EXPERTS = N_GROUPS * EXPERTS_PER_GROUP
ROUTER_W = 128

IN_COL_VA = 1024
IN_COLS_QK = (0, 256, 512, 768, 1536, 1792, 2048, 2304, 2560, 2816)
IN_COL_VB = 3072
IN_COL_GATES = 3840

F32 = jnp.float32
BF16 = jnp.bfloat16
HIGHEST = lax.Precision.HIGHEST


def _dot(a, b):
    return jnp.dot(a, b, preferred_element_type=F32)


def _dot_nt(a, b):
    return lax.dot_general(a, b, (((1,), (1,)), ((), ())), preferred_element_type=F32)


def _sigmoid(x):
    return 1.0 / (1.0 + jnp.exp(-x))


def _ada_kernel(c_ref, w_ref, b_ref, o_ref):
    w = w_ref[0]
    for b in range(c_ref.shape[0]):
        c = c_ref[b]
        c_act = c * _sigmoid(c)
        o_ref[0, b:b + 1, :] = jnp.sum(c_act * w, axis=0, keepdims=True) + b_ref[0]


def _ada(c, w_ada, b_ada):
    depth, d, six_d = w_ada.shape
    bsz = c.shape[0]
    n_col = six_d // d
    return pl.pallas_call(
        _ada_kernel,
        grid=(depth, n_col),
        in_specs=[
            pl.BlockSpec((bsz, d, 1), lambda l, j: (0, 0, 0)),
            pl.BlockSpec((1, d, d), lambda l, j: (l, 0, j)),
            pl.BlockSpec((1, 1, d), lambda l, j: (l, 0, j)),
        ],
        out_specs=pl.BlockSpec((1, bsz, d), lambda l, j: (l, 0, j)),
        out_shape=jax.ShapeDtypeStruct((depth, bsz, six_d), F32),
        name="ada_mod",
    )(c.reshape(bsz, d, 1), w_ada, b_ada.reshape(depth, 1, six_d))


def _rope_kernel(pos_ref, f_ref, cos_ref, sin_ref):
    ang = pos_ref[0].astype(F32) * f_ref[...]
    cos = jnp.cos(ang)
    sin = jnp.sin(ang)
    cos_ref[0] = jnp.concatenate([cos, cos, cos, cos], axis=0).T
    sin_ref[0] = jnp.concatenate([-sin, sin, -sin, sin], axis=0).T


def _rope_tables(positions, ts):
    bsz, seq = positions.shape
    half = HEAD_DIM // 2
    inv_freq = ROPE_THETA ** (-jnp.arange(0, HEAD_DIM, 2, dtype=F32) / HEAD_DIM)
    return pl.pallas_call(
        _rope_kernel,
        grid=(bsz, seq // ts),
        in_specs=[
            pl.BlockSpec((1, 1, ts), lambda b, i: (b, 0, i)),
            pl.BlockSpec((half, 1), lambda b, i: (0, 0)),
        ],
        out_specs=[pl.BlockSpec((1, ts, LANES), lambda b, i: (b, i, 0))] * 2,
        out_shape=[jax.ShapeDtypeStruct((bsz, seq, LANES), F32)] * 2,
        name="rope_tables",
    )(positions.reshape(bsz, 1, seq), inv_freq.reshape(half, 1))


def _inproj_kernel(x_ref, mod_ref, cos_ref, sin_ref, gain_ref, w_ref, wvat_ref, seg_ref,
                   qa_ref, ka_ref, qb0_ref, qb1_ref, qb2_ref, kb0_ref, kb1_ref, kb2_ref,
                   vat_ref, vb0_ref, vb1_ref, vb2_ref, ga_ref, gb_ref, stage_ref):
    x = x_ref[0]
    tm = x.shape[0]
    ms = jnp.mean(x * x, axis=-1, keepdims=True)
    h = x * lax.rsqrt(ms + EPS) * (1.0 + mod_ref[0, 1:2, :]) + mod_ref[0, 0:1, :]
    hb = h.astype(BF16)

    cos = cos_ref[0]
    sin = sin_ref[0]
    seg = seg_ref[...]
    lane = lax.broadcasted_iota(jnp.int32, cos.shape, 1)
    first_half = (lane % HEAD_DIM) < (HEAD_DIM // 2)

    def norm_rope(y2, c):
        msq = _dot((y2 * y2).astype(BF16), seg)
        yn2 = y2 * lax.rsqrt(msq + EPS) * gain_ref[:, c * 256:(c + 1) * 256]
        out = []
        for hf in range(2):
            yn = yn2[:, hf * LANES:(hf + 1) * LANES]
            partner = jnp.where(first_half, pltpu.roll(yn, 96, axis=1), pltpu.roll(yn, 32, axis=1))
            out.append(yn * cos + partner * sin)
        return out

    def store_dilated(ref, group, val):
        dil = B_PAIRS[group][1]
        if dil == 1:
            ref[0] = val.astype(BF16)
            return
        for hf in range(B_GROUP_W // LANES):
            stage_ref[hf] = val[:, hf * LANES:(hf + 1) * LANES]
        for r in range(dil):
            for hf in range(B_GROUP_W // LANES):
                rows = stage_ref[hf, pl.ds(r, tm // dil, stride=dil), :]
                col = r * B_GROUP_W + hf * LANES
                ref[0, :, col:col + LANES] = rows.astype(BF16)

    flat = [(qa_ref, 0), (qa_ref, 1), (ka_ref, 0), (ka_ref, 1)]
    grouped = [(qb0_ref, 0), (qb1_ref, 1), (qb2_ref, 2), (kb0_ref, 0), (kb1_ref, 1), (kb2_ref, 2)]
    def qk_epilogue(c, y2):
        halves = norm_rope(y2, c)
        if c < len(flat):
            ref, t = flat[c]
            for hf in range(2):
                ref[0, :, t * 256 + hf * LANES:t * 256 + (hf + 1) * LANES] = halves[hf].astype(BF16)
        else:
            ref, group = grouped[c - len(flat)]
            store_dilated(ref, group, jnp.concatenate(halves, axis=1))

    def vat_epilogue(c, v):
        vat_ref[0, c * 256:(c + 1) * 256, :] = v.astype(BF16)

    def gate_epilogue(c, g):
        ref = ga_ref if c < 4 else gb_ref
        ref[0, :, (c % 4) * 256:(c % 4 + 1) * 256] = _sigmoid(g).astype(BF16)

    proj = lambda col: _dot(hb, w_ref[0, :, col:col + 256])
    heavy, light = [], []
    for c, col in enumerate(IN_COLS_QK):
        heavy.append((functools.partial(proj, col), functools.partial(qk_epilogue, c)))
    for c in range(wvat_ref.shape[1] // 256):
        light.append((lambda c=c: _dot_nt(wvat_ref[0, c * 256:(c + 1) * 256, :], hb),
                      functools.partial(vat_epilogue, c)))
    for group, ref in enumerate((vb0_ref, vb1_ref, vb2_ref)):
        light.append((functools.partial(proj, IN_COL_VB + group * 256),
                      functools.partial(store_dilated, ref, group)))
    for c in range((w_ref.shape[2] - IN_COL_GATES) // 256):
        light.append((functools.partial(proj, IN_COL_GATES + c * 256), functools.partial(gate_epilogue, c)))
    jobs = []
    for i in range(max(len(heavy), len(light))):
        jobs += light[i:i + 1] + heavy[i:i + 1]
    pending = None
    for matmul, epilogue in jobs:
        res = matmul()
        if pending is not None:
            pending[0](pending[1])
        pending = (epilogue, res)
    pending[0](pending[1])


def _inproj(x, mod_l, cos_l, sin_l, gain, w_all, wvat_all, seg, layer, tm):
    bsz, seq, d = x.shape
    a_v_w = wvat_all.shape[1]
    tok = lambda w: pl.BlockSpec((1, tm, w), lambda b, i: (b, i, 0))
    tok_t = pl.BlockSpec((1, a_v_w, tm), lambda b, i: (b, 0, i))
    const = lambda shape: pl.BlockSpec(shape, lambda b, i: (0,) * len(shape),
                                       pipeline_mode=pl.Buffered(1))
    of_layer = lambda a: pl.BlockSpec((1,) + a.shape[1:], lambda b, i: (layer, 0, 0),
                                      pipeline_mode=pl.Buffered(1))
    row = lambda w: jax.ShapeDtypeStruct((bsz, seq, w), BF16)
    dil_specs = [pl.BlockSpec((1, tm // dl, dl * B_GROUP_W), lambda b, i: (b, i, 0)) for _, dl in B_PAIRS]
    dil_shapes = [jax.ShapeDtypeStruct((bsz, seq // dl, dl * B_GROUP_W), BF16) for _, dl in B_PAIRS]
    outs = pl.pallas_call(
        _inproj_kernel,
        grid=(bsz, seq // tm),
        in_specs=[
            tok(d),
            pl.BlockSpec((1, 6, d), lambda b, i: (b, 0, 0)),
            tok(LANES), tok(LANES),
            const(gain.shape), of_layer(w_all), of_layer(wvat_all), const(seg.shape),
        ],
        out_specs=[tok(512), tok(512)] + dil_specs + dil_specs + [tok_t] + dil_specs + [tok(d), tok(d)],
        out_shape=[row(512), row(512)] + dil_shapes + dil_shapes
                  + [jax.ShapeDtypeStruct((bsz, a_v_w, seq), BF16)] + dil_shapes + [row(d), row(d)],
        scratch_shapes=[pltpu.VMEM((B_GROUP_W // LANES, tm, LANES), F32)],
        compiler_params=pltpu.CompilerParams(dimension_semantics=("parallel", "parallel")),
        name="in_proj",
    )(x, mod_l, cos_l, sin_l, gain, w_all, wvat_all, seg)
    qa, ka = outs[0], outs[1]
    qb, kb, vat, vb, ga, gb = outs[2:5], outs[5:8], outs[8], outs[9:12], outs[12], outs[13]
    return qa, ka, qb, kb, vat, vb, ga, gb


SCORE_BOUND_NO_SHIFT = 64.0
KV_CHUNK = 256


def _attn_a_kernel(bound_ref, lam_ref, sub_ref, q_ref, k_ref, vt_ref, o_ref,
                   m_ref, l_ref, acc_ref, pa_ref, pb_ref, *, tk, lam_init):
    q = q_ref[0]
    tq = q.shape[0]
    seq = k_ref.shape[1]
    lane = lax.broadcasted_iota(jnp.int32, q.shape, 1)
    zero = jnp.zeros_like(q)
    q_maps = (jnp.where(lane < HEAD_DIM, q, zero), jnp.where(lane >= HEAD_DIM, q, zero))
    n_chunk = tk // KV_CHUNK

    acc_ref[...] = jnp.zeros(acc_ref.shape, F32)
    l_ref[...] = jnp.zeros(l_ref.shape, F32)

    def scores_exp(tile, p_ref):
        start = pl.multiple_of(tile * tk, tk)
        k = k_ref[0, pl.ds(start, tk), :]
        for mi in range(2):
            p = jnp.exp2(_dot_nt(k, q_maps[mi]))
            l_ref[mi] += jnp.sum(p.reshape(tk // 8, 8, tq), axis=0)
            p_ref[mi] = p.astype(BF16)

    def weighted_values(tile, p_ref):
        start = pl.multiple_of(tile * tk, tk)
        vt = vt_ref[0, :, pl.ds(start, tk)]
        for mi in range(2):
            acc_ref[mi] += _dot(vt, p_ref[mi])

    n_tiles = seq // tk

    def no_shift_pair(jj, carry):
        t = 2 * jj
        scores_exp(t + 1, pb_ref)
        weighted_values(t, pa_ref)
        scores_exp(t + 2, pa_ref)
        weighted_values(t + 1, pb_ref)
        return carry

    def no_shift_loop():
        scores_exp(0, pa_ref)
        lax.fori_loop(0, n_tiles // 2 - 1, no_shift_pair, 0)
        scores_exp(n_tiles - 1, pb_ref)
        weighted_values(n_tiles - 2, pa_ref)
        weighted_values(n_tiles - 1, pb_ref)

    def online_max_body(j, carry):
        for c in range(n_chunk):
            start = pl.multiple_of(j * tk + c * KV_CHUNK, KV_CHUNK)
            k = k_ref[0, pl.ds(start, KV_CHUNK), :]
            vt = vt_ref[0, :, pl.ds(start, KV_CHUNK)]
            for mi in range(2):
                s = _dot_nt(k, q_maps[mi])
                m_old = m_ref[mi]
                m_new = jnp.maximum(m_old, jnp.max(s, axis=0, keepdims=True))
                alpha = jnp.exp2(m_old - m_new)
                p = jnp.exp2(s - m_new[0:1])
                l_ref[mi] = alpha * l_ref[mi] + jnp.sum(p.reshape(KV_CHUNK // 8, 8, tq), axis=0)
                acc_ref[mi] = alpha[0:1] * acc_ref[mi] + _dot(vt, p.astype(BF16))
                m_ref[mi] = m_new
        return carry

    no_shift = bound_ref[0] <= SCORE_BOUND_NO_SHIFT

    @pl.when(no_shift)
    def _():
        no_shift_loop()

    @pl.when(jnp.logical_not(no_shift))
    def _():
        m_ref[...] = jnp.full(m_ref.shape, -jnp.inf, F32)
        lax.fori_loop(0, seq // tk, online_max_body, 0)

    lam_p = lam_ref[...]
    s1 = jnp.sum(lam_p[0:1] * lam_p[1:2], axis=-1, keepdims=True)
    s2 = jnp.sum(lam_p[2:3] * lam_p[3:4], axis=-1, keepdims=True)
    lam = jnp.exp(s1) - jnp.exp(s2) + lam_init
    l0 = jnp.sum(l_ref[0], axis=0, keepdims=True)
    l1 = jnp.sum(l_ref[1], axis=0, keepdims=True)
    ot = acc_ref[0] / l0 - lam * (acc_ref[1] / l1)
    o = ot.T
    msq = jnp.mean(o * o, axis=-1, keepdims=True)
    o = o * lax.rsqrt(msq + EPS) * sub_ref[...] * (1.0 - lam_init)
    o_ref[0] = o.astype(BF16)


def _attn_a(score_bound, lam_p, subln, qa, ka, vat, lam_init, tq, tk):
    bsz, seq, _ = qa.shape
    return pl.pallas_call(
        functools.partial(_attn_a_kernel, tk=tk, lam_init=lam_init),
        grid=(bsz, A_HEADS, seq // tq),
        in_specs=[
            pl.BlockSpec(memory_space=pltpu.SMEM),
            pl.BlockSpec(lam_p.shape, lambda b, h, i: (0, 0)),
            pl.BlockSpec(subln.shape, lambda b, h, i: (0, 0)),
            pl.BlockSpec((1, tq, LANES), lambda b, h, i: (b, i, h)),
            pl.BlockSpec((1, seq, LANES), lambda b, h, i: (b, 0, h)),
            pl.BlockSpec((1, LANES, seq), lambda b, h, i: (b, h, 0)),
        ],
        out_specs=pl.BlockSpec((1, tq, LANES), lambda b, h, i: (b, i, h)),
        out_shape=jax.ShapeDtypeStruct((bsz, seq, A_HEADS * LANES), BF16),
        scratch_shapes=[
            pltpu.VMEM((2, 8, tq), F32),
            pltpu.VMEM((2, 8, tq), F32),
            pltpu.VMEM((2, LANES, tq), F32),
            pltpu.VMEM((2, tk, tq), BF16),
            pltpu.VMEM((2, tk, tq), BF16),
        ],
        compiler_params=pltpu.CompilerParams(dimension_semantics=("parallel", "parallel", "parallel")),
        name="diff_attn",
    )(score_bound, lam_p, subln, qa, ka, vat)


def _attn_b_kernel(q_ref, k_ref, v_ref, o_ref, lse_ref, *, radius):
    length = k_ref.shape[1]
    sub = 2 * radius
    win = sub + 2 * radius
    n_sub = q_ref.shape[1] // sub
    lane = lax.broadcasted_iota(jnp.int32, (sub, LANES), 1)
    low = lane < HEAD_DIM
    tile = lambda a, c: a[:, c * LANES:(c + 1) * LANES]

    units, windows, scores = [], [], []
    for sb in range(n_sub):
        t0 = pl.program_id(2) * (n_sub * sub) + sb * sub
        start = pl.multiple_of(jnp.clip(t0 - radius, 0, length - win), radius)
        kw = k_ref[0, pl.ds(start, win), :]
        windows.append(v_ref[0, pl.ds(start, win), :])
        q = q_ref[0, sb * sub:(sb + 1) * sub, :]
        qpos = t0 + lax.broadcasted_iota(jnp.int32, (sub, win), 0)
        kpos = start + lax.broadcasted_iota(jnp.int32, (sub, win), 1)
        valid = jnp.abs(kpos - qpos) <= radius
        for c in range(B_GROUP_W // LANES):
            for half in range(2):
                qc = tile(q, c)
                qm = jnp.where(low if half == 0 else jnp.logical_not(low), qc, jnp.zeros_like(qc))
                units.append((sb, c))
                scores.append(jnp.where(valid, _dot_nt(qm, tile(kw, c)), -jnp.inf))
    probs, sums, lses = [], [], []
    for s in scores:
        m = jnp.max(s, axis=-1, keepdims=True)
        p = jnp.exp2(s - m)
        l = jnp.sum(p, axis=-1, keepdims=True)
        probs.append(p.astype(BF16))
        sums.append(l)
        lses.append((m + jnp.log2(l)) * LN2)
    outs = [_dot(probs[i], tile(windows[sb], c)) / sums[i] for i, (sb, c) in enumerate(units)]
    for i in range(0, len(units), 2):
        sb, c = units[i]
        rows, cols = slice(sb * sub, (sb + 1) * sub), slice(c * LANES, (c + 1) * LANES)
        o_ref[0, rows, cols] = jnp.where(low, outs[i], outs[i + 1]).astype(BF16)
        lse_ref[0, rows, cols] = jnp.where(low, lses[i], lses[i + 1])


def _attn_b(qg, kg, vg, group, tq):
    window, dilation = B_PAIRS[group]
    radius = window // (2 * dilation)
    bsz, length, _ = qg.shape
    tq = min(tq, length)
    assert length >= 4 * radius and tq % (2 * radius) == 0
    return pl.pallas_call(
        functools.partial(_attn_b_kernel, radius=radius),
        grid=(bsz, dilation, length // tq),
        in_specs=[
            pl.BlockSpec((1, tq, B_GROUP_W), lambda b, r, i: (b, i, r)),
            pl.BlockSpec((1, length, B_GROUP_W), lambda b, r, i: (b, 0, r)),
            pl.BlockSpec((1, length, B_GROUP_W), lambda b, r, i: (b, 0, r)),
        ],
        out_specs=[pl.BlockSpec((1, tq, B_GROUP_W), lambda b, r, i: (b, i, r))] * 2,
        out_shape=[jax.ShapeDtypeStruct((bsz, length, dilation * B_GROUP_W), BF16),
                   jax.ShapeDtypeStruct((bsz, length, dilation * B_GROUP_W), F32)],
        compiler_params=pltpu.CompilerParams(dimension_semantics=("parallel", "parallel", "parallel")),
        name=f"band_attn_g{group}",
    )(qg, kg, vg)


def _merge_kernel(x_ref, mod_ref, oa_ref, ob0_ref, ls0_ref, ob1_ref, ls1_ref, ob2_ref, ls2_ref,
                  ga_ref, gb_ref, wpa_ref, wpb_ref, wo_ref, wr2_ref, br_ref,
                  x1_ref, h2_ref, route_ref, so1_ref, sl1_ref, so2_ref, sl2_ref):
    tm = x_ref.shape[1]

    def token_major(o_ref, l_ref, group, so_ref, sl_ref):
        dil = B_PAIRS[group][1]
        n_hf = B_GROUP_W // LANES
        for r in range(dil):
            for hf in range(n_hf):
                cols = slice(r * B_GROUP_W + hf * LANES, r * B_GROUP_W + (hf + 1) * LANES)
                so_ref[hf, pl.ds(r, tm // dil, stride=dil), :] = o_ref[0, :, cols].astype(F32)
                sl_ref[hf, pl.ds(r, tm // dil, stride=dil), :] = l_ref[0, :, cols]
        return (jnp.concatenate([so_ref[hf] for hf in range(n_hf)], axis=1),
                jnp.concatenate([sl_ref[hf] for hf in range(n_hf)], axis=1))

    o0, ls0 = ob0_ref[0].astype(F32), ls0_ref[0]
    o1, ls1 = token_major(ob1_ref, ls1_ref, 1, so1_ref, sl1_ref)
    o2, ls2 = token_major(ob2_ref, ls2_ref, 2, so2_ref, sl2_ref)
    mx = jnp.maximum(jnp.maximum(ls0, ls1), ls2)
    e0, e1, e2 = jnp.exp(ls0 - mx), jnp.exp(ls1 - mx), jnp.exp(ls2 - mx)
    ob = ((e0 * o0 + e1 * o1 + e2 * o2) / (e0 + e1 + e2)).astype(BF16)

    n_part = 2
    rows = [slice(p * (tm // n_part), (p + 1) * (tm // n_part)) for p in range(n_part)]
    pa = [_dot(oa_ref[0, r, :], wpa_ref[...]) for r in rows]
    pb = [_dot(ob[r], wpb_ref[...]) for r in rows]
    merged = [(ga_ref[0, r, :].astype(F32) * pa[p] + gb_ref[0, r, :].astype(F32) * pb[p]).astype(BF16)
              for p, r in enumerate(rows)]
    y = [_dot(merged[p], wo_ref[...]) for p in range(n_part)]
    for p, r in enumerate(rows):
        x1 = x_ref[0, r, :] + mod_ref[0, 2:3, :] * y[p]
        x1_ref[0, r, :] = x1
        ms = jnp.mean(x1 * x1, axis=-1, keepdims=True)
        h2 = x1 * lax.rsqrt(ms + EPS) * (1.0 + mod_ref[0, 4:5, :]) + mod_ref[0, 3:4, :]
        h2_hi = h2.astype(BF16)
        h2_ref[0, r, :] = h2_hi
        h2_lo = (h2 - h2_hi.astype(F32)).astype(BF16)
        both = _dot(h2_hi, wr2_ref[...])
        logits = (both[:, :ROUTER_W] + both[:, ROUTER_W:] + _dot(h2_lo, wr2_ref[:, :ROUTER_W])) + br_ref[...]
        route_ref[0, r, :] = _route(logits)


def _route(logits):
    lane = lax.broadcasted_iota(jnp.int32, logits.shape, 1)
    neg = -jnp.inf
    big = ROUTER_W
    is_grp = (lane >= N_EXPERTS) & (lane < N_EXPERTS + N_GROUPS)
    lg = jnp.where(is_grp, logits, neg)
    mg = jnp.max(lg, axis=-1, keepdims=True)
    g_lane = jnp.min(jnp.where(lg == mg, lane, big), axis=-1, keepdims=True)
    g_val = 1.0 / jnp.sum(jnp.exp(lg - mg), axis=-1, keepdims=True)
    lo = (g_lane - N_EXPERTS) * EXPERTS_PER_GROUP
    in_grp = (lane >= lo) & (lane < lo + EXPERTS_PER_GROUP)
    le = jnp.where(in_grp, logits, neg)
    m1 = jnp.max(le, axis=-1, keepdims=True)
    i1 = jnp.min(jnp.where(le == m1, lane, big), axis=-1, keepdims=True)
    le2 = jnp.where(lane == i1, neg, le)
    m2 = jnp.max(le2, axis=-1, keepdims=True)
    i2 = jnp.min(jnp.where(le2 == m2, lane, big), axis=-1, keepdims=True)
    e = jnp.exp(m2 - m1)
    w1 = g_val / (1.0 + e)
    w2 = g_val * e / (1.0 + e)
    return jnp.where(lane == 0, i1.astype(F32),
                     jnp.where(lane == 1, i2.astype(F32),
                               jnp.where(lane == 2, w1, jnp.where(lane == 3, w2, 0.0))))


def _merge(x, mod_l, oa, obs, ga, gb, wpa, wpb, wo, wr, br, tm):
    bsz, seq, d = x.shape
    tok = lambda w: pl.BlockSpec((1, tm, w), lambda b, i: (b, i, 0))
    const = lambda shape: pl.BlockSpec(shape, lambda b, i: (0,) * len(shape),
                                       pipeline_mode=pl.Buffered(1))
    ob_args, ob_specs = [], []
    for (o, lse), (_, dl) in zip(obs, B_PAIRS):
        ob_args += [o, lse]
        ob_specs += [pl.BlockSpec((1, tm // dl, dl * B_GROUP_W), lambda b, i: (b, i, 0))] * 2
    return pl.pallas_call(
        _merge_kernel,
        grid=(bsz, seq // tm),
        in_specs=[tok(d), pl.BlockSpec((1, 6, d), lambda b, i: (b, 0, 0)), tok(oa.shape[-1])]
                 + ob_specs + [tok(d), tok(d)]
                 + [const(w.shape) for w in (wpa, wpb, wo, wr, br)],
        out_specs=[tok(d), tok(d), tok(ROUTER_W)],
        out_shape=[jax.ShapeDtypeStruct((bsz, seq, d), F32),
                   jax.ShapeDtypeStruct((bsz, seq, d), BF16),
                   jax.ShapeDtypeStruct((bsz, seq, ROUTER_W), F32)],
        scratch_shapes=[pltpu.VMEM((B_GROUP_W // LANES, tm, LANES), F32)] * 4,
        compiler_params=pltpu.CompilerParams(dimension_semantics=("parallel", "parallel")),
        name="merge_proj",
    )(x, mod_l, oa, *ob_args, ga, gb, wpa, wpb, wo, wr, br)


MOE_BLOCK = 1024
SEG_ALIGN = 16
ROW_CHUNK = 128
PERM_CHUNK = 256
GATHER_CHUNK = 512
EXPERTS_PER_STEP = 4
SORTED_ROWS = -(-(2 * MOE_BLOCK + N_EXPERTS * (SEG_ALIGN - 1)) // GATHER_CHUNK) * GATHER_CHUNK
SORTED_ROWS_ALLOC = SORTED_ROWS + ROW_CHUNK


def _plan_kernel(route_ref, before_ref, posc_ref, posr_ref, base_ref, npad_ref):
    r = route_ref[0]
    lane = lax.broadcasted_iota(jnp.int32, r.shape, 1).astype(F32)
    oh1 = jnp.where(lane == r[:, 0:1], 1.0, 0.0)
    oh2 = jnp.where(lane == r[:, 1:2], 1.0, 0.0)
    cnt1 = jnp.sum(oh1, axis=0, keepdims=True)
    cnt2 = jnp.sum(oh2, axis=0, keepdims=True)
    npad = jnp.floor((cnt1 + cnt2 + (SEG_ALIGN - 1)) * (1.0 / SEG_ALIGN)) * SEG_ALIGN
    ri = lax.broadcasted_iota(jnp.int32, (LANES, LANES), 0)
    ci = lax.broadcasted_iota(jnp.int32, (LANES, LANES), 1)
    upper = jnp.where(ri < ci, 1.0, 0.0)
    npad8 = jnp.broadcast_to(npad, (8, LANES))
    base = jnp.dot(npad8, upper, precision=HIGHEST, preferred_element_type=F32)[0:1]

    pre = _dot(before_ref[...], jnp.concatenate([oh1, oh2], axis=1).astype(BF16))
    pre1, pre2 = pre[:, :LANES], pre[:, LANES:]
    pos1 = jnp.sum(oh1 * (base + pre1), axis=-1, keepdims=True)
    pos2 = jnp.sum(oh2 * (base + cnt1 + pre2), axis=-1, keepdims=True)
    packed = jnp.where(lane == 0.0, pos1, jnp.where(lane == 1.0, pos2, jnp.where(lane >= 2.0, r, 0.0)))
    posc_ref[0] = packed
    posr_ref[0] = packed.T[0:8, :]
    base_ref[0] = base
    npad_ref[0] = npad


def _moe_plan(route):
    n_blk, tb, _ = route.shape
    vec = pl.BlockSpec((1, 1, LANES), lambda i: (i, 0, 0))
    before = jnp.tril(jnp.ones((tb, tb), BF16), -1)
    return pl.pallas_call(
        _plan_kernel,
        grid=(n_blk,),
        in_specs=[pl.BlockSpec((1, tb, ROUTER_W), lambda i: (i, 0, 0)),
                  pl.BlockSpec((tb, tb), lambda i: (0, 0), pipeline_mode=pl.Buffered(1))],
        out_specs=[pl.BlockSpec((1, tb, LANES), lambda i: (i, 0, 0)),
                   pl.BlockSpec((1, 8, tb), lambda i: (i, 0, 0)), vec, vec],
        out_shape=[jax.ShapeDtypeStruct((n_blk, tb, LANES), F32),
                   jax.ShapeDtypeStruct((n_blk, 8, tb), F32),
                   jax.ShapeDtypeStruct((n_blk, 1, LANES), F32),
                   jax.ShapeDtypeStruct((n_blk, 1, LANES), F32)],
        compiler_params=pltpu.CompilerParams(dimension_semantics=("parallel",)),
        name="moe_plan",
    )(route, before)


def _moe_kernel(base_ref, npad_ref, x1_ref, mod_ref, h_ref, posc_ref, posr_ref, wg_ref, wu_ref, wd_ref,
                o_ref, xs_ref, ys_ref, gathered_ref):
    blk = pl.program_id(0)
    e = pl.program_id(1)
    tb = h_ref.shape[1]

    n_here = wg_ref.shape[0]
    n_experts = pl.num_programs(1) * n_here

    @pl.when(e == 0)
    def _():
        gathered_ref[0] = 0
        xs_ref[SORTED_ROWS:, :] = jnp.zeros((ROW_CHUNK, xs_ref.shape[1]), BF16)
        ys_ref[...] = jnp.zeros(ys_ref.shape, BF16)

    nxt = jnp.minimum((e + 2) * n_here, n_experts)
    need_rows = jnp.minimum(base_ref[blk, nxt] + ROW_CHUNK, SORTED_ROWS)
    need_chunks = (need_rows + GATHER_CHUNK - 1) // GATHER_CHUNK

    def gather_chunk(c, carry):
        r0 = pl.multiple_of(c * GATHER_CHUNK, GATHER_CHUNK)
        rid = (r0 + lax.broadcasted_iota(jnp.int32, (GATHER_CHUNK, tb), 0)).astype(F32)
        pr = posr_ref[0]
        sel = jnp.where(rid == pr[0:1], 1.0, jnp.where(rid == pr[1:2], 1.0, 0.0)).astype(BF16)
        xs_ref[pl.ds(r0, GATHER_CHUNK), :] = _dot(sel, h_ref[0]).astype(BF16)
        return carry

    lax.fori_loop(gathered_ref[0], need_chunks, gather_chunk, 0)
    gathered_ref[0] = jnp.maximum(gathered_ref[0], need_chunks)

    def gate_up(r0, j):
        xc = xs_ref[pl.ds(r0, ROW_CHUNK), :]
        return _dot(xc, wg_ref[j]), _dot(xc, wu_ref[j])

    def down(r0, j, a, u):
        hid = (a * _sigmoid(a)) * u
        ys_ref[pl.ds(r0, ROW_CHUNK), :] = _dot(hid.astype(BF16), wd_ref[j]).astype(BF16)

    starts = [pl.multiple_of(base_ref[blk, e * n_here + j], SEG_ALIGN) for j in range(n_here)]
    first = [gate_up(starts[j], j) for j in range(n_here)]
    for j in range(n_here):
        down(starts[j], j, *first[j])
    for j in range(n_here):
        seg_rows = npad_ref[blk, e * n_here + j]

        def more(c, carry, j=j, seg_rows=seg_rows):
            r0 = starts[j] + jnp.minimum(c * ROW_CHUNK, seg_rows - ROW_CHUNK)
            r0 = pl.multiple_of(r0, SEG_ALIGN)
            down(r0, j, *gate_up(r0, j))
            return carry

        lax.fori_loop(1, (seg_rows + ROW_CHUNK - 1) // ROW_CHUNK, more, 0)

    @pl.when(e == pl.num_programs(1) - 1)
    def _():
        ys = ys_ref[:SORTED_ROWS, :]
        for c in range(tb // PERM_CHUNK):
            pc = posc_ref[0, c * PERM_CHUNK:(c + 1) * PERM_CHUNK, :]
            rid = lax.broadcasted_iota(jnp.int32, (PERM_CHUNK, SORTED_ROWS), 1).astype(F32)
            wsel = (jnp.where(rid == pc[:, 0:1], pc[:, 2:3], 0.0)
                    + jnp.where(rid == pc[:, 1:2], pc[:, 3:4], 0.0)).astype(BF16)
            y = _dot(wsel, ys)
            rows = slice(c * PERM_CHUNK, (c + 1) * PERM_CHUNK)
            o_ref[0, rows, :] = x1_ref[0, rows, :] + mod_ref[0, 5:6, :] * y


def _moe(x1, mod_l, h2, route, weg, weu, wed, layer):
    bsz, seq, d = x1.shape
    n_e, d_e = N_EXPERTS, weg.shape[-1]
    e0 = layer * (n_e // EXPERTS_PER_STEP)
    tb = min(MOE_BLOCK, seq)
    assert tb == MOE_BLOCK, "sorted-row capacity is sized for MOE_BLOCK tokens"
    per_batch = seq // tb
    n_blk = bsz * per_batch
    blocked = lambda a: a.reshape(n_blk, tb, a.shape[-1])
    posc, posr, base, npad = _moe_plan(blocked(route))
    base_i = base.reshape(n_blk, LANES).astype(jnp.int32)
    npad_i = npad.reshape(n_blk, LANES).astype(jnp.int32)

    tok = lambda w: pl.BlockSpec((1, tb, w), lambda i, e, *_: (i, 0, 0))
    grid_spec = pltpu.PrefetchScalarGridSpec(
        num_scalar_prefetch=2,
        grid=(n_blk, n_e // EXPERTS_PER_STEP),
        in_specs=[tok(d),
                  pl.BlockSpec((1, 6, d), lambda i, e, *_: (i // per_batch, 0, 0)),
                  tok(d), tok(LANES),
                  pl.BlockSpec((1, 8, tb), lambda i, e, *_: (i, 0, 0)),
                  pl.BlockSpec((EXPERTS_PER_STEP, d, d_e), lambda i, e, *_: (e0 + e, 0, 0)),
                  pl.BlockSpec((EXPERTS_PER_STEP, d, d_e), lambda i, e, *_: (e0 + e, 0, 0)),
                  pl.BlockSpec((EXPERTS_PER_STEP, d_e, d), lambda i, e, *_: (e0 + e, 0, 0))],
        out_specs=tok(d),
        scratch_shapes=[pltpu.VMEM((SORTED_ROWS_ALLOC, d), BF16),
                        pltpu.VMEM((SORTED_ROWS_ALLOC, d), BF16),
                        pltpu.SMEM((1,), jnp.int32)],
    )
    out = pl.pallas_call(
        _moe_kernel,
        grid_spec=grid_spec,
        out_shape=jax.ShapeDtypeStruct((n_blk, tb, d), F32),
        compiler_params=pltpu.CompilerParams(dimension_semantics=("parallel", "arbitrary")),
        name="moe_experts",
    )(base_i, npad_i, blocked(x1), mod_l, blocked(h2), posc, posr, weg, weu, wed)
    return out.reshape(bsz, seq, d)


def _tiles(seq):
    return dict(
        ts_rope=min(1024, seq),
        tm_proj=min(512, seq),
        tq_a=min(2048, seq),
        tk_a=min(1024, seq),
        tq_b=512,
        tm_merge=min(512, seq),
    )


def kernel(x, c, positions, w_ada, b_ada, w_in, qn_a, kn_a, lam_q1, lam_k1, lam_q2, lam_k2,
           subln_a, qn_b, kn_b, w_pa, w_pb, w_o, w_r1, b_r1, w_r2, b_r2,
           w_e_gate, w_e_up, w_e_down):
    depth = w_ada.shape[0]
    bsz, seq, d = x.shape
    t = _tiles(seq)

    mod = _ada(c, w_ada, b_ada).reshape(depth, bsz, 6, d)
    cos_l, sin_l = _rope_tables(positions, t["ts_rope"])
    seg = jnp.kron(jnp.eye(256 // HEAD_DIM, dtype=F32),
                   jnp.full((HEAD_DIM, HEAD_DIM), 1.0 / HEAD_DIM, F32)).astype(BF16)
    q_scale = HEAD_DIM ** -0.5 * LOG2E

    w_in_b = w_in.astype(BF16)
    wvat_b = w_in[:, :, IN_COL_VA:IN_COL_VA + 512].transpose(0, 2, 1).astype(BF16)
    experts = lambda a: a.astype(BF16).reshape((-1,) + a.shape[2:])
    weg_b, weu_b, wed_b = experts(w_e_gate), experts(w_e_up), experts(w_e_down)

    for layer in range(depth):
        lam_init = 0.8 - 0.6 * math.exp(-0.3 * layer)
        gain = jnp.concatenate([
            jnp.tile(qn_a[layer] * q_scale, 8), jnp.tile(kn_a[layer], 8),
            jnp.tile(qn_b[layer] * q_scale, 12), jnp.tile(kn_b[layer], 12)]).reshape(1, -1)

        qa, ka, qb, kb, vat, vb, ga, gb = _inproj(x, mod[layer], cos_l, sin_l, gain, w_in_b, wvat_b, seg,
                                                  layer, t["tm_proj"])

        score_bound = (1.01 * HEAD_DIM * q_scale * jnp.max(jnp.abs(qn_a[layer]))
                       * jnp.max(jnp.abs(kn_a[layer]))).reshape(1)
        lam_p = jnp.stack([lam_q1[layer], lam_k1[layer], lam_q2[layer], lam_k2[layer]])
        oa = _attn_a(score_bound, lam_p, subln_a[layer].reshape(1, -1), qa, ka, vat, lam_init,
                     t["tq_a"], t["tk_a"])
        obs = [_attn_b(qb[g], kb[g], vb[g], g, t["tq_b"]) for g in range(B_GROUPS)]

        wr = jnp.zeros((d, ROUTER_W), F32)
        wr = wr.at[:, :N_EXPERTS].set(w_r2[layer]).at[:, N_EXPERTS:N_EXPERTS + N_GROUPS].set(w_r1[layer])
        br = jnp.zeros((1, ROUTER_W), F32)
        br = br.at[0, :N_EXPERTS].set(b_r2[layer]).at[0, N_EXPERTS:N_EXPERTS + N_GROUPS].set(b_r1[layer])
        wr_hi = wr.astype(BF16)
        wr2 = jnp.concatenate([wr_hi, (wr - wr_hi.astype(F32)).astype(BF16)], axis=1)
        x1, h2, route = _merge(x, mod[layer], oa, obs, ga, gb,
                               w_pa[layer].astype(BF16), w_pb[layer].astype(BF16), w_o[layer].astype(BF16),
                               wr2, br, t["tm_merge"])

        x = _moe(x1, mod[layer], h2, route, weg_b, weu_b, wed_b, layer)
    return x
```

```python
import functools
import math

import jax
import jax.numpy as jnp
from jax import lax
from jax.experimental import pallas as pl
from jax.experimental.pallas import tpu as pltpu

EPS = 1e-6
ROPE_THETA = 10000.0
LOG2E = math.log2(math.e)
LN2 = math.log(2.0)

A_HEADS = 4
HEAD_DIM = 64
LANES = 128
B_PAIRS = ((128, 1), (512, 4), (2048, 16))
B_GROUPS = len(B_PAIRS)
B_GROUP_W = 256
N_GROUPS = 4
EXPERTS_PER_GROUP = 8
N_EXPERTS = N_GROUPS * EXPERTS_PER_GROUP
ROUTER_W = 128

CHUNK = 256
A_QK_W = A_HEADS * 2 * HEAD_DIM
A_V_W = A_HEADS * LANES

IN_COL_VA = 2 * A_QK_W
IN_COL_QB = IN_COL_VA + A_V_W
IN_COL_KB = IN_COL_QB + B_GROUPS * B_GROUP_W
IN_COL_VB = IN_COL_KB + B_GROUPS * B_GROUP_W
IN_COL_GATES = IN_COL_VB + B_GROUPS * B_GROUP_W
IN_COLS_QK = (tuple(range(0, 2 * A_QK_W, CHUNK))
              + tuple(IN_COL_QB + g * B_GROUP_W for g in range(B_GROUPS))
              + tuple(IN_COL_KB + g * B_GROUP_W for g in range(B_GROUPS)))

F32 = jnp.float32
BF16 = jnp.bfloat16
HIGHEST = lax.Precision.HIGHEST


def _dot(a, b):
    return jnp.dot(a, b, preferred_element_type=F32)


def _dot_nt(a, b):
    return lax.dot_general(a, b, (((1,), (1,)), ((), ())), preferred_element_type=F32)


def _sigmoid(x):
    return 1.0 / (1.0 + jnp.exp(-x))


def _ada_kernel(c_ref, w_ref, b_ref, o_ref):
    w = w_ref[0]
    for b in range(c_ref.shape[0]):
        c = c_ref[b]
        c_act = c * _sigmoid(c)
        o_ref[0, b:b + 1, :] = jnp.sum(c_act * w, axis=0, keepdims=True) + b_ref[0]


def _ada(c, w_ada, b_ada):
    depth, d, six_d = w_ada.shape
    bsz = c.shape[0]
    n_col = six_d // d
    return pl.pallas_call(
        _ada_kernel,
        grid=(depth, n_col),
        in_specs=[
            pl.BlockSpec((bsz, d, 1), lambda l, j: (0, 0, 0)),
            pl.BlockSpec((1, d, d), lambda l, j: (l, 0, j)),
            pl.BlockSpec((1, 1, d), lambda l, j: (l, 0, j)),
        ],
        out_specs=pl.BlockSpec((1, bsz, d), lambda l, j: (l, 0, j)),
        out_shape=jax.ShapeDtypeStruct((depth, bsz, six_d), F32),
        name="ada_mod",
    )(c.reshape(bsz, d, 1), w_ada, b_ada.reshape(depth, 1, six_d))


def _rope_kernel(pos_ref, f_ref, cos_ref, sin_ref):
    ang = pos_ref[0].astype(F32) * f_ref[...]
    cos = jnp.cos(ang)
    sin = jnp.sin(ang)
    cos_ref[0] = jnp.concatenate([cos, cos, cos, cos], axis=0).T
    sin_ref[0] = jnp.concatenate([-sin, sin, -sin, sin], axis=0).T


def _rope_tables(positions, ts):
    bsz, seq = positions.shape
    half = HEAD_DIM // 2
    inv_freq = ROPE_THETA ** (-jnp.arange(0, HEAD_DIM, 2, dtype=F32) / HEAD_DIM)
    return pl.pallas_call(
        _rope_kernel,
        grid=(bsz, seq // ts),
        in_specs=[
            pl.BlockSpec((1, 1, ts), lambda b, i: (b, 0, i)),
            pl.BlockSpec((half, 1), lambda b, i: (0, 0)),
        ],
        out_specs=[pl.BlockSpec((1, ts, LANES), lambda b, i: (b, i, 0))] * 2,
        out_shape=[jax.ShapeDtypeStruct((bsz, seq, LANES), F32)] * 2,
        name="rope_tables",
    )(positions.reshape(bsz, 1, seq), inv_freq.reshape(half, 1))


def _inproj_kernel(x_ref, mod_ref, cos_ref, sin_ref, gain_ref, w_ref, wvat_ref, seg_ref,
                   qa_ref, ka_ref, qb0_ref, qb1_ref, qb2_ref, kb0_ref, kb1_ref, kb2_ref,
                   vat_ref, vb0_ref, vb1_ref, vb2_ref, ga_ref, gb_ref, stage_ref):
    x = x_ref[0]
    tm = x.shape[0]
    ms = jnp.mean(x * x, axis=-1, keepdims=True)
    h = x * lax.rsqrt(ms + EPS) * (1.0 + mod_ref[0, 1:2, :]) + mod_ref[0, 0:1, :]
    hb = h.astype(BF16)

    cos = cos_ref[0]
    sin = sin_ref[0]
    seg = seg_ref[...]
    lane = lax.broadcasted_iota(jnp.int32, cos.shape, 1)
    first_half = (lane % HEAD_DIM) < (HEAD_DIM // 2)

    def norm_rope(y2, c):
        msq = _dot((y2 * y2).astype(BF16), seg)
        yn2 = y2 * lax.rsqrt(msq + EPS) * gain_ref[:, c * CHUNK:(c + 1) * CHUNK]
        out = []
        for hf in range(2):
            yn = yn2[:, hf * LANES:(hf + 1) * LANES]
            partner = jnp.where(first_half, pltpu.roll(yn, 96, axis=1), pltpu.roll(yn, 32, axis=1))
            out.append(yn * cos + partner * sin)
        return out

    def store_dilated(ref, group, val):
        dil = B_PAIRS[group][1]
        if dil == 1:
            ref[0] = val.astype(BF16)
            return
        for hf in range(B_GROUP_W // LANES):
            stage_ref[hf] = val[:, hf * LANES:(hf + 1) * LANES]
        for r in range(dil):
            for hf in range(B_GROUP_W // LANES):
                rows = stage_ref[hf, pl.ds(r, tm // dil, stride=dil), :]
                col = r * B_GROUP_W + hf * LANES
                ref[0, :, col:col + LANES] = rows.astype(BF16)

    flat = [(qa_ref, 0), (qa_ref, 1), (ka_ref, 0), (ka_ref, 1)]
    grouped = [(qb0_ref, 0), (qb1_ref, 1), (qb2_ref, 2), (kb0_ref, 0), (kb1_ref, 1), (kb2_ref, 2)]
    def qk_epilogue(c, y2):
        halves = norm_rope(y2, c)
        if c < len(flat):
            ref, t = flat[c]
            for hf in range(2):
                col = t * CHUNK + hf * LANES
                ref[0, :, col:col + LANES] = halves[hf].astype(BF16)
        else:
            ref, group = grouped[c - len(flat)]
            store_dilated(ref, group, jnp.concatenate(halves, axis=1))

    def vat_epilogue(c, v):
        vat_ref[0, c * CHUNK:(c + 1) * CHUNK, :] = v.astype(BF16)

    gate_chunks = ga_ref.shape[2] // CHUNK

    def gate_epilogue(c, g):
        ref, t = (ga_ref, c) if c < gate_chunks else (gb_ref, c - gate_chunks)
        ref[0, :, t * CHUNK:(t + 1) * CHUNK] = _sigmoid(g).astype(BF16)

    proj = lambda col: _dot(hb, w_ref[0, :, col:col + CHUNK])
    heavy, light = [], []
    for c, col in enumerate(IN_COLS_QK):
        heavy.append((functools.partial(proj, col), functools.partial(qk_epilogue, c)))
    for c in range(wvat_ref.shape[1] // CHUNK):
        light.append((lambda c=c: _dot_nt(wvat_ref[0, c * CHUNK:(c + 1) * CHUNK, :], hb),
                      functools.partial(vat_epilogue, c)))
    for group, ref in enumerate((vb0_ref, vb1_ref, vb2_ref)):
        light.append((functools.partial(proj, IN_COL_VB + group * CHUNK),
                      functools.partial(store_dilated, ref, group)))
    for c in range((w_ref.shape[2] - IN_COL_GATES) // CHUNK):
        light.append((functools.partial(proj, IN_COL_GATES + c * CHUNK), functools.partial(gate_epilogue, c)))
    jobs = []
    for i in range(max(len(heavy), len(light))):
        jobs += light[i:i + 1] + heavy[i:i + 1]
    pending = None
    for matmul, epilogue in jobs:
        res = matmul()
        if pending is not None:
            pending[0](pending[1])
        pending = (epilogue, res)
    pending[0](pending[1])


def _inproj(x, mod_l, cos_l, sin_l, gain, w_all, wvat_all, seg, layer, tm):
    bsz, seq, d = x.shape
    a_v_w = wvat_all.shape[1]
    tok = lambda w: pl.BlockSpec((1, tm, w), lambda b, i: (b, i, 0))
    tok_t = pl.BlockSpec((1, a_v_w, tm), lambda b, i: (b, 0, i))
    const = lambda shape: pl.BlockSpec(shape, lambda b, i: (0,) * len(shape),
                                       pipeline_mode=pl.Buffered(1))
    of_layer = lambda a: pl.BlockSpec((1,) + a.shape[1:], lambda b, i: (layer, 0, 0),
                                      pipeline_mode=pl.Buffered(1))
    row = lambda w: jax.ShapeDtypeStruct((bsz, seq, w), BF16)
    dil_specs = [pl.BlockSpec((1, tm // dl, dl * B_GROUP_W), lambda b, i: (b, i, 0)) for _, dl in B_PAIRS]
    dil_shapes = [jax.ShapeDtypeStruct((bsz, seq // dl, dl * B_GROUP_W), BF16) for _, dl in B_PAIRS]
    outs = pl.pallas_call(
        _inproj_kernel,
        grid=(bsz, seq // tm),
        in_specs=[
            tok(d),
            pl.BlockSpec((1, 6, d), lambda b, i: (b, 0, 0)),
            tok(LANES), tok(LANES),
            const(gain.shape), of_layer(w_all), of_layer(wvat_all), const(seg.shape),
        ],
        out_specs=[tok(A_QK_W), tok(A_QK_W)] + dil_specs + dil_specs + [tok_t] + dil_specs + [tok(d), tok(d)],
        out_shape=[row(A_QK_W), row(A_QK_W)] + dil_shapes + dil_shapes
                  + [jax.ShapeDtypeStruct((bsz, a_v_w, seq), BF16)] + dil_shapes + [row(d), row(d)],
        scratch_shapes=[pltpu.VMEM((B_GROUP_W // LANES, tm, LANES), F32)],
        compiler_params=pltpu.CompilerParams(dimension_semantics=("parallel", "parallel")),
        name="in_proj",
    )(x, mod_l, cos_l, sin_l, gain, w_all, wvat_all, seg)
    qa, ka = outs[0], outs[1]
    qb, kb, vat, vb, ga, gb = outs[2:5], outs[5:8], outs[8], outs[9:12], outs[12], outs[13]
    return qa, ka, qb, kb, vat, vb, ga, gb


SCORE_BOUND_NO_SHIFT = 64.0
KV_CHUNK = 256


def _attn_a_kernel(bound_ref, lam_ref, sub_ref, q_ref, k_ref, vt_ref, o_ref,
                   m_ref, l_ref, acc_ref, pa_ref, pb_ref, *, tk, lam_init):
    q = q_ref[0]
    tq = q.shape[0]
    seq = k_ref.shape[1]
    lane = lax.broadcasted_iota(jnp.int32, q.shape, 1)
    zero = jnp.zeros_like(q)
    q_maps = (jnp.where(lane < HEAD_DIM, q, zero), jnp.where(lane >= HEAD_DIM, q, zero))
    n_chunk = tk // KV_CHUNK

    acc_ref[...] = jnp.zeros(acc_ref.shape, F32)
    l_ref[...] = jnp.zeros(l_ref.shape, F32)

    def scores_exp(tile, p_ref):
        start = pl.multiple_of(tile * tk, tk)
        k = k_ref[0, pl.ds(start, tk), :]
        for mi in range(2):
            p = jnp.exp2(_dot_nt(k, q_maps[mi]))
            l_ref[mi] += jnp.sum(p.reshape(tk // 8, 8, tq), axis=0)
            p_ref[mi] = p.astype(BF16)

    def weighted_values(tile, p_ref):
        start = pl.multiple_of(tile * tk, tk)
        vt = vt_ref[0, :, pl.ds(start, tk)]
        for mi in range(2):
            acc_ref[mi] += _dot(vt, p_ref[mi])

    n_tiles = seq // tk

    def no_shift_pair(jj, carry):
        t = 2 * jj
        scores_exp(t + 1, pb_ref)
        weighted_values(t, pa_ref)
        scores_exp(t + 2, pa_ref)
        weighted_values(t + 1, pb_ref)
        return carry

    def no_shift_loop():
        scores_exp(0, pa_ref)
        lax.fori_loop(0, n_tiles // 2 - 1, no_shift_pair, 0)
        scores_exp(n_tiles - 1, pb_ref)
        weighted_values(n_tiles - 2, pa_ref)
        weighted_values(n_tiles - 1, pb_ref)

    def online_max_body(j, carry):
        for c in range(n_chunk):
            start = pl.multiple_of(j * tk + c * KV_CHUNK, KV_CHUNK)
            k = k_ref[0, pl.ds(start, KV_CHUNK), :]
            vt = vt_ref[0, :, pl.ds(start, KV_CHUNK)]
            for mi in range(2):
                s = _dot_nt(k, q_maps[mi])
                m_old = m_ref[mi]
                m_new = jnp.maximum(m_old, jnp.max(s, axis=0, keepdims=True))
                alpha = jnp.exp2(m_old - m_new)
                p = jnp.exp2(s - m_new[0:1])
                l_ref[mi] = alpha * l_ref[mi] + jnp.sum(p.reshape(KV_CHUNK // 8, 8, tq), axis=0)
                acc_ref[mi] = alpha[0:1] * acc_ref[mi] + _dot(vt, p.astype(BF16))
                m_ref[mi] = m_new
        return carry

    no_shift = bound_ref[0] <= SCORE_BOUND_NO_SHIFT

    @pl.when(no_shift)
    def _():
        no_shift_loop()

    @pl.when(jnp.logical_not(no_shift))
    def _():
        m_ref[...] = jnp.full(m_ref.shape, -jnp.inf, F32)
        lax.fori_loop(0, seq // tk, online_max_body, 0)

    lam_p = lam_ref[...]
    s1 = jnp.sum(lam_p[0:1] * lam_p[1:2], axis=-1, keepdims=True)
    s2 = jnp.sum(lam_p[2:3] * lam_p[3:4], axis=-1, keepdims=True)
    lam = jnp.exp(s1) - jnp.exp(s2) + lam_init
    l0 = jnp.sum(l_ref[0], axis=0, keepdims=True)
    l1 = jnp.sum(l_ref[1], axis=0, keepdims=True)
    ot = acc_ref[0] / l0 - lam * (acc_ref[1] / l1)
    o = ot.T
    msq = jnp.mean(o * o, axis=-1, keepdims=True)
    o = o * lax.rsqrt(msq + EPS) * sub_ref[...] * (1.0 - lam_init)
    o_ref[0] = o.astype(BF16)


def _attn_a(score_bound, lam_p, subln, qa, ka, vat, lam_init, tq, tk):
    bsz, seq, _ = qa.shape
    return pl.pallas_call(
        functools.partial(_attn_a_kernel, tk=tk, lam_init=lam_init),
        grid=(bsz, A_HEADS, seq // tq),
        in_specs=[
            pl.BlockSpec(memory_space=pltpu.SMEM),
            pl.BlockSpec(lam_p.shape, lambda b, h, i: (0, 0)),
            pl.BlockSpec(subln.shape, lambda b, h, i: (0, 0)),
            pl.BlockSpec((1, tq, LANES), lambda b, h, i: (b, i, h)),
            pl.BlockSpec((1, seq, LANES), lambda b, h, i: (b, 0, h)),
            pl.BlockSpec((1, LANES, seq), lambda b, h, i: (b, h, 0)),
        ],
        out_specs=pl.BlockSpec((1, tq, LANES), lambda b, h, i: (b, i, h)),
        out_shape=jax.ShapeDtypeStruct((bsz, seq, A_HEADS * LANES), BF16),
        scratch_shapes=[
            pltpu.VMEM((2, 8, tq), F32),
            pltpu.VMEM((2, 8, tq), F32),
            pltpu.VMEM((2, LANES, tq), F32),
            pltpu.VMEM((2, tk, tq), BF16),
            pltpu.VMEM((2, tk, tq), BF16),
        ],
        compiler_params=pltpu.CompilerParams(dimension_semantics=("parallel", "parallel", "parallel")),
        name="diff_attn",
    )(score_bound, lam_p, subln, qa, ka, vat)


def _attn_b_kernel(q_ref, k_ref, v_ref, o_ref, lse_ref, *, radius):
    length = k_ref.shape[1]
    sub = 2 * radius
    win = sub + 2 * radius
    n_sub = q_ref.shape[1] // sub
    lane = lax.broadcasted_iota(jnp.int32, (sub, LANES), 1)
    low = lane < HEAD_DIM
    tile = lambda a, c: a[:, c * LANES:(c + 1) * LANES]

    units, windows, scores = [], [], []
    for sb in range(n_sub):
        t0 = pl.program_id(2) * (n_sub * sub) + sb * sub
        start = pl.multiple_of(jnp.clip(t0 - radius, 0, length - win), radius)
        kw = k_ref[0, pl.ds(start, win), :]
        windows.append(v_ref[0, pl.ds(start, win), :])
        q = q_ref[0, sb * sub:(sb + 1) * sub, :]
        qpos = t0 + lax.broadcasted_iota(jnp.int32, (sub, win), 0)
        kpos = start + lax.broadcasted_iota(jnp.int32, (sub, win), 1)
        valid = jnp.abs(kpos - qpos) <= radius
        for c in range(B_GROUP_W // LANES):
            for half in range(2):
                qc = tile(q, c)
                qm = jnp.where(low if half == 0 else jnp.logical_not(low), qc, jnp.zeros_like(qc))
                units.append((sb, c))
                scores.append(jnp.where(valid, _dot_nt(qm, tile(kw, c)), -jnp.inf))
    probs, sums, lses = [], [], []
    for s in scores:
        m = jnp.max(s, axis=-1, keepdims=True)
        p = jnp.exp2(s - m)
        l = jnp.sum(p, axis=-1, keepdims=True)
        probs.append(p.astype(BF16))
        sums.append(l)
        lses.append((m + jnp.log2(l)) * LN2)
    outs = [_dot(probs[i], tile(windows[sb], c)) / sums[i] for i, (sb, c) in enumerate(units)]
    for i in range(0, len(units), 2):
        sb, c = units[i]
        rows, cols = slice(sb * sub, (sb + 1) * sub), slice(c * LANES, (c + 1) * LANES)
        o_ref[0, rows, cols] = jnp.where(low, outs[i], outs[i + 1]).astype(BF16)
        lse_ref[0, rows, cols] = jnp.where(low, lses[i], lses[i + 1])


def _attn_b(qg, kg, vg, group, tq):
    window, dilation = B_PAIRS[group]
    radius = window // (2 * dilation)
    bsz, length, _ = qg.shape
    tq = min(tq, length)
    assert length >= 4 * radius and tq % (2 * radius) == 0
    return pl.pallas_call(
        functools.partial(_attn_b_kernel, radius=radius),
        grid=(bsz, dilation, length // tq),
        in_specs=[
            pl.BlockSpec((1, tq, B_GROUP_W), lambda b, r, i: (b, i, r)),
            pl.BlockSpec((1, length, B_GROUP_W), lambda b, r, i: (b, 0, r)),
            pl.BlockSpec((1, length, B_GROUP_W), lambda b, r, i: (b, 0, r)),
        ],
        out_specs=[pl.BlockSpec((1, tq, B_GROUP_W), lambda b, r, i: (b, i, r))] * 2,
        out_shape=[jax.ShapeDtypeStruct((bsz, length, dilation * B_GROUP_W), BF16),
                   jax.ShapeDtypeStruct((bsz, length, dilation * B_GROUP_W), F32)],
        compiler_params=pltpu.CompilerParams(dimension_semantics=("parallel", "parallel", "parallel")),
        name=f"band_attn_g{group}",
    )(qg, kg, vg)


def _merge_kernel(x_ref, mod_ref, oa_ref, ob0_ref, ls0_ref, ob1_ref, ls1_ref, ob2_ref, ls2_ref,
                  ga_ref, gb_ref, wpa_ref, wpb_ref, wo_ref, wr2_ref, br_ref,
                  x1_ref, h2_ref, route_ref, so1_ref, sl1_ref, so2_ref, sl2_ref):
    tm = x_ref.shape[1]

    def token_major(o_ref, l_ref, group, so_ref, sl_ref):
        dil = B_PAIRS[group][1]
        n_hf = B_GROUP_W // LANES
        for r in range(dil):
            for hf in range(n_hf):
                cols = slice(r * B_GROUP_W + hf * LANES, r * B_GROUP_W + (hf + 1) * LANES)
                so_ref[hf, pl.ds(r, tm // dil, stride=dil), :] = o_ref[0, :, cols].astype(F32)
                sl_ref[hf, pl.ds(r, tm // dil, stride=dil), :] = l_ref[0, :, cols]
        return (jnp.concatenate([so_ref[hf] for hf in range(n_hf)], axis=1),
                jnp.concatenate([sl_ref[hf] for hf in range(n_hf)], axis=1))

    o0, ls0 = ob0_ref[0].astype(F32), ls0_ref[0]
    o1, ls1 = token_major(ob1_ref, ls1_ref, 1, so1_ref, sl1_ref)
    o2, ls2 = token_major(ob2_ref, ls2_ref, 2, so2_ref, sl2_ref)
    mx = jnp.maximum(jnp.maximum(ls0, ls1), ls2)
    e0, e1, e2 = jnp.exp(ls0 - mx), jnp.exp(ls1 - mx), jnp.exp(ls2 - mx)
    ob = ((e0 * o0 + e1 * o1 + e2 * o2) / (e0 + e1 + e2)).astype(BF16)

    n_part = 2
    rows = [slice(p * (tm // n_part), (p + 1) * (tm // n_part)) for p in range(n_part)]
    pa = [_dot(oa_ref[0, r, :], wpa_ref[...]) for r in rows]
    pb = [_dot(ob[r], wpb_ref[...]) for r in rows]
    merged = [(ga_ref[0, r, :].astype(F32) * pa[p] + gb_ref[0, r, :].astype(F32) * pb[p]).astype(BF16)
              for p, r in enumerate(rows)]
    y = [_dot(merged[p], wo_ref[...]) for p in range(n_part)]
    for p, r in enumerate(rows):
        x1 = x_ref[0, r, :] + mod_ref[0, 2:3, :] * y[p]
        x1_ref[0, r, :] = x1
        ms = jnp.mean(x1 * x1, axis=-1, keepdims=True)
        h2 = x1 * lax.rsqrt(ms + EPS) * (1.0 + mod_ref[0, 4:5, :]) + mod_ref[0, 3:4, :]
        h2_hi = h2.astype(BF16)
        h2_ref[0, r, :] = h2_hi
        h2_lo = (h2 - h2_hi.astype(F32)).astype(BF16)
        both = _dot(h2_hi, wr2_ref[...])
        logits = (both[:, :ROUTER_W] + both[:, ROUTER_W:] + _dot(h2_lo, wr2_ref[:, :ROUTER_W])) + br_ref[...]
        route_ref[0, r, :] = _route(logits)


def _route(logits):
    lane = lax.broadcasted_iota(jnp.int32, logits.shape, 1)
    neg = -jnp.inf
    big = ROUTER_W
    is_grp = (lane >= N_EXPERTS) & (lane < N_EXPERTS + N_GROUPS)
    lg = jnp.where(is_grp, logits, neg)
    mg = jnp.max(lg, axis=-1, keepdims=True)
    g_lane = jnp.min(jnp.where(lg == mg, lane, big), axis=-1, keepdims=True)
    g_val = 1.0 / jnp.sum(jnp.exp(lg - mg), axis=-1, keepdims=True)
    lo = (g_lane - N_EXPERTS) * EXPERTS_PER_GROUP
    in_grp = (lane >= lo) & (lane < lo + EXPERTS_PER_GROUP)
    le = jnp.where(in_grp, logits, neg)
    m1 = jnp.max(le, axis=-1, keepdims=True)
    i1 = jnp.min(jnp.where(le == m1, lane, big), axis=-1, keepdims=True)
    le2 = jnp.where(lane == i1, neg, le)
    m2 = jnp.max(le2, axis=-1, keepdims=True)
    i2 = jnp.min(jnp.where(le2 == m2, lane, big), axis=-1, keepdims=True)
    e = jnp.exp(m2 - m1)
    w1 = g_val / (1.0 + e)
    w2 = g_val * e / (1.0 + e)
    return jnp.where(lane == 0, i1.astype(F32),
                     jnp.where(lane == 1, i2.astype(F32),
                               jnp.where(lane == 2, w1, jnp.where(lane == 3, w2, 0.0))))


def _merge(x, mod_l, oa, obs, ga, gb, wpa, wpb, wo, wr, br, tm):
    bsz, seq, d = x.shape
    tok = lambda w: pl.BlockSpec((1, tm, w), lambda b, i: (b, i, 0))
    const = lambda shape: pl.BlockSpec(shape, lambda b, i: (0,) * len(shape),
                                       pipeline_mode=pl.Buffered(1))
    ob_args, ob_specs = [], []
    for (o, lse), (_, dl) in zip(obs, B_PAIRS):
        ob_args += [o, lse]
        ob_specs += [pl.BlockSpec((1, tm // dl, dl * B_GROUP_W), lambda b, i: (b, i, 0))] * 2
    return pl.pallas_call(
        _merge_kernel,
        grid=(bsz, seq // tm),
        in_specs=[tok(d), pl.BlockSpec((1, 6, d), lambda b, i: (b, 0, 0)), tok(oa.shape[-1])]
                 + ob_specs + [tok(d), tok(d)]
                 + [const(w.shape) for w in (wpa, wpb, wo, wr, br)],
        out_specs=[tok(d), tok(d), tok(ROUTER_W)],
        out_shape=[jax.ShapeDtypeStruct((bsz, seq, d), F32),
                   jax.ShapeDtypeStruct((bsz, seq, d), BF16),
                   jax.ShapeDtypeStruct((bsz, seq, ROUTER_W), F32)],
        scratch_shapes=[pltpu.VMEM((B_GROUP_W // LANES, tm, LANES), F32)] * 4,
        compiler_params=pltpu.CompilerParams(dimension_semantics=("parallel", "parallel")),
        name="merge_proj",
    )(x, mod_l, oa, *ob_args, ga, gb, wpa, wpb, wo, wr, br)


MOE_BLOCK = 1024
SEG_ALIGN = 16
ROW_CHUNK = 128
PERM_CHUNK = 256
GATHER_CHUNK = 512
EXPERTS_PER_STEP = 4
SORTED_ROWS = -(-(2 * MOE_BLOCK + N_EXPERTS * (SEG_ALIGN - 1)) // GATHER_CHUNK) * GATHER_CHUNK
SORTED_ROWS_ALLOC = SORTED_ROWS + ROW_CHUNK


def _plan_kernel(route_ref, before_ref, posc_ref, posr_ref, base_ref, npad_ref):
    r = route_ref[0]
    lane = lax.broadcasted_iota(jnp.int32, r.shape, 1).astype(F32)
    oh1 = jnp.where(lane == r[:, 0:1], 1.0, 0.0)
    oh2 = jnp.where(lane == r[:, 1:2], 1.0, 0.0)
    cnt1 = jnp.sum(oh1, axis=0, keepdims=True)
    cnt2 = jnp.sum(oh2, axis=0, keepdims=True)
    npad = jnp.floor((cnt1 + cnt2 + (SEG_ALIGN - 1)) * (1.0 / SEG_ALIGN)) * SEG_ALIGN
    ri = lax.broadcasted_iota(jnp.int32, (LANES, LANES), 0)
    ci = lax.broadcasted_iota(jnp.int32, (LANES, LANES), 1)
    upper = jnp.where(ri < ci, 1.0, 0.0)
    npad8 = jnp.broadcast_to(npad, (8, LANES))
    base = jnp.dot(npad8, upper, precision=HIGHEST, preferred_element_type=F32)[0:1]

    pre = _dot(before_ref[...], jnp.concatenate([oh1, oh2], axis=1).astype(BF16))
    pre1, pre2 = pre[:, :LANES], pre[:, LANES:]
    pos1 = jnp.sum(oh1 * (base + pre1), axis=-1, keepdims=True)
    pos2 = jnp.sum(oh2 * (base + cnt1 + pre2), axis=-1, keepdims=True)
    packed = jnp.where(lane == 0.0, pos1, jnp.where(lane == 1.0, pos2, jnp.where(lane >= 2.0, r, 0.0)))
    posc_ref[0] = packed
    posr_ref[0] = packed.T[0:8, :]
    base_ref[0] = base
    npad_ref[0] = npad


def _moe_plan(route):
    n_blk, tb, _ = route.shape
    vec = pl.BlockSpec((1, 1, LANES), lambda i: (i, 0, 0))
    before = jnp.tril(jnp.ones((tb, tb), BF16), -1)
    return pl.pallas_call(
        _plan_kernel,
        grid=(n_blk,),
        in_specs=[pl.BlockSpec((1, tb, ROUTER_W), lambda i: (i, 0, 0)),
                  pl.BlockSpec((tb, tb), lambda i: (0, 0), pipeline_mode=pl.Buffered(1))],
        out_specs=[pl.BlockSpec((1, tb, LANES), lambda i: (i, 0, 0)),
                   pl.BlockSpec((1, 8, tb), lambda i: (i, 0, 0)), vec, vec],
        out_shape=[jax.ShapeDtypeStruct((n_blk, tb, LANES), F32),
                   jax.ShapeDtypeStruct((n_blk, 8, tb), F32),
                   jax.ShapeDtypeStruct((n_blk, 1, LANES), F32),
                   jax.ShapeDtypeStruct((n_blk, 1, LANES), F32)],
        compiler_params=pltpu.CompilerParams(dimension_semantics=("parallel",)),
        name="moe_plan",
    )(route, before)


def _moe_kernel(base_ref, npad_ref, x1_ref, mod_ref, h_ref, posc_ref, posr_ref, wg_ref, wu_ref, wd_ref,
                o_ref, xs_ref, ys_ref, gathered_ref):
    blk = pl.program_id(0)
    e = pl.program_id(1)
    tb = h_ref.shape[1]

    n_here = wg_ref.shape[0]
    n_experts = pl.num_programs(1) * n_here

    @pl.when(e == 0)
    def _():
        gathered_ref[0] = 0
        xs_ref[SORTED_ROWS:, :] = jnp.zeros((ROW_CHUNK, xs_ref.shape[1]), BF16)
        ys_ref[...] = jnp.zeros(ys_ref.shape, BF16)

    nxt = jnp.minimum((e + 2) * n_here, n_experts)
    need_rows = jnp.minimum(base_ref[blk, nxt] + ROW_CHUNK, SORTED_ROWS)
    need_chunks = (need_rows + GATHER_CHUNK - 1) // GATHER_CHUNK

    def gather_chunk(c, carry):
        r0 = pl.multiple_of(c * GATHER_CHUNK, GATHER_CHUNK)
        rid = (r0 + lax.broadcasted_iota(jnp.int32, (GATHER_CHUNK, tb), 0)).astype(F32)
        pr = posr_ref[0]
        sel = jnp.where(rid == pr[0:1], 1.0, jnp.where(rid == pr[1:2], 1.0, 0.0)).astype(BF16)
        xs_ref[pl.ds(r0, GATHER_CHUNK), :] = _dot(sel, h_ref[0]).astype(BF16)
        return carry

    lax.fori_loop(gathered_ref[0], need_chunks, gather_chunk, 0)
    gathered_ref[0] = jnp.maximum(gathered_ref[0], need_chunks)

    def gate_up(r0, j):
        xc = xs_ref[pl.ds(r0, ROW_CHUNK), :]
        return _dot(xc, wg_ref[j]), _dot(xc, wu_ref[j])

    def down(r0, j, a, u):
        hid = (a * _sigmoid(a)) * u
        ys_ref[pl.ds(r0, ROW_CHUNK), :] = _dot(hid.astype(BF16), wd_ref[j]).astype(BF16)

    starts = [pl.multiple_of(base_ref[blk, e * n_here + j], SEG_ALIGN) for j in range(n_here)]
    first = [gate_up(starts[j], j) for j in range(n_here)]
    for j in range(n_here):
        down(starts[j], j, *first[j])
    for j in range(n_here):
        seg_rows = npad_ref[blk, e * n_here + j]

        def more(c, carry, j=j, seg_rows=seg_rows):
            r0 = starts[j] + jnp.minimum(c * ROW_CHUNK, seg_rows - ROW_CHUNK)
            r0 = pl.multiple_of(r0, SEG_ALIGN)
            down(r0, j, *gate_up(r0, j))
            return carry

        lax.fori_loop(1, (seg_rows + ROW_CHUNK - 1) // ROW_CHUNK, more, 0)

    @pl.when(e == pl.num_programs(1) - 1)
    def _():
        ys = ys_ref[:SORTED_ROWS, :]
        for c in range(tb // PERM_CHUNK):
            pc = posc_ref[0, c * PERM_CHUNK:(c + 1) * PERM_CHUNK, :]
            rid = lax.broadcasted_iota(jnp.int32, (PERM_CHUNK, SORTED_ROWS), 1).astype(F32)
            wsel = (jnp.where(rid == pc[:, 0:1], pc[:, 2:3], 0.0)
                    + jnp.where(rid == pc[:, 1:2], pc[:, 3:4], 0.0)).astype(BF16)
            y = _dot(wsel, ys)
            rows = slice(c * PERM_CHUNK, (c + 1) * PERM_CHUNK)
            o_ref[0, rows, :] = x1_ref[0, rows, :] + mod_ref[0, 5:6, :] * y


def _moe(x1, mod_l, h2, route, weg, weu, wed, layer):
    bsz, seq, d = x1.shape
    n_e, d_e = N_EXPERTS, weg.shape[-1]
    e0 = layer * (n_e // EXPERTS_PER_STEP)
    tb = min(MOE_BLOCK, seq)
    assert tb == MOE_BLOCK, "sorted-row capacity is sized for MOE_BLOCK tokens"
    per_batch = seq // tb
    n_blk = bsz * per_batch
    blocked = lambda a: a.reshape(n_blk, tb, a.shape[-1])
    posc, posr, base, npad = _moe_plan(blocked(route))
    base_i = base.reshape(n_blk, LANES).astype(jnp.int32)
    npad_i = npad.reshape(n_blk, LANES).astype(jnp.int32)

    tok = lambda w: pl.BlockSpec((1, tb, w), lambda i, e, *_: (i, 0, 0))
    grid_spec = pltpu.PrefetchScalarGridSpec(
        num_scalar_prefetch=2,
        grid=(n_blk, n_e // EXPERTS_PER_STEP),
        in_specs=[tok(d),
                  pl.BlockSpec((1, 6, d), lambda i, e, *_: (i // per_batch, 0, 0)),
                  tok(d), tok(LANES),
                  pl.BlockSpec((1, 8, tb), lambda i, e, *_: (i, 0, 0)),
                  pl.BlockSpec((EXPERTS_PER_STEP, d, d_e), lambda i, e, *_: (e0 + e, 0, 0)),
                  pl.BlockSpec((EXPERTS_PER_STEP, d, d_e), lambda i, e, *_: (e0 + e, 0, 0)),
                  pl.BlockSpec((EXPERTS_PER_STEP, d_e, d), lambda i, e, *_: (e0 + e, 0, 0))],
        out_specs=tok(d),
        scratch_shapes=[pltpu.VMEM((SORTED_ROWS_ALLOC, d), BF16),
                        pltpu.VMEM((SORTED_ROWS_ALLOC, d), BF16),
                        pltpu.SMEM((1,), jnp.int32)],
    )
    out = pl.pallas_call(
        _moe_kernel,
        grid_spec=grid_spec,
        out_shape=jax.ShapeDtypeStruct((n_blk, tb, d), F32),
        compiler_params=pltpu.CompilerParams(dimension_semantics=("parallel", "arbitrary")),
        name="moe_experts",
    )(base_i, npad_i, blocked(x1), mod_l, blocked(h2), posc, posr, weg, weu, wed)
    return out.reshape(bsz, seq, d)


def _tiles(seq):
    return dict(
        ts_rope=min(1024, seq),
        tm_proj=min(512, seq),
        tq_a=min(2048, seq),
        tk_a=min(1024, seq),
        tq_b=512,
        tm_merge=min(512, seq),
    )


def kernel(x, c, positions, w_ada, b_ada, w_in, qn_a, kn_a, lam_q1, lam_k1, lam_q2, lam_k2,
           subln_a, qn_b, kn_b, w_pa, w_pb, w_o, w_r1, b_r1, w_r2, b_r2,
           w_e_gate, w_e_up, w_e_down):
    depth = w_ada.shape[0]
    bsz, seq, d = x.shape
    t = _tiles(seq)

    mod = _ada(c, w_ada, b_ada).reshape(depth, bsz, 6, d)
    cos_l, sin_l = _rope_tables(positions, t["ts_rope"])
    seg = jnp.kron(jnp.eye(CHUNK // HEAD_DIM, dtype=F32),
                   jnp.full((HEAD_DIM, HEAD_DIM), 1.0 / HEAD_DIM, F32)).astype(BF16)
    q_scale = HEAD_DIM ** -0.5 * LOG2E

    w_in_b = w_in.astype(BF16)
    wvat_b = w_in[:, :, IN_COL_VA:IN_COL_VA + A_V_W].transpose(0, 2, 1).astype(BF16)
    experts = lambda a: a.astype(BF16).reshape((-1,) + a.shape[2:])
    weg_b, weu_b, wed_b = experts(w_e_gate), experts(w_e_up), experts(w_e_down)

    for layer in range(depth):
        lam_init = 0.8 - 0.6 * math.exp(-0.3 * layer)
        gain = jnp.concatenate([
            jnp.tile(qn_a[layer] * q_scale, 8), jnp.tile(kn_a[layer], 8),
            jnp.tile(qn_b[layer] * q_scale, 12), jnp.tile(kn_b[layer], 12)]).reshape(1, -1)

        qa, ka, qb, kb, vat, vb, ga, gb = _inproj(x, mod[layer], cos_l, sin_l, gain, w_in_b, wvat_b, seg,
                                                  layer, t["tm_proj"])

        score_bound = (1.01 * HEAD_DIM * q_scale * jnp.max(jnp.abs(qn_a[layer]))
                       * jnp.max(jnp.abs(kn_a[layer]))).reshape(1)
        lam_p = jnp.stack([lam_q1[layer], lam_k1[layer], lam_q2[layer], lam_k2[layer]])
        oa = _attn_a(score_bound, lam_p, subln_a[layer].reshape(1, -1), qa, ka, vat, lam_init,
                     t["tq_a"], t["tk_a"])
        obs = [_attn_b(qb[g], kb[g], vb[g], g, t["tq_b"]) for g in range(B_GROUPS)]

        wr = jnp.zeros((d, ROUTER_W), F32)
        wr = wr.at[:, :N_EXPERTS].set(w_r2[layer]).at[:, N_EXPERTS:N_EXPERTS + N_GROUPS].set(w_r1[layer])
        br = jnp.zeros((1, ROUTER_W), F32)
        br = br.at[0, :N_EXPERTS].set(b_r2[layer]).at[0, N_EXPERTS:N_EXPERTS + N_GROUPS].set(b_r1[layer])
        wr_hi = wr.astype(BF16)
        wr2 = jnp.concatenate([wr_hi, (wr - wr_hi.astype(F32)).astype(BF16)], axis=1)
        x1, h2, route = _merge(x, mod[layer], oa, obs, ga, gb,
                               w_pa[layer].astype(BF16), w_pb[layer].astype(BF16), w_o[layer].astype(BF16),
                               wr2, br, t["tm_merge"])

        x = _moe(x1, mod[layer], h2, route, weg_b, weu_b, wed_b, layer)
    return x
```

```python
import functools
import math

import jax
import jax.numpy as jnp
from jax import lax
from jax.experimental import pallas as pl
from jax.experimental.pallas import tpu as pltpu

EPS = 1e-6
ROPE_THETA = 10000.0
LOG2E = math.log2(math.e)
LN2 = math.log(2.0)

A_HEADS = 4
HEAD_DIM = 64
LANES = 128
B_PAIRS = ((128, 1), (512, 4), (2048, 16))
B_GROUPS = len(B_PAIRS)
B_GROUP_W = 256
N_GROUPS = 4
EXPERTS_PER_GROUP = 8
N_EXPERTS = N_GROUPS * EXPERTS_PER_GROUP
ROUTER_W = 128

CHUNK = 256
A_QK_W = A_HEADS * 2 * HEAD_DIM
A_V_W = A_HEADS * LANES

IN_COL_VA = 2 * A_QK_W
IN_COL_QB = IN_COL_VA + A_V_W
IN_COL_KB = IN_COL_QB + B_GROUPS * B_GROUP_W
IN_COL_VB = IN_COL_KB + B_GROUPS * B_GROUP_W
IN_COL_GATES = IN_COL_VB + B_GROUPS * B_GROUP_W
IN_COLS_QK = (tuple(range(0, 2 * A_QK_W, CHUNK))
              + tuple(IN_COL_QB + g * B_GROUP_W for g in range(B_GROUPS))
              + tuple(IN_COL_KB + g * B_GROUP_W for g in range(B_GROUPS)))

F32 = jnp.float32
BF16 = jnp.bfloat16
HIGHEST = lax.Precision.HIGHEST


def _dot(a, b):
    return jnp.dot(a, b, preferred_element_type=F32)


def _dot_nt(a, b):
    return lax.dot_general(a, b, (((1,), (1,)), ((), ())), preferred_element_type=F32)


def _sigmoid(x):
    return 1.0 / (1.0 + jnp.exp(-x))


def _ada_kernel(c_ref, w_ref, b_ref, o_ref):
    w = w_ref[0]
    for b in range(c_ref.shape[0]):
        c = c_ref[b]
        c_act = c * _sigmoid(c)
        o_ref[0, b:b + 1, :] = jnp.sum(c_act * w, axis=0, keepdims=True) + b_ref[0]


def _ada(c, w_ada, b_ada):
    depth, d, six_d = w_ada.shape
    bsz = c.shape[0]
    n_col = six_d // d
    return pl.pallas_call(
        _ada_kernel,
        grid=(depth, n_col),
        in_specs=[
            pl.BlockSpec((bsz, d, 1), lambda l, j: (0, 0, 0)),
            pl.BlockSpec((1, d, d), lambda l, j: (l, 0, j)),
            pl.BlockSpec((1, 1, d), lambda l, j: (l, 0, j)),
        ],
        out_specs=pl.BlockSpec((1, bsz, d), lambda l, j: (l, 0, j)),
        out_shape=jax.ShapeDtypeStruct((depth, bsz, six_d), F32),
        name="ada_mod",
    )(c.reshape(bsz, d, 1), w_ada, b_ada.reshape(depth, 1, six_d))


def _rope_kernel(pos_ref, f_ref, cos_ref, sin_ref):
    ang = pos_ref[0].astype(F32) * f_ref[...]
    cos = jnp.cos(ang)
    sin = jnp.sin(ang)
    cos_ref[0] = jnp.concatenate([cos, cos, cos, cos], axis=0).T
    sin_ref[0] = jnp.concatenate([-sin, sin, -sin, sin], axis=0).T


def _rope_tables(positions, ts):
    bsz, seq = positions.shape
    half = HEAD_DIM // 2
    inv_freq = ROPE_THETA ** (-jnp.arange(0, HEAD_DIM, 2, dtype=F32) / HEAD_DIM)
    return pl.pallas_call(
        _rope_kernel,
        grid=(bsz, seq // ts),
        in_specs=[
            pl.BlockSpec((1, 1, ts), lambda b, i: (b, 0, i)),
            pl.BlockSpec((half, 1), lambda b, i: (0, 0)),
        ],
        out_specs=[pl.BlockSpec((1, ts, LANES), lambda b, i: (b, i, 0))] * 2,
        out_shape=[jax.ShapeDtypeStruct((bsz, seq, LANES), F32)] * 2,
        name="rope_tables",
    )(positions.reshape(bsz, 1, seq), inv_freq.reshape(half, 1))


def _inproj_kernel(x_ref, mod_ref, cos_ref, sin_ref, gain_ref, w_ref, wvat_ref, seg_ref,
                   qa_ref, ka_ref, qb0_ref, qb1_ref, qb2_ref, kb0_ref, kb1_ref, kb2_ref,
                   vat_ref, vb0_ref, vb1_ref, vb2_ref, ga_ref, gb_ref, stage_ref):
    x = x_ref[0]
    tm = x.shape[0]
    ms = jnp.mean(x * x, axis=-1, keepdims=True)
    h = x * lax.rsqrt(ms + EPS) * (1.0 + mod_ref[0, 1:2, :]) + mod_ref[0, 0:1, :]
    hb = h.astype(BF16)

    cos = cos_ref[0]
    sin = sin_ref[0]
    seg = seg_ref[...]
    lane = lax.broadcasted_iota(jnp.int32, cos.shape, 1)
    first_half = (lane % HEAD_DIM) < (HEAD_DIM // 2)

    def norm_rope(y2, c):
        msq = _dot((y2 * y2).astype(BF16), seg)
        yn2 = y2 * lax.rsqrt(msq + EPS) * gain_ref[:, c * CHUNK:(c + 1) * CHUNK]
        out = []
        for hf in range(2):
            yn = yn2[:, hf * LANES:(hf + 1) * LANES]
            partner = jnp.where(first_half, pltpu.roll(yn, 96, axis=1), pltpu.roll(yn, 32, axis=1))
            out.append(yn * cos + partner * sin)
        return out

    def store_dilated(ref, group, val):
        dil = B_PAIRS[group][1]
        if dil == 1:
            ref[0] = val.astype(BF16)
            return
        for hf in range(B_GROUP_W // LANES):
            stage_ref[hf] = val[:, hf * LANES:(hf + 1) * LANES]
        for r in range(dil):
            for hf in range(B_GROUP_W // LANES):
                rows = stage_ref[hf, pl.ds(r, tm // dil, stride=dil), :]
                col = r * B_GROUP_W + hf * LANES
                ref[0, :, col:col + LANES] = rows.astype(BF16)

    flat = [(qa_ref, 0), (qa_ref, 1), (ka_ref, 0), (ka_ref, 1)]
    grouped = [(qb0_ref, 0), (qb1_ref, 1), (qb2_ref, 2), (kb0_ref, 0), (kb1_ref, 1), (kb2_ref, 2)]
    def qk_epilogue(c, y2):
        halves = norm_rope(y2, c)
        if c < len(flat):
            ref, t = flat[c]
            for hf in range(2):
                col = t * CHUNK + hf * LANES
                ref[0, :, col:col + LANES] = halves[hf].astype(BF16)
        else:
            ref, group = grouped[c - len(flat)]
            store_dilated(ref, group, jnp.concatenate(halves, axis=1))

    def vat_epilogue(c, v):
        vat_ref[0, c * CHUNK:(c + 1) * CHUNK, :] = v.astype(BF16)

    gate_chunks = ga_ref.shape[2] // CHUNK

    def gate_epilogue(c, g):
        ref, t = (ga_ref, c) if c < gate_chunks else (gb_ref, c - gate_chunks)
        ref[0, :, t * CHUNK:(t + 1) * CHUNK] = _sigmoid(g).astype(BF16)

    proj = lambda col: _dot(hb, w_ref[0, :, col:col + CHUNK])
    heavy, light = [], []
    for c, col in enumerate(IN_COLS_QK):
        heavy.append((functools.partial(proj, col), functools.partial(qk_epilogue, c)))
    for c in range(wvat_ref.shape[1] // CHUNK):
        light.append((lambda c=c: _dot_nt(wvat_ref[0, c * CHUNK:(c + 1) * CHUNK, :], hb),
                      functools.partial(vat_epilogue, c)))
    for group, ref in enumerate((vb0_ref, vb1_ref, vb2_ref)):
        light.append((functools.partial(proj, IN_COL_VB + group * CHUNK),
                      functools.partial(store_dilated, ref, group)))
    for c in range((w_ref.shape[2] - IN_COL_GATES) // CHUNK):
        light.append((functools.partial(proj, IN_COL_GATES + c * CHUNK), functools.partial(gate_epilogue, c)))
    jobs = []
    for i in range(max(len(heavy), len(light))):
        jobs += light[i:i + 1] + heavy[i:i + 1]
    pending = None
    for matmul, epilogue in jobs:
        res = matmul()
        if pending is not None:
            pending[0](pending[1])
        pending = (epilogue, res)
    pending[0](pending[1])


def _inproj(x, mod_l, cos_l, sin_l, gain, w_all, wvat_all, seg, layer, tm):
    bsz, seq, d = x.shape
    a_v_w = wvat_all.shape[1]
    tok = lambda w: pl.BlockSpec((1, tm, w), lambda b, i: (b, i, 0))
    tok_t = pl.BlockSpec((1, a_v_w, tm), lambda b, i: (b, 0, i))
    const = lambda shape: pl.BlockSpec(shape, lambda b, i: (0,) * len(shape),
                                       pipeline_mode=pl.Buffered(1))
    of_layer = lambda a: pl.BlockSpec((1,) + a.shape[1:], lambda b, i: (layer, 0, 0),
                                      pipeline_mode=pl.Buffered(1))
    row = lambda w: jax.ShapeDtypeStruct((bsz, seq, w), BF16)
    dil_specs = [pl.BlockSpec((1, tm // dl, dl * B_GROUP_W), lambda b, i: (b, i, 0)) for _, dl in B_PAIRS]
    dil_shapes = [jax.ShapeDtypeStruct((bsz, seq // dl, dl * B_GROUP_W), BF16) for _, dl in B_PAIRS]
    outs = pl.pallas_call(
        _inproj_kernel,
        grid=(bsz, seq // tm),
        in_specs=[
            tok(d),
            pl.BlockSpec((1, 6, d), lambda b, i: (b, 0, 0)),
            tok(LANES), tok(LANES),
            const(gain.shape), of_layer(w_all), of_layer(wvat_all), const(seg.shape),
        ],
        out_specs=[tok(A_QK_W), tok(A_QK_W)] + dil_specs + dil_specs + [tok_t] + dil_specs + [tok(d), tok(d)],
        out_shape=[row(A_QK_W), row(A_QK_W)] + dil_shapes + dil_shapes
                  + [jax.ShapeDtypeStruct((bsz, a_v_w, seq), BF16)] + dil_shapes + [row(d), row(d)],
        scratch_shapes=[pltpu.VMEM((B_GROUP_W // LANES, tm, LANES), F32)],
        compiler_params=pltpu.CompilerParams(dimension_semantics=("parallel", "parallel")),
        name="in_proj",
    )(x, mod_l, cos_l, sin_l, gain, w_all, wvat_all, seg)
    qa, ka = outs[0], outs[1]
    qb, kb, vat, vb, ga, gb = outs[2:5], outs[5:8], outs[8], outs[9:12], outs[12], outs[13]
    return qa, ka, qb, kb, vat, vb, ga, gb


SCORE_BOUND_NO_SHIFT = 64.0
KV_CHUNK = 256


def _attn_a_kernel(bound_ref, lam_ref, sub_ref, q_ref, k_ref, vt_ref, o_ref,
                   m_ref, l_ref, acc_ref, pa_ref, pb_ref, *, tk, lam_init):
    q = q_ref[0]
    tq = q.shape[0]
    seq = k_ref.shape[1]
    lane = lax.broadcasted_iota(jnp.int32, q.shape, 1)
    zero = jnp.zeros_like(q)
    q_maps = (jnp.where(lane < HEAD_DIM, q, zero), jnp.where(lane >= HEAD_DIM, q, zero))
    n_chunk = tk // KV_CHUNK

    acc_ref[...] = jnp.zeros(acc_ref.shape, F32)
    l_ref[...] = jnp.zeros(l_ref.shape, F32)

    def scores_exp(tile, p_ref):
        start = pl.multiple_of(tile * tk, tk)
        k = k_ref[0, pl.ds(start, tk), :]
        for mi in range(2):
            p = jnp.exp2(_dot_nt(k, q_maps[mi]))
            l_ref[mi] += jnp.sum(p.reshape(tk // 8, 8, tq), axis=0)
            p_ref[mi] = p.astype(BF16)

    def weighted_values(tile, p_ref):
        start = pl.multiple_of(tile * tk, tk)
        vt = vt_ref[0, :, pl.ds(start, tk)]
        for mi in range(2):
            acc_ref[mi] += _dot(vt, p_ref[mi])

    n_tiles = seq // tk

    def no_shift_pair(jj, carry):
        t = 2 * jj
        scores_exp(t + 1, pb_ref)
        weighted_values(t, pa_ref)
        scores_exp(t + 2, pa_ref)
        weighted_values(t + 1, pb_ref)
        return carry

    def no_shift_loop():
        scores_exp(0, pa_ref)
        lax.fori_loop(0, n_tiles // 2 - 1, no_shift_pair, 0)
        scores_exp(n_tiles - 1, pb_ref)
        weighted_values(n_tiles - 2, pa_ref)
        weighted_values(n_tiles - 1, pb_ref)

    def online_max_body(j, carry):
        for c in range(n_chunk):
            start = pl.multiple_of(j * tk + c * KV_CHUNK, KV_CHUNK)
            k = k_ref[0, pl.ds(start, KV_CHUNK), :]
            vt = vt_ref[0, :, pl.ds(start, KV_CHUNK)]
            for mi in range(2):
                s = _dot_nt(k, q_maps[mi])
                m_old = m_ref[mi]
                m_new = jnp.maximum(m_old, jnp.max(s, axis=0, keepdims=True))
                alpha = jnp.exp2(m_old - m_new)
                p = jnp.exp2(s - m_new[0:1])
                l_ref[mi] = alpha * l_ref[mi] + jnp.sum(p.reshape(KV_CHUNK // 8, 8, tq), axis=0)
                acc_ref[mi] = alpha[0:1] * acc_ref[mi] + _dot(vt, p.astype(BF16))
                m_ref[mi] = m_new
        return carry

    no_shift = bound_ref[0] <= SCORE_BOUND_NO_SHIFT

    @pl.when(no_shift)
    def _():
        no_shift_loop()

    @pl.when(jnp.logical_not(no_shift))
    def _():
        m_ref[...] = jnp.full(m_ref.shape, -jnp.inf, F32)
        lax.fori_loop(0, seq // tk, online_max_body, 0)

    lam_p = lam_ref[...]
    s1 = jnp.sum(lam_p[0:1] * lam_p[1:2], axis=-1, keepdims=True)
    s2 = jnp.sum(lam_p[2:3] * lam_p[3:4], axis=-1, keepdims=True)
    lam = jnp.exp(s1) - jnp.exp(s2) + lam_init
    l0 = jnp.sum(l_ref[0], axis=0, keepdims=True)
    l1 = jnp.sum(l_ref[1], axis=0, keepdims=True)
    ot = acc_ref[0] / l0 - lam * (acc_ref[1] / l1)
    o = ot.T
    msq = jnp.mean(o * o, axis=-1, keepdims=True)
    o = o * lax.rsqrt(msq + EPS) * sub_ref[...] * (1.0 - lam_init)
    o_ref[0] = o.astype(BF16)


def _attn_a(score_bound, lam_p, subln, qa, ka, vat, lam_init, tq, tk):
    bsz, seq, _ = qa.shape
    return pl.pallas_call(
        functools.partial(_attn_a_kernel, tk=tk, lam_init=lam_init),
        grid=(bsz, A_HEADS, seq // tq),
        in_specs=[
            pl.BlockSpec(memory_space=pltpu.SMEM),
            pl.BlockSpec(lam_p.shape, lambda b, h, i: (0, 0)),
            pl.BlockSpec(subln.shape, lambda b, h, i: (0, 0)),
            pl.BlockSpec((1, tq, LANES), lambda b, h, i: (b, i, h)),
            pl.BlockSpec((1, seq, LANES), lambda b, h, i: (b, 0, h)),
            pl.BlockSpec((1, LANES, seq), lambda b, h, i: (b, h, 0)),
        ],
        out_specs=pl.BlockSpec((1, tq, LANES), lambda b, h, i: (b, i, h)),
        out_shape=jax.ShapeDtypeStruct((bsz, seq, A_HEADS * LANES), BF16),
        scratch_shapes=[
            pltpu.VMEM((2, 8, tq), F32),
            pltpu.VMEM((2, 8, tq), F32),
            pltpu.VMEM((2, LANES, tq), F32),
            pltpu.VMEM((2, tk, tq), BF16),
            pltpu.VMEM((2, tk, tq), BF16),
        ],
        compiler_params=pltpu.CompilerParams(dimension_semantics=("parallel", "parallel", "parallel")),
        name="diff_attn",
    )(score_bound, lam_p, subln, qa, ka, vat)


def _attn_b_kernel(q_ref, k_ref, v_ref, o_ref, lse_ref, *, radius):
    length = k_ref.shape[1]
    sub = 2 * radius
    win = sub + 2 * radius
    n_sub = q_ref.shape[1] // sub
    lane = lax.broadcasted_iota(jnp.int32, (sub, LANES), 1)
    low = lane < HEAD_DIM
    tile = lambda a, c: a[:, c * LANES:(c + 1) * LANES]

    units, windows, scores = [], [], []
    for sb in range(n_sub):
        t0 = pl.program_id(2) * (n_sub * sub) + sb * sub
        start = pl.multiple_of(jnp.clip(t0 - radius, 0, length - win), radius)
        kw = k_ref[0, pl.ds(start, win), :]
        windows.append(v_ref[0, pl.ds(start, win), :])
        q = q_ref[0, sb * sub:(sb + 1) * sub, :]
        qpos = t0 + lax.broadcasted_iota(jnp.int32, (sub, win), 0)
        kpos = start + lax.broadcasted_iota(jnp.int32, (sub, win), 1)
        valid = jnp.abs(kpos - qpos) <= radius
        for c in range(B_GROUP_W // LANES):
            for half in range(2):
                qc = tile(q, c)
                qm = jnp.where(low if half == 0 else jnp.logical_not(low), qc, jnp.zeros_like(qc))
                units.append((sb, c))
                scores.append(jnp.where(valid, _dot_nt(qm, tile(kw, c)), -jnp.inf))
    probs, sums, lses = [], [], []
    for s in scores:
        m = jnp.max(s, axis=-1, keepdims=True)
        p = jnp.exp2(s - m)
        l = jnp.sum(p, axis=-1, keepdims=True)
        probs.append(p.astype(BF16))
        sums.append(l)
        lses.append((m + jnp.log2(l)) * LN2)
    outs = [_dot(probs[i], tile(windows[sb], c)) / sums[i] for i, (sb, c) in enumerate(units)]
    for i in range(0, len(units), 2):
        sb, c = units[i]
        rows, cols = slice(sb * sub, (sb + 1) * sub), slice(c * LANES, (c + 1) * LANES)
        o_ref[0, rows, cols] = jnp.where(low, outs[i], outs[i + 1]).astype(BF16)
        lse_ref[0, rows, cols] = jnp.where(low, lses[i], lses[i + 1])


def _attn_b(qg, kg, vg, group, tq):
    window, dilation = B_PAIRS[group]
    radius = window // (2 * dilation)
    bsz, length, _ = qg.shape
    tq = min(tq, length)
    assert length >= 4 * radius and tq % (2 * radius) == 0
    return pl.pallas_call(
        functools.partial(_attn_b_kernel, radius=radius),
        grid=(bsz, dilation, length // tq),
        in_specs=[
            pl.BlockSpec((1, tq, B_GROUP_W), lambda b, r, i: (b, i, r)),
            pl.BlockSpec((1, length, B_GROUP_W), lambda b, r, i: (b, 0, r)),
            pl.BlockSpec((1, length, B_GROUP_W), lambda b, r, i: (b, 0, r)),
        ],
        out_specs=[pl.BlockSpec((1, tq, B_GROUP_W), lambda b, r, i: (b, i, r))] * 2,
        out_shape=[jax.ShapeDtypeStruct((bsz, length, dilation * B_GROUP_W), BF16),
                   jax.ShapeDtypeStruct((bsz, length, dilation * B_GROUP_W), F32)],
        compiler_params=pltpu.CompilerParams(dimension_semantics=("parallel", "parallel", "parallel")),
        name=f"band_attn_g{group}",
    )(qg, kg, vg)


def _merge_kernel(x_ref, mod_ref, oa_ref, ob0_ref, ls0_ref, ob1_ref, ls1_ref, ob2_ref, ls2_ref,
                  ga_ref, gb_ref, wpa_ref, wpb_ref, wo_ref, wr2_ref, br_ref,
                  x1_ref, h2_ref, route_ref, so1_ref, sl1_ref, so2_ref, sl2_ref):
    tm = x_ref.shape[1]

    def token_major(o_ref, l_ref, group, so_ref, sl_ref):
        dil = B_PAIRS[group][1]
        n_hf = B_GROUP_W // LANES
        for r in range(dil):
            for hf in range(n_hf):
                cols = slice(r * B_GROUP_W + hf * LANES, r * B_GROUP_W + (hf + 1) * LANES)
                so_ref[hf, pl.ds(r, tm // dil, stride=dil), :] = o_ref[0, :, cols].astype(F32)
                sl_ref[hf, pl.ds(r, tm // dil, stride=dil), :] = l_ref[0, :, cols]
        return (jnp.concatenate([so_ref[hf] for hf in range(n_hf)], axis=1),
                jnp.concatenate([sl_ref[hf] for hf in range(n_hf)], axis=1))

    o0, ls0 = ob0_ref[0].astype(F32), ls0_ref[0]
    o1, ls1 = token_major(ob1_ref, ls1_ref, 1, so1_ref, sl1_ref)
    o2, ls2 = token_major(ob2_ref, ls2_ref, 2, so2_ref, sl2_ref)
    mx = jnp.maximum(jnp.maximum(ls0, ls1), ls2)
    e0, e1, e2 = jnp.exp(ls0 - mx), jnp.exp(ls1 - mx), jnp.exp(ls2 - mx)
    ob = ((e0 * o0 + e1 * o1 + e2 * o2) / (e0 + e1 + e2)).astype(BF16)

    n_part = 2
    rows = [slice(p * (tm // n_part), (p + 1) * (tm // n_part)) for p in range(n_part)]
    pa = [_dot(oa_ref[0, r, :], wpa_ref[...]) for r in rows]
    pb = [_dot(ob[r], wpb_ref[...]) for r in rows]
    merged = [(ga_ref[0, r, :].astype(F32) * pa[p] + gb_ref[0, r, :].astype(F32) * pb[p]).astype(BF16)
              for p, r in enumerate(rows)]
    y = [_dot(merged[p], wo_ref[...]) for p in range(n_part)]
    for p, r in enumerate(rows):
        x1 = x_ref[0, r, :] + mod_ref[0, 2:3, :] * y[p]
        x1_ref[0, r, :] = x1
        ms = jnp.mean(x1 * x1, axis=-1, keepdims=True)
        h2 = x1 * lax.rsqrt(ms + EPS) * (1.0 + mod_ref[0, 4:5, :]) + mod_ref[0, 3:4, :]
        h2_hi = h2.astype(BF16)
        h2_ref[0, r, :] = h2_hi
        h2_lo = (h2 - h2_hi.astype(F32)).astype(BF16)
        both = _dot(h2_hi, wr2_ref[...])
        logits = (both[:, :ROUTER_W] + both[:, ROUTER_W:] + _dot(h2_lo, wr2_ref[:, :ROUTER_W])) + br_ref[...]
        route_ref[0, r, :] = _route(logits)


def _route(logits):
    lane = lax.broadcasted_iota(jnp.int32, logits.shape, 1)
    neg = -jnp.inf
    big = ROUTER_W
    is_grp = (lane >= N_EXPERTS) & (lane < N_EXPERTS + N_GROUPS)
    lg = jnp.where(is_grp, logits, neg)
    mg = jnp.max(lg, axis=-1, keepdims=True)
    g_lane = jnp.min(jnp.where(lg == mg, lane, big), axis=-1, keepdims=True)
    g_val = 1.0 / jnp.sum(jnp.exp(lg - mg), axis=-1, keepdims=True)
    lo = (g_lane - N_EXPERTS) * EXPERTS_PER_GROUP
    in_grp = (lane >= lo) & (lane < lo + EXPERTS_PER_GROUP)
    le = jnp.where(in_grp, logits, neg)
    m1 = jnp.max(le, axis=-1, keepdims=True)
    i1 = jnp.min(jnp.where(le == m1, lane, big), axis=-1, keepdims=True)
    le2 = jnp.where(lane == i1, neg, le)
    m2 = jnp.max(le2, axis=-1, keepdims=True)
    i2 = jnp.min(jnp.where(le2 == m2, lane, big), axis=-1, keepdims=True)
    e = jnp.exp(m2 - m1)
    w1 = g_val / (1.0 + e)
    w2 = g_val * e / (1.0 + e)
    return jnp.where(lane == 0, i1.astype(F32),
                     jnp.where(lane == 1, i2.astype(F32),
                               jnp.where(lane == 2, w1, jnp.where(lane == 3, w2, 0.0))))


def _merge(x, mod_l, oa, obs, ga, gb, wpa, wpb, wo, wr, br, tm):
    bsz, seq, d = x.shape
    tok = lambda w: pl.BlockSpec((1, tm, w), lambda b, i: (b, i, 0))
    const = lambda shape: pl.BlockSpec(shape, lambda b, i: (0,) * len(shape),
                                       pipeline_mode=pl.Buffered(1))
    ob_args, ob_specs = [], []
    for (o, lse), (_, dl) in zip(obs, B_PAIRS):
        ob_args += [o, lse]
        ob_specs += [pl.BlockSpec((1, tm // dl, dl * B_GROUP_W), lambda b, i: (b, i, 0))] * 2
    return pl.pallas_call(
        _merge_kernel,
        grid=(bsz, seq // tm),
        in_specs=[tok(d), pl.BlockSpec((1, 6, d), lambda b, i: (b, 0, 0)), tok(oa.shape[-1])]
                 + ob_specs + [tok(d), tok(d)]
                 + [const(w.shape) for w in (wpa, wpb, wo, wr, br)],
        out_specs=[tok(d), tok(d), tok(ROUTER_W)],
        out_shape=[jax.ShapeDtypeStruct((bsz, seq, d), F32),
                   jax.ShapeDtypeStruct((bsz, seq, d), BF16),
                   jax.ShapeDtypeStruct((bsz, seq, ROUTER_W), F32)],
        scratch_shapes=[pltpu.VMEM((B_GROUP_W // LANES, tm, LANES), F32)] * 4,
        compiler_params=pltpu.CompilerParams(dimension_semantics=("parallel", "parallel")),
        name="merge_proj",
    )(x, mod_l, oa, *ob_args, ga, gb, wpa, wpb, wo, wr, br)


MOE_BLOCK = 1024
SEG_ALIGN = 16
ROW_CHUNK = 128
PERM_CHUNK = 256
GATHER_CHUNK = 512
EXPERTS_PER_STEP = 8
SORTED_ROWS = -(-(2 * MOE_BLOCK + N_EXPERTS * (SEG_ALIGN - 1)) // GATHER_CHUNK) * GATHER_CHUNK
SORTED_ROWS_ALLOC = SORTED_ROWS + ROW_CHUNK


def _plan_kernel(route_ref, before_ref, posc_ref, posr_ref, base_ref, npad_ref):
    r = route_ref[0]
    lane = lax.broadcasted_iota(jnp.int32, r.shape, 1).astype(F32)
    oh1 = jnp.where(lane == r[:, 0:1], 1.0, 0.0)
    oh2 = jnp.where(lane == r[:, 1:2], 1.0, 0.0)
    cnt1 = jnp.sum(oh1, axis=0, keepdims=True)
    cnt2 = jnp.sum(oh2, axis=0, keepdims=True)
    npad = jnp.floor((cnt1 + cnt2 + (SEG_ALIGN - 1)) * (1.0 / SEG_ALIGN)) * SEG_ALIGN
    ri = lax.broadcasted_iota(jnp.int32, (LANES, LANES), 0)
    ci = lax.broadcasted_iota(jnp.int32, (LANES, LANES), 1)
    upper = jnp.where(ri < ci, 1.0, 0.0)
    npad8 = jnp.broadcast_to(npad, (8, LANES))
    base = jnp.dot(npad8, upper, precision=HIGHEST, preferred_element_type=F32)[0:1]

    pre = _dot(before_ref[...], jnp.concatenate([oh1, oh2], axis=1).astype(BF16))
    pre1, pre2 = pre[:, :LANES], pre[:, LANES:]
    pos1 = jnp.sum(oh1 * (base + pre1), axis=-1, keepdims=True)
    pos2 = jnp.sum(oh2 * (base + cnt1 + pre2), axis=-1, keepdims=True)
    packed = jnp.where(lane == 0.0, pos1, jnp.where(lane == 1.0, pos2, jnp.where(lane >= 2.0, r, 0.0)))
    posc_ref[0] = packed
    posr_ref[0] = packed.T[0:8, :]
    base_ref[0] = base
    npad_ref[0] = npad


def _moe_plan(route):
    n_blk, tb, _ = route.shape
    vec = pl.BlockSpec((1, 1, LANES), lambda i: (i, 0, 0))
    before = jnp.tril(jnp.ones((tb, tb), BF16), -1)
    return pl.pallas_call(
        _plan_kernel,
        grid=(n_blk,),
        in_specs=[pl.BlockSpec((1, tb, ROUTER_W), lambda i: (i, 0, 0)),
                  pl.BlockSpec((tb, tb), lambda i: (0, 0), pipeline_mode=pl.Buffered(1))],
        out_specs=[pl.BlockSpec((1, tb, LANES), lambda i: (i, 0, 0)),
                   pl.BlockSpec((1, 8, tb), lambda i: (i, 0, 0)), vec, vec],
        out_shape=[jax.ShapeDtypeStruct((n_blk, tb, LANES), F32),
                   jax.ShapeDtypeStruct((n_blk, 8, tb), F32),
                   jax.ShapeDtypeStruct((n_blk, 1, LANES), F32),
                   jax.ShapeDtypeStruct((n_blk, 1, LANES), F32)],
        compiler_params=pltpu.CompilerParams(dimension_semantics=("parallel",)),
        name="moe_plan",
    )(route, before)


def _moe_kernel(base_ref, npad_ref, x1_ref, mod_ref, h_ref, posc_ref, posr_ref, wg_ref, wu_ref, wd_ref,
                o_ref, xs_ref, gathered_ref):
    blk = pl.program_id(0)
    e = pl.program_id(1)
    tb = h_ref.shape[1]

    n_here = wg_ref.shape[0]
    n_experts = pl.num_programs(1) * n_here

    @pl.when(e == 0)
    def _():
        gathered_ref[0] = 0
        xs_ref[SORTED_ROWS:, :] = jnp.zeros((ROW_CHUNK, xs_ref.shape[1]), BF16)

    nxt = jnp.minimum((e + 2) * n_here, n_experts)
    need_rows = jnp.minimum(base_ref[blk, nxt] + ROW_CHUNK, SORTED_ROWS)
    need_rows = jnp.where(nxt == n_experts, SORTED_ROWS, need_rows)
    need_chunks = (need_rows + GATHER_CHUNK - 1) // GATHER_CHUNK

    def gather_chunk(c, carry):
        r0 = pl.multiple_of(c * GATHER_CHUNK, GATHER_CHUNK)
        rid = (r0 + lax.broadcasted_iota(jnp.int32, (GATHER_CHUNK, tb), 0)).astype(F32)
        pr = posr_ref[0]
        sel = jnp.where(rid == pr[0:1], 1.0, jnp.where(rid == pr[1:2], 1.0, 0.0)).astype(BF16)
        xs_ref[pl.ds(r0, GATHER_CHUNK), :] = _dot(sel, h_ref[0]).astype(BF16)
        return carry

    lax.fori_loop(gathered_ref[0], need_chunks, gather_chunk, 0)
    gathered_ref[0] = jnp.maximum(gathered_ref[0], need_chunks)

    def gate_up(r0, j):
        xc = xs_ref[pl.ds(r0, ROW_CHUNK), :]
        return _dot(xc, wg_ref[j]), _dot(xc, wu_ref[j])

    row_in_chunk = lax.broadcasted_iota(jnp.int32, (ROW_CHUNK, 1), 0)

    def down(r0, j, valid_rows, a, u):
        hid = (a * _sigmoid(a)) * u
        out = _dot(hid.astype(BF16), wd_ref[j]).astype(BF16)
        old = xs_ref[pl.ds(r0, ROW_CHUNK), :]
        xs_ref[pl.ds(r0, ROW_CHUNK), :] = jnp.where(row_in_chunk < valid_rows, out, old)

    experts = [e * n_here + j for j in range(n_here)]
    starts = [pl.multiple_of(base_ref[blk, ex], SEG_ALIGN) for ex in experts]
    seg_rows = [npad_ref[blk, ex] for ex in experts]
    first = [gate_up(starts[j], j) for j in range(n_here)]
    for j in range(n_here):
        down(starts[j], j, seg_rows[j], *first[j])
    for j in range(n_here):
        def more(c, carry, j=j):
            r0 = pl.multiple_of(starts[j] + c * ROW_CHUNK, SEG_ALIGN)
            down(r0, j, seg_rows[j] - c * ROW_CHUNK, *gate_up(r0, j))
            return carry

        lax.fori_loop(1, (seg_rows[j] + ROW_CHUNK - 1) // ROW_CHUNK, more, 0)

    @pl.when(e == pl.num_programs(1) - 1)
    def _():
        ys = xs_ref[:SORTED_ROWS, :]
        for c in range(tb // PERM_CHUNK):
            pc = posc_ref[0, c * PERM_CHUNK:(c + 1) * PERM_CHUNK, :]
            rid = lax.broadcasted_iota(jnp.int32, (PERM_CHUNK, SORTED_ROWS), 1).astype(F32)
            wsel = (jnp.where(rid == pc[:, 0:1], pc[:, 2:3], 0.0)
                    + jnp.where(rid == pc[:, 1:2], pc[:, 3:4], 0.0)).astype(BF16)
            y = _dot(wsel, ys)
            rows = slice(c * PERM_CHUNK, (c + 1) * PERM_CHUNK)
            o_ref[0, rows, :] = x1_ref[0, rows, :] + mod_ref[0, 5:6, :] * y


def _moe(x1, mod_l, h2, route, weg, weu, wed, layer):
    bsz, seq, d = x1.shape
    n_e, d_e = N_EXPERTS, weg.shape[-1]
    e0 = layer * (n_e // EXPERTS_PER_STEP)
    tb = min(MOE_BLOCK, seq)
    assert tb == MOE_BLOCK, "sorted-row capacity is sized for MOE_BLOCK tokens"
    per_batch = seq // tb
    n_blk = bsz * per_batch
    blocked = lambda a: a.reshape(n_blk, tb, a.shape[-1])
    posc, posr, base, npad = _moe_plan(blocked(route))
    base_i = base.reshape(n_blk, LANES).astype(jnp.int32)
    npad_i = npad.reshape(n_blk, LANES).astype(jnp.int32)

    tok = lambda w: pl.BlockSpec((1, tb, w), lambda i, e, *_: (i, 0, 0))
    grid_spec = pltpu.PrefetchScalarGridSpec(
        num_scalar_prefetch=2,
        grid=(n_blk, n_e // EXPERTS_PER_STEP),
        in_specs=[tok(d),
                  pl.BlockSpec((1, 6, d), lambda i, e, *_: (i // per_batch, 0, 0)),
                  tok(d), tok(LANES),
                  pl.BlockSpec((1, 8, tb), lambda i, e, *_: (i, 0, 0)),
                  pl.BlockSpec((EXPERTS_PER_STEP, d, d_e), lambda i, e, *_: (e0 + e, 0, 0)),
                  pl.BlockSpec((EXPERTS_PER_STEP, d, d_e), lambda i, e, *_: (e0 + e, 0, 0)),
                  pl.BlockSpec((EXPERTS_PER_STEP, d_e, d), lambda i, e, *_: (e0 + e, 0, 0))],
        out_specs=tok(d),
        scratch_shapes=[pltpu.VMEM((SORTED_ROWS_ALLOC, d), BF16),
                        pltpu.SMEM((1,), jnp.int32)],
    )
    out = pl.pallas_call(
        _moe_kernel,
        grid_spec=grid_spec,
        out_shape=jax.ShapeDtypeStruct((n_blk, tb, d), F32),
        compiler_params=pltpu.CompilerParams(dimension_semantics=("parallel", "arbitrary")),
        name="moe_experts",
    )(base_i, npad_i, blocked(x1), mod_l, blocked(h2), posc, posr, weg, weu, wed)
    return out.reshape(bsz, seq, d)


def _tiles(seq):
    return dict(
        ts_rope=min(1024, seq),
        tm_proj=min(512, seq),
        tq_a=min(2048, seq),
        tk_a=min(1024, seq),
        tq_b=512,
        tm_merge=min(512, seq),
    )


def kernel(x, c, positions, w_ada, b_ada, w_in, qn_a, kn_a, lam_q1, lam_k1, lam_q2, lam_k2,
           subln_a, qn_b, kn_b, w_pa, w_pb, w_o, w_r1, b_r1, w_r2, b_r2,
           w_e_gate, w_e_up, w_e_down):
    depth = w_ada.shape[0]
    bsz, seq, d = x.shape
    t = _tiles(seq)

    mod = _ada(c, w_ada, b_ada).reshape(depth, bsz, 6, d)
    cos_l, sin_l = _rope_tables(positions, t["ts_rope"])
    seg = jnp.kron(jnp.eye(CHUNK // HEAD_DIM, dtype=F32),
                   jnp.full((HEAD_DIM, HEAD_DIM), 1.0 / HEAD_DIM, F32)).astype(BF16)
    q_scale = HEAD_DIM ** -0.5 * LOG2E

    w_in_b = w_in.astype(BF16)
    wvat_b = w_in[:, :, IN_COL_VA:IN_COL_VA + A_V_W].transpose(0, 2, 1).astype(BF16)
    experts = lambda a: a.astype(BF16).reshape((-1,) + a.shape[2:])
    weg_b, weu_b, wed_b = experts(w_e_gate), experts(w_e_up), experts(w_e_down)

    for layer in range(depth):
        lam_init = 0.8 - 0.6 * math.exp(-0.3 * layer)
        gain = jnp.concatenate([
            jnp.tile(qn_a[layer] * q_scale, 8), jnp.tile(kn_a[layer], 8),
            jnp.tile(qn_b[layer] * q_scale, 12), jnp.tile(kn_b[layer], 12)]).reshape(1, -1)

        qa, ka, qb, kb, vat, vb, ga, gb = _inproj(x, mod[layer], cos_l, sin_l, gain, w_in_b, wvat_b, seg,
                                                  layer, t["tm_proj"])

        score_bound = (1.01 * HEAD_DIM * q_scale * jnp.max(jnp.abs(qn_a[layer]))
                       * jnp.max(jnp.abs(kn_a[layer]))).reshape(1)
        lam_p = jnp.stack([lam_q1[layer], lam_k1[layer], lam_q2[layer], lam_k2[layer]])
        oa = _attn_a(score_bound, lam_p, subln_a[layer].reshape(1, -1), qa, ka, vat, lam_init,
                     t["tq_a"], t["tk_a"])
        obs = [_attn_b(qb[g], kb[g], vb[g], g, t["tq_b"]) for g in range(B_GROUPS)]

        wr = jnp.zeros((d, ROUTER_W), F32)
        wr = wr.at[:, :N_EXPERTS].set(w_r2[layer]).at[:, N_EXPERTS:N_EXPERTS + N_GROUPS].set(w_r1[layer])
        br = jnp.zeros((1, ROUTER_W), F32)
        br = br.at[0, :N_EXPERTS].set(b_r2[layer]).at[0, N_EXPERTS:N_EXPERTS + N_GROUPS].set(b_r1[layer])
        wr_hi = wr.astype(BF16)
        wr2 = jnp.concatenate([wr_hi, (wr - wr_hi.astype(F32)).astype(BF16)], axis=1)
        x1, h2, route = _merge(x, mod[layer], oa, obs, ga, gb,
                               w_pa[layer].astype(BF16), w_pb[layer].astype(BF16), w_o[layer].astype(BF16),
                               wr2, br, t["tm_merge"])

        x = _moe(x1, mod[layer], h2, route, weg_b, weu_b, wed_b, layer)
    return x
```
